```python
import math
import jax
import jax.numpy as jnp
import numpy as np

D_MODEL = 1024
BATCH = 4
SEQ = 4096
DEPTH = 2
DEC_BATCH = 128
DEC_SEQ = 1
PAST_LEN = 8192
PAGE_SIZE = 128

HEAD_DIM = 64
ROT_DIM = HEAD_DIM // 4
ROPE_THETA = 500000.0
N_A_LAYERS = DEPTH // 2
N_B_LAYERS = DEPTH - N_A_LAYERS
H_A = D_MODEL // HEAD_DIM
KVH_A = H_A // 4
WIN_A = 128
B_GROUPS = ((128, 1), (512, 4), (2048, 16))
N_BG = len(B_GROUPS)
H_B = D_MODEL // HEAD_DIM
KVH_B = H_B // 4
D_FF = 256 * ((8 * D_MODEL // 3 + 255) // 256)
EPS = 1e-6

kernel_name = 'yoco_swa_sink_dilated_macaron_step'


def _rmsnorm(x, g):
    xf = x.astype(jnp.float32)
    y = xf * jax.lax.rsqrt(jnp.mean(xf * xf, -1, keepdims=True) + EPS) * g.astype(jnp.float32)
    return y.astype(x.dtype)


def _swiglu(h, w_gu, w_dn):
    gate, up = jnp.split(h @ w_gu, 2, axis=-1)
    return (jax.nn.silu(gate) * up) @ w_dn


def _rope_tables(pos):
    inv = ROPE_THETA ** (-jnp.arange(0, ROT_DIM, 2, dtype=jnp.float32) / ROT_DIM)
    ang = pos[:, None] * inv[None, :]
    return jnp.cos(ang), jnp.sin(ang)


def _rope(x, cos, sin):
    shp = cos.shape[:1] + (1,) * (x.ndim - 3) + cos.shape[1:]
    c, s = cos.reshape(shp), sin.reshape(shp)
    x = x.astype(jnp.float32)
    half = ROT_DIM // 2
    x1, x2 = x[..., :half], x[..., half:ROT_DIM]
    return jnp.concatenate([x1 * c - x2 * s, x2 * c + x1 * s, x[..., ROT_DIM:]], axis=-1)


def _attend(s, sink):
    m = jnp.max(s, -1, keepdims=True)
    if sink is not None:
        m = jnp.maximum(m, sink)
    p = jnp.exp(s - m)
    l = jnp.sum(p, -1, keepdims=True)
    if sink is not None:
        l = l + jnp.exp(sink - m)
    return p / l, (m + jnp.log(l))[..., 0]


def _last_rows(kv, w):
    pad = [(0, 0), (w, 0)] + [(0, 0)] * (kv.ndim - 2)
    return jnp.pad(kv, pad)[:, -w:]


def _banded(q, k, v, win, sink):
    n, length, kvh, grp, hd = q.shape
    bq = math.gcd(length, win)
    nb = length // bq
    nk = bq + win
    pad = ((0, 0), (win, 0), (0, 0), (0, 0))
    idx = jnp.arange(nb)[:, None] * bq + jnp.arange(nk)[None, :]
    kb = jnp.pad(k.astype(jnp.float32), pad)[:, idx]
    vb = jnp.pad(v.astype(jnp.float32), pad)[:, idx]
    qb = q.reshape(n, nb, bq, kvh, grp, hd)
    s = jnp.einsum('nbqkgd,nbjkd->nbkgqj', qb, kb)
    i = jnp.arange(bq)[:, None]
    j = jnp.arange(nk)[None, :]
    dist = i - j + win
    kpos = jnp.arange(nb)[:, None, None] * bq + j[None] - win
    valid = (dist >= 0) & (dist <= win) & (kpos >= 0)
    s = jnp.where(valid[None, :, None, None], s, -jnp.inf)
    p, lse = _attend(s, None if sink is None else sink[:, :, None, None])
    o = jnp.einsum('nbkgqj,nbjkd->nbqkgd', p, vb).reshape(n, length, kvh, grp, hd)
    lse = lse.transpose(0, 1, 4, 2, 3).reshape(n, length, kvh, grp)
    return o, lse


def _dilated_prompt(q, k, v, win, dil):
    b, s_len = q.shape[:2]
    length = s_len // dil

    def fold(t):
        t = t.reshape((b, length, dil) + t.shape[2:])
        return jnp.swapaxes(t, 1, 2).reshape((b * dil, length) + t.shape[3:])

    def unfold(t):
        t = t.reshape((b, dil, length) + t.shape[2:])
        return jnp.swapaxes(t, 1, 2).reshape((b, s_len) + t.shape[3:])

    o, lse = _banded(fold(q), fold(k), fold(v), win // dil, None)
    return unfold(o), unfold(lse)


def _strided_decode(q, kv_all, win, dil, sink):
    ds = q.shape[1]
    taps = win // dil + 1
    idx = win + jnp.arange(ds)[:, None] - jnp.arange(taps)[None, :] * dil
    valid = idx + (PAST_LEN - win) >= 0
    g = kv_all[:, idx].astype(jnp.float32)
    s = jnp.einsum('bqkgd,bqjkd->bqkgj', q, g[:, :, :, 0])
    s = jnp.where(valid[None, :, None, None, :], s, -jnp.inf)
    p, lse = _attend(s, None if sink is None else sink[:, :, None])
    o = jnp.einsum('bqkgj,bqjkd->bqkgd', p, g[:, :, :, 1])
    return o, lse


def _mixer_a(h, w_qkv, w_o, sink, cos, sin, buf):
    b, s_len = h.shape[:2]
    grp = H_A // KVH_A
    qkv = h @ w_qkv
    q = qkv[..., :H_A * HEAD_DIM].reshape(b, s_len, KVH_A, grp, HEAD_DIM)
    k = qkv[..., H_A * HEAD_DIM:(H_A + KVH_A) * HEAD_DIM].reshape(b, s_len, KVH_A, HEAD_DIM)
    v = qkv[..., (H_A + KVH_A) * HEAD_DIM:].reshape(b, s_len, KVH_A, HEAD_DIM)
    q = _rope(q, cos, sin) * HEAD_DIM ** -0.5
    kv = jnp.stack([_rope(k, cos, sin).astype(h.dtype), v], axis=2)
    sk = sink.astype(jnp.float32).reshape(KVH_A, grp)
    if buf is None:
        o, _ = _banded(q, kv[:, :, 0], kv[:, :, 1], WIN_A, sk)
        new_buf = _last_rows(kv, WIN_A)
    else:
        kv_all = jnp.concatenate([buf.astype(kv.dtype), kv], axis=1)
        o, _ = _strided_decode(q, kv_all, WIN_A, 1, sk)
        new_buf = kv_all[:, -WIN_A:]
    return o.reshape(b, s_len, H_A * HEAD_DIM).astype(h.dtype) @ w_o, new_buf


def _shared_kv_b(x, g_kv, w_kv, cos, sin, bufs):
    b, s_len = x.shape[:2]
    kv = (_rmsnorm(x, g_kv) @ w_kv).reshape(b, s_len, N_BG, 2, KVH_B, HEAD_DIM)
    k = _rope(kv[:, :, :, 0], cos, sin).astype(x.dtype)
    kv = jnp.stack([k, kv[:, :, :, 1]], axis=3)
    srcs, new = [], []
    for gi, (win, _) in enumerate(B_GROUPS):
        kv_g = kv[:, :, gi]
        if bufs is None:
            srcs.append(kv_g)
            new.append(_last_rows(kv_g, win))
        else:
            src = jnp.concatenate([bufs[gi].astype(kv_g.dtype), kv_g], axis=1)
            srcs.append(src)
            new.append(src[:, -win:])
    return srcs, new


def _mixer_b(h, w_q, w_o, srcs, cos, sin, decode):
    b, s_len = h.shape[:2]
    grp = H_B // KVH_B
    q_all = (h @ w_q).reshape(b, s_len, N_BG, KVH_B, grp, HEAD_DIM)
    outs, lses = [], []
    for gi, (win, dil) in enumerate(B_GROUPS):
        q = _rope(q_all[:, :, gi], cos, sin) * HEAD_DIM ** -0.5
        if decode:
            o, lse = _strided_decode(q, srcs[gi], win, dil, None)
        else:
            o, lse = _dilated_prompt(q, srcs[gi][:, :, 0], srcs[gi][:, :, 1], win, dil)
        outs.append(o)
        lses.append(lse)
    wts = jax.nn.softmax(jnp.stack(lses, 0), axis=0)
    o = jnp.sum(wts[..., None] * jnp.stack(outs, 0), axis=0)
    return o.reshape(b, s_len, H_B * HEAD_DIM).astype(h.dtype) @ w_o


def _trunk(x, cos, sin, a_bufs, b_bufs, norm_g, w_ffn_gu, w_ffn_dn, w_qkv_a, sink_a, w_o_a,
           g_kv_b, w_kv_b, w_q_b, w_o_b):
    decode = b_bufs is not None
    new_a, srcs, new_b = [], None, None
    for l in range(DEPTH):
        x = x + 0.5 * _rmsnorm(_swiglu(_rmsnorm(x, norm_g[l, 0]), w_ffn_gu[l, 0], w_ffn_dn[l, 0]), norm_g[l, 1])
        h = _rmsnorm(x, norm_g[l, 2])
        if l < N_A_LAYERS:
            mix, nbuf = _mixer_a(h, w_qkv_a[l], w_o_a[l], sink_a[l], cos, sin,
                                 a_bufs[l] if decode else None)
            new_a.append(nbuf)
        else:
            if l == N_A_LAYERS:
                srcs, new_b = _shared_kv_b(x, g_kv_b, w_kv_b, cos, sin, b_bufs)
            mix = _mixer_b(h, w_q_b[l - N_A_LAYERS], w_o_b[l - N_A_LAYERS], srcs, cos, sin, decode)
        x = x + _rmsnorm(mix, norm_g[l, 3])
        x = x + 0.5 * _rmsnorm(_swiglu(_rmsnorm(x, norm_g[l, 4]), w_ffn_gu[l, 1], w_ffn_dn[l, 1]), norm_g[l, 5])
    return x, jnp.stack(new_a, 0), new_b


def setup_inputs(seed: int = 0) -> dict:
    key = jax.random.key(seed)
    ks = jax.random.split(key, 20)

    def nrm(k, shape, scale):
        return jax.random.normal(k, shape, jnp.float32) * scale

    return {
        'x_prompt': nrm(ks[0], (BATCH, SEQ, D_MODEL), 1.0),
        'x_sample': nrm(ks[1], (DEC_BATCH, DEC_SEQ, D_MODEL), 1.0),
        'cache_a_kv': nrm(ks[2], (N_A_LAYERS, DEC_BATCH, WIN_A, 2, KVH_A, HEAD_DIM), 1.0),
        'cache_b_kv_w128': nrm(ks[3], (DEC_BATCH, B_GROUPS[0][0], 2, KVH_B, HEAD_DIM), 1.0),
        'cache_b_kv_w512': nrm(ks[4], (DEC_BATCH, B_GROUPS[1][0], 2, KVH_B, HEAD_DIM), 1.0),
        'cache_b_kv_w2048': nrm(ks[5], (DEC_BATCH, B_GROUPS[2][0], 2, KVH_B, HEAD_DIM), 1.0),
        'norm_g': 1.0 + nrm(ks[6], (DEPTH, 6, D_MODEL), 0.1),
        'w_ffn_gu': nrm(ks[7], (DEPTH, 2, D_MODEL, 2 * D_FF), D_MODEL ** -0.5),
        'w_ffn_dn': nrm(ks[8], (DEPTH, 2, D_FF, D_MODEL), D_FF ** -0.5),
        'w_qkv_a': nrm(ks[9], (N_A_LAYERS, D_MODEL, (H_A + 2 * KVH_A) * HEAD_DIM), D_MODEL ** -0.5),
        'sink_a': nrm(ks[10], (N_A_LAYERS, H_A), 0.5),
        'w_o_a': nrm(ks[11], (N_A_LAYERS, H_A * HEAD_DIM, D_MODEL), (H_A * HEAD_DIM) ** -0.5),
        'g_kv_b': 1.0 + nrm(ks[12], (D_MODEL,), 0.1),
        'w_kv_b': nrm(ks[13], (D_MODEL, N_BG * 2 * KVH_B * HEAD_DIM), D_MODEL ** -0.5),
        'w_q_b': nrm(ks[14], (N_B_LAYERS, D_MODEL, N_BG * H_B * HEAD_DIM), D_MODEL ** -0.5),
        'w_o_b': nrm(ks[15], (N_B_LAYERS, H_B * HEAD_DIM, D_MODEL), (H_B * HEAD_DIM) ** -0.5),
    }


def reference(x_prompt, x_sample, cache_a_kv, cache_b_kv_w128, cache_b_kv_w512, cache_b_kv_w2048,
              norm_g, w_ffn_gu, w_ffn_dn, w_qkv_a, sink_a, w_o_a, g_kv_b, w_kv_b, w_q_b, w_o_b):
    cos_p, sin_p = _rope_tables(jnp.arange(x_prompt.shape[1], dtype=jnp.float32))
    cos_s, sin_s = _rope_tables(jnp.arange(x_sample.shape[1], dtype=jnp.float32) + PAST_LEN)
    y_p, a_p, b_p = _trunk(x_prompt, cos_p, sin_p, None, None, norm_g, w_ffn_gu, w_ffn_dn,
                           w_qkv_a, sink_a, w_o_a, g_kv_b, w_kv_b, w_q_b, w_o_b)
    y_s, a_s, b_s = _trunk(x_sample, cos_s, sin_s, cache_a_kv,
                           (cache_b_kv_w128, cache_b_kv_w512, cache_b_kv_w2048),
                           norm_g, w_ffn_gu, w_ffn_dn, w_qkv_a, sink_a, w_o_a,
                           g_kv_b, w_kv_b, w_q_b, w_o_b)
    return (y_p, y_s, a_p, b_p[0], b_p[1], b_p[2], a_s, b_s[0], b_s[1], b_s[2])
```

```python
import functools

import jax
import jax.numpy as jnp
from jax.experimental import pallas as pl
from jax.experimental.pallas import tpu as pltpu

F32 = jnp.float32
BF16 = jnp.bfloat16

D_MODEL = 1024
HEAD_DIM = 64
ROT_DIM = HEAD_DIM // 4
ROPE_THETA = 500000.0
N_HEADS = 16
N_KV = 4
D_KV = N_KV * HEAD_DIM
D_FF = 2816
EPS = 1e-6
PAST_LEN = 8192
WIN_A = 128
B_GROUPS = ((128, 1), (512, 4), (2048, 16))
TAPS = 128

LANES = 128
FF_CHUNK = 256
ROW_TILE = 512
QBLK = 128
NEG = -1e30
VMEM_LIMIT = 56 * 1024 * 1024


def _params(sem):
    return pltpu.CompilerParams(dimension_semantics=sem, vmem_limit_bytes=VMEM_LIMIT)


def _resident(shape):
    nd = len(shape)
    return pl.BlockSpec(shape, lambda *_: (0,) * nd, pipeline_mode=pl.Buffered(1))


def _rms_scale(x):
    return x * jax.lax.rsqrt(jnp.mean(x * x, axis=-1, keepdims=True) + EPS)


def _ffn_kernel(x_ref, gpre_ref, wg_ref, wu_ref, wd_ref, gpost_ref, o_ref):
    x = x_ref[...]
    xn = (_rms_scale(x) * gpre_ref[...]).astype(BF16)
    acc = None
    for c in range(D_FF // FF_CHUNK):
        sl = slice(c * FF_CHUNK, (c + 1) * FF_CHUNK)
        gate = jnp.dot(xn, wg_ref[:, sl], preferred_element_type=F32)
        up = jnp.dot(xn, wu_ref[:, sl], preferred_element_type=F32)
        act = (gate * jax.nn.sigmoid(gate) * up).astype(BF16)
        part = jnp.dot(act, wd_ref[sl, :], preferred_element_type=F32)
        acc = part if acc is None else acc + part
    o_ref[...] = x + 0.5 * (_rms_scale(acc) * gpost_ref[...])


def _ffn(x, gpre, wg, wu, wd, gpost):
    t = x.shape[0]
    tm = min(ROW_TILE, t)
    row = pl.BlockSpec((tm, D_MODEL), lambda i: (i, 0))
    return pl.pallas_call(
        _ffn_kernel,
        grid=(t // tm,),
        in_specs=[row, _resident((1, D_MODEL)), _resident(wg.shape), _resident(wu.shape),
                  _resident(wd.shape), _resident((1, D_MODEL))],
        out_specs=row,
        out_shape=jax.ShapeDtypeStruct((t, D_MODEL), F32),
        compiler_params=_params(("parallel",)),
        name="ffn",
    )(x, gpre, wg, wu, wd, gpost)


def _proj_kernel(x_ref, gq_ref, gk_ref, wq_ref, wkv_ref, rc_ref, rs1_ref, rs2_ref,
                 q_ref, kv_ref, k2_ref, v2_ref, *, cache_rows):
    tm = x_ref.shape[0]
    xh = _rms_scale(x_ref[...])
    hq = (xh * gq_ref[...]).astype(BF16)
    hk = (xh * gk_ref[...]).astype(BF16)
    rc, rs1, rs2 = rc_ref[...], rs1_ref[...], rs2_ref[...]

    def rope(t):
        return t * rc + pltpu.roll(t, LANES - ROT_DIM // 2, 1) * rs1 + pltpu.roll(t, ROT_DIM // 2, 1) * rs2

    q = jnp.dot(hq, wq_ref[...], preferred_element_type=F32)
    for c in range(D_MODEL // LANES):
        sl = slice(c * LANES, (c + 1) * LANES)
        q_ref[:, sl] = (rope(q[:, sl]) * HEAD_DIM ** -0.5).astype(BF16)

    kv = jnp.dot(hk, wkv_ref[...], preferred_element_type=F32)
    lo = jax.lax.broadcasted_iota(jnp.int32, (tm, LANES), 1) < HEAD_DIM
    r0 = tm - cache_rows
    for c in range(2 * D_KV // LANES):
        sl = slice(c * LANES, (c + 1) * LANES)
        t = kv[:, sl]
        if c < D_KV // LANES:
            t = rope(t)
        kv_ref[:, sl] = t[r0:, :]
        sw = pltpu.roll(t, HEAD_DIM, 1)
        dst = k2_ref if c < D_KV // LANES else v2_ref
        cc = c % (D_KV // LANES)
        dst[:, (2 * cc) * LANES:(2 * cc + 1) * LANES] = jnp.where(lo, t, sw).astype(BF16)
        dst[:, (2 * cc + 1) * LANES:(2 * cc + 2) * LANES] = jnp.where(lo, sw, t).astype(BF16)


def _proj(x, gq, gk, wq, wkv, tabs, *, nb, dil):
    length = x.shape[1]
    tm = min(ROW_TILE, length)
    cache_rows = min(QBLK, tm)
    grid = (nb, dil, length // tm)
    tab = pl.BlockSpec((tm, LANES), lambda b, r, i: (i, r))
    folded = lambda w: pl.BlockSpec((None, None, tm, w), lambda b, r, i: (b, r, i, 0))
    return pl.pallas_call(
        functools.partial(_proj_kernel, cache_rows=cache_rows),
        grid=grid,
        in_specs=[pl.BlockSpec((None, tm, D_MODEL), lambda b, r, i: (b, i, r)),
                  _resident((1, D_MODEL)), _resident((1, D_MODEL)),
                  _resident(wq.shape), _resident(wkv.shape), tab, tab, tab],
        out_specs=[folded(D_MODEL),
                   pl.BlockSpec((None, cache_rows, 2 * D_KV), lambda b, r, i: (b, 0, r)),
                   folded(2 * D_KV), folded(2 * D_KV)],
        out_shape=[jax.ShapeDtypeStruct((nb, dil, length, D_MODEL), BF16),
                   jax.ShapeDtypeStruct((nb, cache_rows, dil * 2 * D_KV), F32),
                   jax.ShapeDtypeStruct((nb, dil, length, 2 * D_KV), BF16),
                   jax.ShapeDtypeStruct((nb, dil, length, 2 * D_KV), BF16)],
        compiler_params=_params(("parallel", "parallel", "arbitrary")),
        name="proj",
    )(x, gq, gk, wq, wkv, *tabs)


def _band_kernel(*refs, has_sink, has_lse):
    refs = list(refs)
    sink_ref = refs.pop(0) if has_sink else None
    q_ref, kp_ref, kc_ref, vp_ref, vc_ref, bias_ref, o_ref = refs[:7]
    lse_ref = refs[7] if has_lse else None

    bias = bias_ref[...]
    lo_k = jax.lax.broadcasted_iota(jnp.int32, (2 * QBLK, LANES), 1) < HEAD_DIM
    lo_o = jax.lax.broadcasted_iota(jnp.int32, (2 * QBLK, LANES), 1) < HEAD_DIM
    top = jax.lax.broadcasted_iota(jnp.int32, (2 * QBLK, 1), 0) < QBLK
    zero = jnp.zeros((), BF16)

    for h in range(N_KV):
        ksl = slice(h * LANES, (h + 1) * LANES)
        k2 = jnp.concatenate([kp_ref[:, ksl], kc_ref[:, ksl]], axis=0)
        v2 = jnp.concatenate([vp_ref[:, ksl], vc_ref[:, ksl]], axis=0)
        kbd = jnp.concatenate([jnp.where(lo_k, k2, zero), jnp.where(lo_k, zero, k2)], axis=0)
        vbd = jnp.concatenate([jnp.where(lo_k, v2, zero), jnp.where(lo_k, zero, v2)], axis=0)
        qq = jnp.concatenate([q_ref[:, (2 * h) * LANES:(2 * h + 1) * LANES],
                              q_ref[:, (2 * h + 1) * LANES:(2 * h + 2) * LANES]], axis=0)
        s = jax.lax.dot_general(qq, kbd, (((1,), (1,)), ((), ())),
                                preferred_element_type=F32)
        ps, inv, lse = [], [], []
        for c in range(2):
            sc = s[:, c * 2 * QBLK:(c + 1) * 2 * QBLK] + bias
            m = jnp.max(sc, axis=-1, keepdims=True)
            if has_sink:
                sk = jnp.where(top, sink_ref[4 * h + c], sink_ref[4 * h + 2 + c])
                m = jnp.maximum(m, sk)
            p = jnp.exp(sc - m)
            l = jnp.sum(p, axis=-1, keepdims=True)
            if has_sink:
                l = l + jnp.exp(sk - m)
            ps.append(p.astype(BF16))
            inv.append(1.0 / l)
            if has_lse:
                lse.append(m + jnp.log(l))
        o = jnp.dot(jnp.concatenate(ps, axis=1), vbd, preferred_element_type=F32)
        o = o * jnp.where(lo_o, inv[0], inv[1])
        o_ref[:, (2 * h) * LANES:(2 * h + 1) * LANES] = o[:QBLK].astype(o_ref.dtype)
        o_ref[:, (2 * h + 1) * LANES:(2 * h + 2) * LANES] = o[QBLK:].astype(o_ref.dtype)
        if has_lse:
            le = jnp.where(lo_o, lse[0], lse[1])
            lse_ref[:, (2 * h) * LANES:(2 * h + 1) * LANES] = le[:QBLK]
            lse_ref[:, (2 * h + 1) * LANES:(2 * h + 2) * LANES] = le[QBLK:]


def _band_attention(q, k2, v2, bias, sink, *, has_lse):
    nb, dil, length, _ = q.shape
    grid = (nb, dil, length // QBLK)
    has_sink = sink is not None
    cur = lambda w: pl.BlockSpec((None, None, QBLK, w), lambda b, r, i: (b, r, i, 0))
    prev = lambda w: pl.BlockSpec((None, None, QBLK, w),
                                  lambda b, r, i: (b, r, jnp.maximum(i - 1, 0), 0))
    out = pl.BlockSpec((None, QBLK, D_MODEL), lambda b, r, i: (b, i, r))
    in_specs = [cur(D_MODEL), prev(2 * D_KV), cur(2 * D_KV), prev(2 * D_KV), cur(2 * D_KV),
                pl.BlockSpec((None, 2 * QBLK, 2 * QBLK), lambda b, r, i: (jnp.minimum(i, 1), 0, 0))]
    args = [q, k2, k2, v2, v2, bias]
    if has_sink:
        in_specs = [pl.BlockSpec(memory_space=pltpu.SMEM)] + in_specs
        args = [sink] + args
    out_specs = [out]
    out_shape = [jax.ShapeDtypeStruct((nb, length, dil * D_MODEL), BF16)]
    if has_lse:
        out_specs.append(out)
        out_shape.append(jax.ShapeDtypeStruct((nb, length, dil * D_MODEL), F32))
    return pl.pallas_call(
        functools.partial(_band_kernel, has_sink=has_sink, has_lse=has_lse),
        grid=grid, in_specs=in_specs, out_specs=out_specs, out_shape=out_shape,
        compiler_params=_params(("parallel", "parallel", "parallel")),
        name="band_attn",
    )(*args)


def _decode_kernel(*refs, has_sink):
    refs = list(refs)
    sink_ref = refs.pop(0) if has_sink else None
    qbd_ref, taps_ref, new_ref, o_ref, lse_ref = refs
    nreq = qbd_ref.shape[0]

    def body(b, carry):
        qb = qbd_ref[b]
        kb = taps_ref[b, :, 0:D_KV].astype(BF16)
        vb = taps_ref[b, :, D_KV:2 * D_KV].astype(BF16)
        kn = new_ref[b, :, 0:D_KV]
        vn = new_ref[b, :, D_KV:2 * D_KV]
        s = jax.lax.dot_general(qb.astype(BF16), kb, (((1,), (1,)), ((), ())),
                                preferred_element_type=F32)
        sn = jnp.sum(qb * kn, axis=-1, keepdims=True)
        m = jnp.maximum(jnp.max(s, axis=-1, keepdims=True), sn)
        if has_sink:
            sk = sink_ref[...]
            m = jnp.maximum(m, sk)
        p = jnp.exp(s - m)
        pn = jnp.exp(sn - m)
        l = jnp.sum(p, axis=-1, keepdims=True) + pn
        if has_sink:
            l = l + jnp.exp(sk - m)
        o = jnp.dot(p.astype(BF16), vb, preferred_element_type=F32) + pn * vn
        o_ref[b] = o / l
        lse_ref[b] = jnp.broadcast_to(m + jnp.log(l), (N_HEADS, LANES))
        return carry

    jax.lax.fori_loop(0, nreq, body, 0)


def _decode_attention(qbd, taps, new, sink_rows):
    nreq = qbd.shape[0]
    rb = 32
    has_sink = sink_rows is not None
    in_specs = [pl.BlockSpec((rb, N_HEADS, D_KV), lambda i: (i, 0, 0)),
                pl.BlockSpec((rb, TAPS, 2 * D_KV), lambda i: (i, 0, 0)),
                pl.BlockSpec((rb, 1, 2 * D_KV), lambda i: (i, 0, 0))]
    args = [qbd, taps, new]
    if has_sink:
        in_specs = [pl.BlockSpec((N_HEADS, 1), lambda i: (0, 0))] + in_specs
        args = [sink_rows] + args
    return pl.pallas_call(
        functools.partial(_decode_kernel, has_sink=has_sink),
        grid=(nreq // rb,), in_specs=in_specs,
        out_specs=[pl.BlockSpec((rb, N_HEADS, D_KV), lambda i: (i, 0, 0)),
                   pl.BlockSpec((rb, N_HEADS, LANES), lambda i: (i, 0, 0))],
        out_shape=[jax.ShapeDtypeStruct((nreq, N_HEADS, D_KV), F32),
                   jax.ShapeDtypeStruct((nreq, N_HEADS, LANES), F32)],
        compiler_params=_params(("parallel",)),
        name="decode_attn",
    )(*args)


def _oproj_kernel(*refs, n_groups):
    o_refs = refs[:n_groups]
    lse_refs = refs[n_groups:2 * n_groups] if n_groups > 1 else ()
    x_ref, wo_ref, g_ref, out_ref = refs[-4:]
    if n_groups == 1:
        o = o_refs[0][...].astype(BF16)
    else:
        lses = [r[...] for r in lse_refs]
        m = functools.reduce(jnp.maximum, lses)
        es = [jnp.exp(l - m) for l in lses]
        num = functools.reduce(jnp.add, [e * r[...].astype(F32) for e, r in zip(es, o_refs)])
        o = (num / functools.reduce(jnp.add, es)).astype(BF16)
    mix = jnp.dot(o, wo_ref[...], preferred_element_type=F32)
    out_ref[...] = x_ref[...] + _rms_scale(mix) * g_ref[...]


def _oproj(os_, lses, x, wo, g):
    t = x.shape[0]
    tm = min(ROW_TILE, t)
    row = pl.BlockSpec((tm, D_MODEL), lambda i: (i, 0))
    n_groups = len(os_)
    return pl.pallas_call(
        functools.partial(_oproj_kernel, n_groups=n_groups),
        grid=(t // tm,),
        in_specs=[row] * (n_groups + len(lses) + 1) + [_resident(wo.shape), _resident((1, D_MODEL))],
        out_specs=row,
        out_shape=jax.ShapeDtypeStruct((t, D_MODEL), F32),
        compiler_params=_params(("parallel",)),
        name="oproj",
    )(*os_, *lses, x, wo, g)


SHIFT_CHUNKS = 8


def _shift_kernel(*refs, n_caches):
    caches = refs[:n_caches]
    news = refs[n_caches:2 * n_caches]
    outs = refs[2 * n_caches:3 * n_caches]
    sem = refs[3 * n_caches]
    copies = []
    row = 2 * D_KV
    for ci in range(n_caches):
        nreq, flat = caches[ci].shape
        step = nreq // SHIFT_CHUNKS
        for j in range(SHIFT_CHUNKS):
            rows = pl.ds(j * step, step)
            copies.append(pltpu.make_async_copy(
                caches[ci].at[rows, pl.ds(row, flat - row)],
                outs[ci].at[rows, pl.ds(0, flat - row)], sem.at[ci, j]))
        copies.append(pltpu.make_async_copy(
            news[ci], outs[ci].at[:, pl.ds(flat - row, row)], sem.at[ci, SHIFT_CHUNKS]))
    for c in copies:
        c.start()
    for c in copies:
        c.wait()


def _shift_caches(caches, news):
    n = len(caches)
    anyspec = pl.BlockSpec(memory_space=pl.ANY)
    return pl.pallas_call(
        functools.partial(_shift_kernel, n_caches=n),
        in_specs=[anyspec] * (2 * n),
        out_specs=[anyspec] * n,
        out_shape=[jax.ShapeDtypeStruct(c.shape, c.dtype) for c in caches],
        scratch_shapes=[pltpu.SemaphoreType.DMA((n, SHIFT_CHUNKS + 1))],
        name="shift_caches",
    )(*caches, *news)


def _rope_tabs(pos):
    half = ROT_DIM // 2
    inv = ROPE_THETA ** (-jnp.arange(0, ROT_DIM, 2, dtype=F32) / ROT_DIM)
    ang = pos[:, None] * inv[None, :]
    cos, sin = jnp.cos(ang), jnp.sin(ang)
    n = pos.shape[0]
    pad = jnp.zeros((n, HEAD_DIM - ROT_DIM), F32)
    zer = jnp.zeros((n, half), F32)
    c = jnp.concatenate([cos, cos, pad + 1.0], axis=1)
    s1 = jnp.concatenate([-sin, zer, pad], axis=1)
    s2 = jnp.concatenate([zer, sin, pad], axis=1)
    return tuple(jnp.tile(t, (1, LANES // HEAD_DIM)) for t in (c, s1, s2))


def _band_bias():
    i = jnp.arange(QBLK)[:, None]
    j = jnp.arange(QBLK)[None, :]
    prev_ok = j >= i
    cur_ok = j <= i
    later = jnp.concatenate([prev_ok, cur_ok], axis=1)
    first = jnp.concatenate([jnp.zeros_like(prev_ok), cur_ok], axis=1)
    b = jnp.where(jnp.stack([first, later]), 0.0, NEG).astype(F32)
    return jnp.tile(b, (1, 2, 1))


def _heads_ig(w):
    k = w.shape[0]
    return w.reshape(k, N_KV, N_HEADS // N_KV, HEAD_DIM).transpose(0, 2, 1, 3).reshape(k, D_MODEL)


def _block_diag_q(q):
    r = q.shape[0]
    q4 = q.astype(F32).reshape(r, 4, 1, N_KV, HEAD_DIM)
    eye = jnp.eye(N_KV, dtype=F32)[None, None, :, :, None]
    return (q4 * eye).reshape(r, N_HEADS, D_KV)


def _own_head(o):
    r = o.shape[0]
    o5 = o.reshape(r, 4, N_KV, N_KV, HEAD_DIM)
    d = jnp.einsum('rigkd,gk->rigd', o5, jnp.eye(N_KV, dtype=o.dtype))
    return d.transpose(0, 2, 1, 3).reshape(r, D_MODEL)


def _expand_lse(lse):
    r = lse.shape[0]
    l = lse[:, :, 0].reshape(r, 4, N_KV).transpose(0, 2, 1).reshape(r, N_HEADS)
    return jnp.repeat(l, HEAD_DIM, axis=1)


def kernel(x_prompt, x_sample, cache_a_kv, cache_b_kv_w128, cache_b_kv_w512, cache_b_kv_w2048,
           norm_g, w_ffn_gu, w_ffn_dn, w_qkv_a, sink_a, w_o_a, g_kv_b, w_kv_b, w_q_b, w_o_b):
    nb, seq, _ = x_prompt.shape
    nreq = x_sample.shape[0]
    tp = nb * seq
    b_caches = (cache_b_kv_w128, cache_b_kv_w512, cache_b_kv_w2048)

    def gain(l, k):
        return norm_g[l, k].reshape(1, D_MODEL)

    def ffn_w(l, k):
        gu = w_ffn_gu[l, k].astype(BF16)
        return gu[:, :D_FF], gu[:, D_FF:], w_ffn_dn[l, k].astype(BF16)

    bias = _band_bias()
    tabs_p = _rope_tabs(jnp.arange(seq, dtype=F32))
    tabs_s = _rope_tabs(jnp.zeros((nreq,), F32) + PAST_LEN)

    wq_a = w_qkv_a[0][:, :D_MODEL]
    wkv_a = w_qkv_a[0][:, D_MODEL:].astype(BF16)
    wo_a = w_o_a[0].astype(BF16)
    wo_b = w_o_b[0].astype(BF16)
    g_kv = g_kv_b.reshape(1, D_MODEL)
    sink = sink_a[0].astype(F32)
    sink_rows = sink.reshape(N_KV, 4).T.reshape(N_HEADS, 1)

    x = x_prompt.reshape(tp, D_MODEL)
    x = _ffn(x, gain(0, 0), *ffn_w(0, 0), gain(0, 1))
    q, kv_a, k2, v2 = _proj(x.reshape(nb, seq, D_MODEL), gain(0, 2), gain(0, 2),
                            wq_a.astype(BF16), wkv_a, tabs_p, nb=nb, dil=1)
    o, = _band_attention(q, k2, v2, bias, sink, has_lse=False)
    x = _oproj([o.reshape(tp, D_MODEL)], [], x, wo_a, gain(0, 3))
    x = _ffn(x, gain(0, 4), *ffn_w(0, 1), gain(0, 5))
    x = _ffn(x, gain(1, 0), *ffn_w(1, 0), gain(1, 1))
    os_, lses, kv_b = [], [], []
    for gi, (win, dil) in enumerate(B_GROUPS):
        wq = w_q_b[0][:, gi * D_MODEL:(gi + 1) * D_MODEL].astype(BF16)
        wkv = w_kv_b[:, gi * 2 * D_KV:(gi + 1) * 2 * D_KV].astype(BF16)
        tabs = tuple(t.reshape(seq // dil, dil * LANES) for t in tabs_p)
        q, kvc, k2, v2 = _proj(x.reshape(nb, seq // dil, dil * D_MODEL), gain(1, 2), g_kv,
                               wq, wkv, tabs, nb=nb, dil=dil)
        o, lse = _band_attention(q, k2, v2, bias, None, has_lse=True)
        os_.append(o.reshape(tp, D_MODEL))
        lses.append(lse.reshape(tp, D_MODEL))
        kv_b.append(kvc.reshape(nb, win, 2, N_KV, HEAD_DIM))
    x = _oproj(os_, lses, x, wo_b, gain(1, 3))
    y_p = _ffn(x, gain(1, 4), *ffn_w(1, 1), gain(1, 5)).reshape(nb, seq, D_MODEL)
    a_p = kv_a.reshape(1, nb, WIN_A, 2, N_KV, HEAD_DIM)

    xs = x_sample.reshape(nreq, D_MODEL)
    xs = _ffn(xs, gain(0, 0), *ffn_w(0, 0), gain(0, 1))
    q, new_a, _, _ = _proj(xs.reshape(1, nreq, D_MODEL), gain(0, 2), gain(0, 2),
                           _heads_ig(wq_a).astype(BF16), wkv_a, tabs_s, nb=1, dil=1)
    new_a = new_a.reshape(nreq, 1, 2 * D_KV)
    cache_a = cache_a_kv[0].reshape(nreq, WIN_A, 2 * D_KV)
    o, _ = _decode_attention(_block_diag_q(q.reshape(nreq, D_MODEL)), cache_a, new_a, sink_rows)
    xs = _oproj([_own_head(o)], [], xs, wo_a, gain(0, 3))
    xs = _ffn(xs, gain(0, 4), *ffn_w(0, 1), gain(0, 5))
    xs = _ffn(xs, gain(1, 0), *ffn_w(1, 0), gain(1, 1))
    os_, lses, news = [], [], []
    for gi, (win, dil) in enumerate(B_GROUPS):
        wq = _heads_ig(w_q_b[0][:, gi * D_MODEL:(gi + 1) * D_MODEL]).astype(BF16)
        wkv = w_kv_b[:, gi * 2 * D_KV:(gi + 1) * 2 * D_KV].astype(BF16)
        q, new_g, _, _ = _proj(xs.reshape(1, nreq, D_MODEL), gain(1, 2), g_kv, wq, wkv, tabs_s,
                               nb=1, dil=1)
        new_g = new_g.reshape(nreq, 1, 2 * D_KV)
        taps = b_caches[gi].reshape(nreq, win // dil, dil * 2 * D_KV)
        o, lse = _decode_attention(_block_diag_q(q.reshape(nreq, D_MODEL)), taps, new_g, None)
        os_.append(_own_head(o))
        lses.append(_expand_lse(lse))
        news.append(new_g)
    xs = _oproj(os_, lses, xs, wo_b, gain(1, 3))
    y_s = _ffn(xs, gain(1, 4), *ffn_w(1, 1), gain(1, 5)).reshape(nreq, 1, D_MODEL)

    flat = [cache_a.reshape(nreq, -1)] + [c.reshape(nreq, -1) for c in b_caches]
    shifted = _shift_caches(flat, [n.reshape(nreq, 2 * D_KV) for n in [new_a] + news])
    a_s = shifted[0].reshape(1, nreq, WIN_A, 2, N_KV, HEAD_DIM)
    b_s = [s.reshape(nreq, w, 2, N_KV, HEAD_DIM) for s, (w, _) in zip(shifted[1:], B_GROUPS)]

    return (y_p, y_s, a_p, kv_b[0], kv_b[1], kv_b[2], a_s, b_s[0], b_s[1], b_s[2])
```

```python
import functools

import jax
import jax.numpy as jnp
from jax.experimental import pallas as pl
from jax.experimental.pallas import tpu as pltpu

F32 = jnp.float32
BF16 = jnp.bfloat16

D_MODEL = 1024
HEAD_DIM = 64
ROT_DIM = HEAD_DIM // 4
ROPE_THETA = 500000.0
N_HEADS = 16
N_KV = 4
D_KV = N_KV * HEAD_DIM
D_FF = 2816
EPS = 1e-6
PAST_LEN = 8192
WIN_A = 128
B_GROUPS = ((128, 1), (512, 4), (2048, 16))

LANES = 128
FF_CHUNK = 256
ROW_TILE = 512
QBLK = 128
NEG = -1e30
VMEM_LIMIT = 56 * 1024 * 1024
DECODE_BLOCK_BYTES = 4 * 1024 * 1024


def _params(sem):
    return pltpu.CompilerParams(dimension_semantics=sem, vmem_limit_bytes=VMEM_LIMIT)


def _resident(shape):
    nd = len(shape)
    return pl.BlockSpec(shape, lambda *_: (0,) * nd, pipeline_mode=pl.Buffered(1))


def _rms_scale(x):
    return x * jax.lax.rsqrt(jnp.mean(x * x, axis=-1, keepdims=True) + EPS)


def _split_bf16(x, parts):
    out = []
    for _ in range(parts - 1):
        hi = x.astype(BF16)
        out.append(hi)
        x = x - hi.astype(F32)
    out.append(x.astype(BF16))
    return out


def _onehot_dot(sel, x, parts):
    return functools.reduce(
        jnp.add, [jnp.dot(sel, p, preferred_element_type=F32) for p in _split_bf16(x, parts)])


def _ffn_kernel(x_ref, gpre_ref, wg_ref, wu_ref, wd_ref, gpost_ref, o_ref):
    x = x_ref[...]
    xn = (_rms_scale(x) * gpre_ref[...]).astype(BF16)
    acc = None
    for c in range(D_FF // FF_CHUNK):
        sl = slice(c * FF_CHUNK, (c + 1) * FF_CHUNK)
        gate = jnp.dot(xn, wg_ref[:, sl], preferred_element_type=F32)
        up = jnp.dot(xn, wu_ref[:, sl], preferred_element_type=F32)
        act = (gate * jax.nn.sigmoid(gate) * up).astype(BF16)
        part = jnp.dot(act, wd_ref[sl, :], preferred_element_type=F32)
        acc = part if acc is None else acc + part
    o_ref[...] = x + 0.5 * (_rms_scale(acc) * gpost_ref[...])


def _ffn(x, gpre, wg, wu, wd, gpost):
    t = x.shape[0]
    tm = min(ROW_TILE, t)
    row = pl.BlockSpec((tm, D_MODEL), lambda i: (i, 0))
    return pl.pallas_call(
        _ffn_kernel,
        grid=(t // tm,),
        in_specs=[row, _resident((1, D_MODEL)), _resident(wg.shape), _resident(wu.shape),
                  _resident(wd.shape), _resident((1, D_MODEL))],
        out_specs=row,
        out_shape=jax.ShapeDtypeStruct((t, D_MODEL), F32),
        compiler_params=_params(("parallel",)),
        name="ffn",
    )(x, gpre, wg, wu, wd, gpost)


def _project(x_ref, gq_ref, gk_ref, wq_ref, wkv_ref, rc_ref, rs1_ref, rs2_ref):
    xh = _rms_scale(x_ref[...])
    hq = (xh * gq_ref[...]).astype(BF16)
    hk = (xh * gk_ref[...]).astype(BF16)
    rc, rs1, rs2 = rc_ref[...], rs1_ref[...], rs2_ref[...]

    def rope(t):
        return t * rc + pltpu.roll(t, LANES - ROT_DIM // 2, 1) * rs1 + pltpu.roll(t, ROT_DIM // 2, 1) * rs2

    q = jnp.dot(hq, wq_ref[...], preferred_element_type=F32)
    qs = [(rope(q[:, c * LANES:(c + 1) * LANES]) * HEAD_DIM ** -0.5).astype(BF16)
          for c in range(D_MODEL // LANES)]
    kv = jnp.dot(hk, wkv_ref[...], preferred_element_type=F32)
    kvs = []
    for c in range(2 * D_KV // LANES):
        t = kv[:, c * LANES:(c + 1) * LANES]
        kvs.append(rope(t) if c < D_KV // LANES else t)
    return qs, kvs


def _dup_heads(kvs):
    rows = kvs[0].shape[0]
    lo = jax.lax.broadcasted_iota(jnp.int32, (rows, LANES), 1) < HEAD_DIM
    out = []
    for t in kvs:
        sw = pltpu.roll(t, HEAD_DIM, 1)
        out.append(jnp.where(lo, t, sw).astype(BF16))
        out.append(jnp.where(lo, sw, t).astype(BF16))
    return out


def _proj_kernel(*refs, dil, cache_rows):
    ins = refs[:8]
    perm_ref = refs[8] if dil > 1 else None
    q_ref, kv_ref, k2_ref, v2_ref = refs[-4:]
    tm = ins[0].shape[0]
    qs, kvs = _project(*ins)
    for c, t in enumerate(kvs):
        kv_ref[:, c * LANES:(c + 1) * LANES] = t[tm - cache_rows:, :]
    dup = _dup_heads(kvs)
    cols = jnp.concatenate(qs + dup, axis=1)
    if dil > 1:
        cols = jnp.dot(perm_ref[...], cols, preferred_element_type=F32).astype(BF16)
    n = tm // dil
    for r in range(dil):
        blk = cols[r * n:(r + 1) * n, :]
        q_ref[r] = blk[:, :D_MODEL]
        k2_ref[r] = blk[:, D_MODEL:D_MODEL + 2 * D_KV]
        v2_ref[r] = blk[:, D_MODEL + 2 * D_KV:]


def _proj(x, gq, gk, wq, wkv, tabs, perm, *, dil, win):
    nb, seq, _ = x.shape
    tm = ROW_TILE
    nt = seq // tm
    cache_rows = min(win, tm)
    first_cache_tile = nt - win // cache_rows
    n = tm // dil
    tab = pl.BlockSpec((tm, LANES), lambda b, i: (i, 0))
    folded = lambda w: pl.BlockSpec((None, dil, n, w), lambda b, i: (b, 0, i, 0))
    in_specs = [pl.BlockSpec((None, tm, D_MODEL), lambda b, i: (b, i, 0)),
                _resident((1, D_MODEL)), _resident((1, D_MODEL)),
                _resident(wq.shape), _resident(wkv.shape), tab, tab, tab]
    args = [x, gq, gk, wq, wkv, *tabs]
    if dil > 1:
        in_specs.append(_resident(perm.shape))
        args.append(perm)
    return pl.pallas_call(
        functools.partial(_proj_kernel, dil=dil, cache_rows=cache_rows),
        grid=(nb, nt),
        in_specs=in_specs,
        out_specs=[folded(D_MODEL),
                   pl.BlockSpec((None, cache_rows, 2 * D_KV),
                                lambda b, i: (b, jnp.maximum(i - first_cache_tile, 0), 0)),
                   folded(2 * D_KV), folded(2 * D_KV)],
        out_shape=[jax.ShapeDtypeStruct((nb, dil, seq // dil, D_MODEL), BF16),
                   jax.ShapeDtypeStruct((nb, win, 2 * D_KV), F32),
                   jax.ShapeDtypeStruct((nb, dil, seq // dil, 2 * D_KV), BF16),
                   jax.ShapeDtypeStruct((nb, dil, seq // dil, 2 * D_KV), BF16)],
        compiler_params=_params(("parallel", "arbitrary")),
        name="proj",
    )(*args)


def _proj_decode_kernel(*refs):
    q_ref, new_ref, newt_ref = refs[-3:]
    qs, kvs = _project(*refs[:8])
    for c, t in enumerate(qs):
        q_ref[:, c * LANES:(c + 1) * LANES] = t
    for c, t in enumerate(kvs):
        new_ref[:, c * LANES:(c + 1) * LANES] = t
        newt_ref[c * LANES:(c + 1) * LANES, :] = t.T


def _proj_decode(x, gq, gk, wq, wkv, tabs):
    nreq = x.shape[0]
    full = lambda w: pl.BlockSpec((nreq, w), lambda i: (0, 0))
    return pl.pallas_call(
        _proj_decode_kernel,
        grid=(1,),
        in_specs=[full(D_MODEL), _resident((1, D_MODEL)), _resident((1, D_MODEL)),
                  _resident(wq.shape), _resident(wkv.shape), full(LANES), full(LANES), full(LANES)],
        out_specs=[full(D_MODEL), full(2 * D_KV), pl.BlockSpec((2 * D_KV, nreq), lambda i: (0, 0))],
        out_shape=[jax.ShapeDtypeStruct((nreq, D_MODEL), BF16),
                   jax.ShapeDtypeStruct((nreq, 2 * D_KV), F32),
                   jax.ShapeDtypeStruct((2 * D_KV, nreq), F32)],
        compiler_params=_params(("arbitrary",)),
        name="proj_decode",
    )(x, gq, gk, wq, wkv, *tabs)


def _band_kernel(*refs, has_sink, has_lse):
    refs = list(refs)
    sink_ref = refs.pop(0) if has_sink else None
    q_ref, kp_ref, kc_ref, vp_ref, vc_ref, bias_ref, o_ref = refs[:7]
    lse_ref = refs[7] if has_lse else None

    bias = bias_ref[...]
    lo_k = jax.lax.broadcasted_iota(jnp.int32, (2 * QBLK, LANES), 1) < HEAD_DIM
    lane_q = jax.lax.broadcasted_iota(jnp.int32, (QBLK, LANES), 1)
    top = jax.lax.broadcasted_iota(jnp.int32, (2 * QBLK, 1), 0) < QBLK
    zero = jnp.zeros((), BF16)
    lse_tile = jnp.zeros((QBLK, LANES), F32)

    for h in range(N_KV):
        ksl = slice(h * LANES, (h + 1) * LANES)
        k2 = jnp.concatenate([kp_ref[:, ksl], kc_ref[:, ksl]], axis=0)
        v2 = jnp.concatenate([vp_ref[:, ksl], vc_ref[:, ksl]], axis=0)
        kbd = jnp.concatenate([jnp.where(lo_k, k2, zero), jnp.where(lo_k, zero, k2)], axis=0)
        vbd = jnp.concatenate([jnp.where(lo_k, v2, zero), jnp.where(lo_k, zero, v2)], axis=0)
        qq = jnp.concatenate([q_ref[:, (2 * h) * LANES:(2 * h + 1) * LANES],
                              q_ref[:, (2 * h + 1) * LANES:(2 * h + 2) * LANES]], axis=0)
        s = jax.lax.dot_general(qq, kbd, (((1,), (1,)), ((), ())),
                                preferred_element_type=F32)
        ps, inv = [], []
        for c in range(2):
            sc = s[:, c * 2 * QBLK:(c + 1) * 2 * QBLK] + bias
            m = jnp.max(sc, axis=-1, keepdims=True)
            if has_sink:
                sk = jnp.where(top, sink_ref[4 * h + c], sink_ref[4 * h + 2 + c])
                m = jnp.maximum(m, sk)
            p = jnp.exp(sc - m)
            l = jnp.sum(p, axis=-1, keepdims=True)
            if has_sink:
                l = l + jnp.exp(sk - m)
            ps.append(p.astype(BF16))
            inv.append(1.0 / l)
            if has_lse:
                lse = m + jnp.log(l)
                lse_tile = jnp.where(lane_q == 4 * h + c, lse[:QBLK], lse_tile)
                lse_tile = jnp.where(lane_q == 4 * h + 2 + c, lse[QBLK:], lse_tile)
        o = jnp.dot(jnp.concatenate(ps, axis=1), vbd, preferred_element_type=F32)
        o = o * jnp.where(lo_k, inv[0], inv[1])
        o_ref[:, (2 * h) * LANES:(2 * h + 1) * LANES] = o[:QBLK].astype(o_ref.dtype)
        o_ref[:, (2 * h + 1) * LANES:(2 * h + 2) * LANES] = o[QBLK:].astype(o_ref.dtype)
    if has_lse:
        lse_ref[...] = lse_tile


def _band_attention(q, k2, v2, bias, sink, *, has_lse):
    nb, dil, length, _ = q.shape
    grid = (nb, dil, length // QBLK)
    has_sink = sink is not None
    cur = lambda w: pl.BlockSpec((None, None, QBLK, w), lambda b, r, i: (b, r, i, 0))
    prev = lambda w: pl.BlockSpec((None, None, QBLK, w),
                                  lambda b, r, i: (b, r, jnp.maximum(i - 1, 0), 0))
    in_specs = [cur(D_MODEL), prev(2 * D_KV), cur(2 * D_KV), prev(2 * D_KV), cur(2 * D_KV),
                pl.BlockSpec((None, 2 * QBLK, 2 * QBLK), lambda b, r, i: (jnp.minimum(i, 1), 0, 0))]
    args = [q, k2, k2, v2, v2, bias]
    if has_sink:
        in_specs = [pl.BlockSpec(memory_space=pltpu.SMEM)] + in_specs
        args = [sink] + args
    out_specs = [cur(D_MODEL)]
    out_shape = [jax.ShapeDtypeStruct((nb, dil, length, D_MODEL), BF16)]
    if has_lse:
        out_specs.append(cur(LANES))
        out_shape.append(jax.ShapeDtypeStruct((nb, dil, length, LANES), F32))
    return pl.pallas_call(
        functools.partial(_band_kernel, has_sink=has_sink, has_lse=has_lse),
        grid=grid, in_specs=in_specs, out_specs=out_specs, out_shape=out_shape,
        compiler_params=_params(("parallel", "parallel", "parallel")),
        name="band_attn",
    )(*args)


def _decode_kernel(*refs, has_sink, rb):
    refs = list(refs)
    sink_ref = refs.pop(0) if has_sink else None
    qbd_ref, cache_ref, new_ref, newt_ref, tap_ref, o_ref, lse_ref, cout_ref = refs
    win = cache_ref.shape[2]
    base = pl.program_id(0) * rb
    lane = jax.lax.broadcasted_iota(jnp.int32, (2 * D_KV, LANES), 1)
    tap_bias = tap_ref[...]

    def body(b, carry):
        qb = qbd_ref[b]
        cache = cache_ref[b]
        kn = new_ref[b, :, 0:D_KV]
        vn = new_ref[b, :, D_KV:2 * D_KV]
        s = jnp.dot(qb.astype(BF16), cache[0:D_KV].astype(BF16),
                    preferred_element_type=F32) + tap_bias
        sn = jnp.sum(qb * kn, axis=-1, keepdims=True)
        m = jnp.maximum(jnp.max(s, axis=-1, keepdims=True), sn)
        if has_sink:
            sk = sink_ref[...]
            m = jnp.maximum(m, sk)
        p = jnp.exp(s - m)
        pn = jnp.exp(sn - m)
        l = jnp.sum(p, axis=-1, keepdims=True) + pn
        if has_sink:
            l = l + jnp.exp(sk - m)
        o = jax.lax.dot_general(p.astype(BF16), cache[D_KV:2 * D_KV].astype(BF16),
                                (((1,), (1,)), ((), ())), preferred_element_type=F32) + pn * vn
        o_ref[b] = o / l
        lse_ref[b] = jnp.broadcast_to(m + jnp.log(l), (N_HEADS, LANES))
        shifted = pltpu.roll(cache, win - 1, 1)
        newcol = pltpu.roll(newt_ref[...], LANES - 1 - (base + b), 1)
        cout_ref[b] = shifted
        cout_ref[b, :, win - LANES:win] = jnp.where(lane == LANES - 1, newcol, shifted[:, win - LANES:win])
        return carry

    jax.lax.fori_loop(0, rb, body, 0)


def _decode_attention(qbd, cache, new, newt, tap_bias, sink_rows):
    nreq, feat, win = cache.shape
    rb = max(1, DECODE_BLOCK_BYTES // (feat * win * 4))
    has_sink = sink_rows is not None
    req = lambda *tail: pl.BlockSpec((rb,) + tail, lambda i: (i,) + (0,) * len(tail))
    in_specs = [req(N_HEADS, D_KV), req(feat, win), req(1, 2 * D_KV),
                pl.BlockSpec((feat, nreq), lambda i: (0, 0)),
                pl.BlockSpec((1, win), lambda i: (0, 0))]
    args = [qbd, cache, new, newt, tap_bias]
    if has_sink:
        in_specs = [pl.BlockSpec((N_HEADS, 1), lambda i: (0, 0))] + in_specs
        args = [sink_rows] + args
    return pl.pallas_call(
        functools.partial(_decode_kernel, has_sink=has_sink, rb=rb),
        grid=(nreq // rb,), in_specs=in_specs,
        out_specs=[req(N_HEADS, D_KV), req(N_HEADS, LANES), req(feat, win)],
        out_shape=[jax.ShapeDtypeStruct((nreq, N_HEADS, D_KV), F32),
                   jax.ShapeDtypeStruct((nreq, N_HEADS, LANES), F32),
                   jax.ShapeDtypeStruct(cache.shape, F32)],
        compiler_params=_params(("parallel",)),
        name="decode_attn",
    )(*args)


def _oproj_kernel(*refs, n_groups):
    o_refs = refs[:n_groups]
    lse_refs = refs[n_groups:2 * n_groups] if n_groups > 1 else ()
    x_ref, wo_ref, g_ref, out_ref = refs[-4:]
    if n_groups == 1:
        o = o_refs[0][...].astype(BF16)
    else:
        lses = [r[...] for r in lse_refs]
        m = functools.reduce(jnp.maximum, lses)
        es = [jnp.exp(l - m) for l in lses]
        den = functools.reduce(jnp.add, es)
        o = functools.reduce(jnp.add, [(e / den) * r[...].astype(F32)
                                       for e, r in zip(es, o_refs)]).astype(BF16)
    mix = jnp.dot(o, wo_ref[...], preferred_element_type=F32)
    out_ref[...] = x_ref[...] + _rms_scale(mix) * g_ref[...]


def _oproj(os_, lses, x, wo, g):
    t = x.shape[0]
    tm = min(ROW_TILE, t)
    row = pl.BlockSpec((tm, D_MODEL), lambda i: (i, 0))
    n_groups = len(os_)
    return pl.pallas_call(
        functools.partial(_oproj_kernel, n_groups=n_groups),
        grid=(t // tm,),
        in_specs=[row] * (n_groups + len(lses) + 1) + [_resident(wo.shape), _resident((1, D_MODEL))],
        out_specs=row,
        out_shape=jax.ShapeDtypeStruct((t, D_MODEL), F32),
        compiler_params=_params(("parallel",)),
        name="oproj",
    )(*os_, *lses, x, wo, g)


def _oproj_folded_kernel(*refs, dils):
    ng = len(dils)
    o_refs, lse_refs = refs[:ng], refs[ng:2 * ng]
    unperm_refs = refs[2 * ng:2 * ng + sum(d > 1 for d in dils)]
    expand_ref, x_ref, wo_ref, g_ref, out_ref = refs[-5:]
    os_, lses, k = [], [], 0
    for gi, dil in enumerate(dils):
        o = jnp.concatenate([o_refs[gi][r] for r in range(dil)], axis=0)
        lse = jnp.concatenate([lse_refs[gi][r] for r in range(dil)], axis=0)
        if dil > 1:
            sel = unperm_refs[k][...]
            k += 1
            o = jnp.dot(sel, o, preferred_element_type=F32)
            lse = _onehot_dot(sel, lse, 3)
        else:
            o = o.astype(F32)
        os_.append(o)
        lses.append(lse)
    m = functools.reduce(jnp.maximum, lses)
    es = [jnp.exp(l - m) for l in lses]
    den = functools.reduce(jnp.add, es)
    ws = [_onehot_dot_rhs(e / den, expand_ref[...]) for e in es]
    o = functools.reduce(jnp.add, [w * o for w, o in zip(ws, os_)]).astype(BF16)
    mix = jnp.dot(o, wo_ref[...], preferred_element_type=F32)
    out_ref[...] = x_ref[...] + _rms_scale(mix) * g_ref[...]


def _onehot_dot_rhs(x, sel):
    return functools.reduce(
        jnp.add, [jnp.dot(p, sel, preferred_element_type=F32) for p in _split_bf16(x, 2)])


def _oproj_folded(os_, lses, unperms, expand, x, wo, g):
    nb, seq, _ = x.shape
    tm = ROW_TILE
    dils = tuple(o.shape[1] for o in os_)
    folded = lambda dil, w: pl.BlockSpec((None, dil, tm // dil, w), lambda b, i: (b, 0, i, 0))
    row = pl.BlockSpec((None, tm, D_MODEL), lambda b, i: (b, i, 0))
    in_specs = ([folded(d, D_MODEL) for d in dils] + [folded(d, LANES) for d in dils]
                + [_resident(u.shape) for u in unperms]
                + [_resident(expand.shape), row, _resident(wo.shape), _resident((1, D_MODEL))])
    return pl.pallas_call(
        functools.partial(_oproj_folded_kernel, dils=dils),
        grid=(nb, seq // tm),
        in_specs=in_specs,
        out_specs=row,
        out_shape=jax.ShapeDtypeStruct((nb, seq, D_MODEL), F32),
        compiler_params=_params(("parallel", "parallel")),
        name="oproj_folded",
    )(*os_, *lses, *unperms, expand, x, wo, g)


def _rope_tabs(pos):
    half = ROT_DIM // 2
    inv = ROPE_THETA ** (-jnp.arange(0, ROT_DIM, 2, dtype=F32) / ROT_DIM)
    ang = pos[:, None] * inv[None, :]
    cos, sin = jnp.cos(ang), jnp.sin(ang)
    n = pos.shape[0]
    pad = jnp.zeros((n, HEAD_DIM - ROT_DIM), F32)
    zer = jnp.zeros((n, half), F32)
    c = jnp.concatenate([cos, cos, pad + 1.0], axis=1)
    s1 = jnp.concatenate([-sin, zer, pad], axis=1)
    s2 = jnp.concatenate([zer, sin, pad], axis=1)
    return tuple(jnp.tile(t, (1, LANES // HEAD_DIM)) for t in (c, s1, s2))


def _band_bias():
    i = jnp.arange(QBLK)[:, None]
    j = jnp.arange(QBLK)[None, :]
    prev_ok = j >= i
    cur_ok = j <= i
    later = jnp.concatenate([prev_ok, cur_ok], axis=1)
    first = jnp.concatenate([jnp.zeros_like(prev_ok), cur_ok], axis=1)
    b = jnp.where(jnp.stack([first, later]), 0.0, NEG).astype(F32)
    return jnp.tile(b, (1, 2, 1))


def _fold_perm(dil):
    n = ROW_TILE // dil
    dst = jnp.arange(ROW_TILE)
    src = (dst % n) * dil + dst // n
    return (src[:, None] == jnp.arange(ROW_TILE)[None, :]).astype(BF16)


def _head_expand():
    return (jnp.arange(LANES)[:, None] == jnp.arange(D_MODEL)[None, :] // HEAD_DIM).astype(BF16)


def _heads_ig(w):
    k = w.shape[0]
    return w.reshape(k, N_KV, N_HEADS // N_KV, HEAD_DIM).transpose(0, 2, 1, 3).reshape(k, D_MODEL)


def _block_diag_q(q):
    r = q.shape[0]
    q4 = q.astype(F32).reshape(r, 4, 1, N_KV, HEAD_DIM)
    eye = jnp.eye(N_KV, dtype=F32)[None, None, :, :, None]
    return (q4 * eye).reshape(r, N_HEADS, D_KV)


def _own_head(o):
    r = o.shape[0]
    o5 = o.reshape(r, 4, N_KV, N_KV, HEAD_DIM)
    d = jnp.einsum('rigkd,gk->rigd', o5, jnp.eye(N_KV, dtype=o.dtype))
    return d.transpose(0, 2, 1, 3).reshape(r, D_MODEL)


def _expand_lse(lse):
    r = lse.shape[0]
    l = lse[:, :, 0].reshape(r, 4, N_KV).transpose(0, 2, 1).reshape(r, N_HEADS)
    return jnp.repeat(l, HEAD_DIM, axis=1)


def _window_minor(cache):
    r, w = cache.shape[:2]
    return cache.transpose(0, 2, 3, 4, 1).reshape(r, 2 * D_KV, w)


def _window_major(cache_t):
    r, _, w = cache_t.shape
    return cache_t.reshape(r, 2, N_KV, HEAD_DIM, w).transpose(0, 4, 1, 2, 3)


def kernel(x_prompt, x_sample, cache_a_kv, cache_b_kv_w128, cache_b_kv_w512, cache_b_kv_w2048,
           norm_g, w_ffn_gu, w_ffn_dn, w_qkv_a, sink_a, w_o_a, g_kv_b, w_kv_b, w_q_b, w_o_b):
    nb, seq, _ = x_prompt.shape
    nreq = x_sample.shape[0]
    tp = nb * seq
    b_caches = (cache_b_kv_w128, cache_b_kv_w512, cache_b_kv_w2048)

    def gain(l, k):
        return norm_g[l, k].reshape(1, D_MODEL)

    def ffn_w(l, k):
        gu = w_ffn_gu[l, k].astype(BF16)
        return gu[:, :D_FF], gu[:, D_FF:], w_ffn_dn[l, k].astype(BF16)

    bias = _band_bias()
    tabs_p = _rope_tabs(jnp.arange(seq, dtype=F32))
    tabs_s = _rope_tabs(jnp.zeros((nreq,), F32) + PAST_LEN)
    perms = {dil: _fold_perm(dil) for _, dil in B_GROUPS if dil > 1}

    wq_a = w_qkv_a[0][:, :D_MODEL]
    wkv_a = w_qkv_a[0][:, D_MODEL:].astype(BF16)
    wo_a = w_o_a[0].astype(BF16)
    wo_b = w_o_b[0].astype(BF16)
    g_kv = g_kv_b.reshape(1, D_MODEL)
    sink = sink_a[0].astype(F32)
    sink_rows = sink.reshape(N_KV, 4).T.reshape(N_HEADS, 1)

    x = x_prompt.reshape(tp, D_MODEL)
    x = _ffn(x, gain(0, 0), *ffn_w(0, 0), gain(0, 1))
    q, kv_a, k2, v2 = _proj(x.reshape(nb, seq, D_MODEL), gain(0, 2), gain(0, 2),
                            wq_a.astype(BF16), wkv_a, tabs_p, None, dil=1, win=WIN_A)
    o, = _band_attention(q, k2, v2, bias, sink, has_lse=False)
    x = _oproj([o.reshape(tp, D_MODEL)], [], x, wo_a, gain(0, 3))
    x = _ffn(x, gain(0, 4), *ffn_w(0, 1), gain(0, 5))
    x = _ffn(x, gain(1, 0), *ffn_w(1, 0), gain(1, 1)).reshape(nb, seq, D_MODEL)
    os_, lses, kv_b = [], [], []
    for gi, (win, dil) in enumerate(B_GROUPS):
        wq = w_q_b[0][:, gi * D_MODEL:(gi + 1) * D_MODEL].astype(BF16)
        wkv = w_kv_b[:, gi * 2 * D_KV:(gi + 1) * 2 * D_KV].astype(BF16)
        q, kvc, k2, v2 = _proj(x, gain(1, 2), g_kv, wq, wkv, tabs_p, perms.get(dil), dil=dil, win=win)
        o, lse = _band_attention(q, k2, v2, bias, None, has_lse=True)
        os_.append(o)
        lses.append(lse)
        kv_b.append(kvc.reshape(nb, win, 2, N_KV, HEAD_DIM))
    unperms = [perms[d].T for _, d in B_GROUPS if d > 1]
    x = _oproj_folded(os_, lses, unperms, _head_expand(), x, wo_b, gain(1, 3))
    y_p = _ffn(x.reshape(tp, D_MODEL), gain(1, 4), *ffn_w(1, 1), gain(1, 5)).reshape(nb, seq, D_MODEL)
    a_p = kv_a.reshape(1, nb, WIN_A, 2, N_KV, HEAD_DIM)

    def tap_bias(win, dil):
        return jnp.where(jnp.arange(win) % dil == 0, 0.0, NEG).astype(F32).reshape(1, win)

    xs = x_sample.reshape(nreq, D_MODEL)
    xs = _ffn(xs, gain(0, 0), *ffn_w(0, 0), gain(0, 1))
    q, new, newt = _proj_decode(xs, gain(0, 2), gain(0, 2), _heads_ig(wq_a).astype(BF16), wkv_a, tabs_s)
    o, _, cache_a_new = _decode_attention(_block_diag_q(q), _window_minor(cache_a_kv[0]),
                                          new.reshape(nreq, 1, 2 * D_KV), newt,
                                          tap_bias(WIN_A, 1), sink_rows)
    xs = _oproj([_own_head(o)], [], xs, wo_a, gain(0, 3))
    xs = _ffn(xs, gain(0, 4), *ffn_w(0, 1), gain(0, 5))
    xs = _ffn(xs, gain(1, 0), *ffn_w(1, 0), gain(1, 1))
    os_, lses, b_s = [], [], []
    for gi, (win, dil) in enumerate(B_GROUPS):
        wq = _heads_ig(w_q_b[0][:, gi * D_MODEL:(gi + 1) * D_MODEL]).astype(BF16)
        wkv = w_kv_b[:, gi * 2 * D_KV:(gi + 1) * 2 * D_KV].astype(BF16)
        q, new, newt = _proj_decode(xs, gain(1, 2), g_kv, wq, wkv, tabs_s)
        o, lse, cache_new = _decode_attention(_block_diag_q(q), _window_minor(b_caches[gi]),
                                              new.reshape(nreq, 1, 2 * D_KV), newt,
                                              tap_bias(win, dil), None)
        os_.append(_own_head(o))
        lses.append(_expand_lse(lse))
        b_s.append(_window_major(cache_new))
    xs = _oproj(os_, lses, xs, wo_b, gain(1, 3))
    y_s = _ffn(xs, gain(1, 4), *ffn_w(1, 1), gain(1, 5)).reshape(nreq, 1, D_MODEL)
    a_s = _window_major(cache_a_new)[None]

    return (y_p, y_s, a_p, kv_b[0], kv_b[1], kv_b[2], a_s, b_s[0], b_s[1], b_s[2])
```

```python
import functools

import jax
import jax.numpy as jnp
from jax.experimental import pallas as pl
from jax.experimental.pallas import tpu as pltpu

F32 = jnp.float32
BF16 = jnp.bfloat16

D_MODEL = 1024
HEAD_DIM = 64
ROT_DIM = HEAD_DIM // 4
ROPE_THETA = 500000.0
N_HEADS = 16
N_KV = 4
D_KV = N_KV * HEAD_DIM
D_FF = 2816
EPS = 1e-6
PAST_LEN = 8192
WIN_A = 128
B_GROUPS = ((128, 1), (512, 4), (2048, 16))

LANES = 128
FF_CHUNK = 256
ROW_TILE = 512
QBLK = 128
NEG = -1e30
LOG2E = 1.4426950408889634
LN2 = 0.6931471805599453
Q_SCALE = HEAD_DIM ** -0.5 * LOG2E
MAX_QBLKS = 4
VMEM_LIMIT = 56 * 1024 * 1024
DECODE_BLOCK_BYTES = 4 * 1024 * 1024


def _params(sem):
    return pltpu.CompilerParams(dimension_semantics=sem, vmem_limit_bytes=VMEM_LIMIT)


def _resident(shape):
    nd = len(shape)
    return pl.BlockSpec(shape, lambda *_: (0,) * nd, pipeline_mode=pl.Buffered(1))


def _rms_scale(x):
    return x * jax.lax.rsqrt(jnp.mean(x * x, axis=-1, keepdims=True) + EPS)


def _split_bf16(x, parts):
    out = []
    for _ in range(parts - 1):
        hi = x.astype(BF16)
        out.append(hi)
        x = x - hi.astype(F32)
    out.append(x.astype(BF16))
    return out


def _onehot_dot(sel, x, parts):
    return functools.reduce(
        jnp.add, [jnp.dot(sel, p, preferred_element_type=F32) for p in _split_bf16(x, parts)])


def _ffn_kernel(x_ref, gpre_ref, wg_ref, wu_ref, wd_ref, gpost_ref, o_ref):
    x = x_ref[...]
    xn = (_rms_scale(x) * gpre_ref[...]).astype(BF16)
    acc = None
    for c in range(D_FF // FF_CHUNK):
        sl = slice(c * FF_CHUNK, (c + 1) * FF_CHUNK)
        gate = jnp.dot(xn, wg_ref[:, sl], preferred_element_type=F32)
        up = jnp.dot(xn, wu_ref[:, sl], preferred_element_type=F32)
        act = (gate * jax.nn.sigmoid(gate) * up).astype(BF16)
        part = jnp.dot(act, wd_ref[sl, :], preferred_element_type=F32)
        acc = part if acc is None else acc + part
    o_ref[...] = x + 0.5 * (_rms_scale(acc) * gpost_ref[...])


def _ffn(x, gpre, wg, wu, wd, gpost):
    t = x.shape[0]
    tm = min(ROW_TILE, t)
    row = pl.BlockSpec((tm, D_MODEL), lambda i: (i, 0))
    return pl.pallas_call(
        _ffn_kernel,
        grid=(t // tm,),
        in_specs=[row, _resident((1, D_MODEL)), _resident(wg.shape), _resident(wu.shape),
                  _resident(wd.shape), _resident((1, D_MODEL))],
        out_specs=row,
        out_shape=jax.ShapeDtypeStruct((t, D_MODEL), F32),
        compiler_params=_params(("parallel",)),
        name="ffn",
    )(x, gpre, wg, wu, wd, gpost)


def _project(x_ref, gq_ref, gk_ref, wq_ref, wkv_ref, rc_ref, rs1_ref, rs2_ref):
    xh = _rms_scale(x_ref[...])
    hq = (xh * gq_ref[...]).astype(BF16)
    hk = (xh * gk_ref[...]).astype(BF16)
    rc, rs1, rs2 = rc_ref[...], rs1_ref[...], rs2_ref[...]

    def rope(t):
        return t * rc + pltpu.roll(t, LANES - ROT_DIM // 2, 1) * rs1 + pltpu.roll(t, ROT_DIM // 2, 1) * rs2

    q = jnp.dot(hq, wq_ref[...], preferred_element_type=F32)
    qs = [(rope(q[:, c * LANES:(c + 1) * LANES]) * Q_SCALE).astype(BF16)
          for c in range(D_MODEL // LANES)]
    kv = jnp.dot(hk, wkv_ref[...], preferred_element_type=F32)
    kvs = []
    for c in range(2 * D_KV // LANES):
        t = kv[:, c * LANES:(c + 1) * LANES]
        kvs.append(rope(t) if c < D_KV // LANES else t)
    return qs, kvs


def _dup_heads(kvs):
    rows = kvs[0].shape[0]
    lo = jax.lax.broadcasted_iota(jnp.int32, (rows, LANES), 1) < HEAD_DIM
    out = []
    for t in kvs:
        sw = pltpu.roll(t, HEAD_DIM, 1)
        out.append(jnp.where(lo, t, sw).astype(BF16))
        out.append(jnp.where(lo, sw, t).astype(BF16))
    return out


def _proj_kernel(*refs, dil, cache_rows):
    ins = refs[:8]
    perm_ref = refs[8] if dil > 1 else None
    q_ref, kv_ref, k2_ref, v2_ref = refs[-4:]
    tm = ins[0].shape[0]
    qs, kvs = _project(*ins)
    for c, t in enumerate(kvs):
        kv_ref[:, c * LANES:(c + 1) * LANES] = t[tm - cache_rows:, :]
    dup = _dup_heads(kvs)
    cols = jnp.concatenate(qs + dup, axis=1)
    if dil > 1:
        cols = jnp.dot(perm_ref[...], cols, preferred_element_type=F32).astype(BF16)
    n = tm // dil
    for r in range(dil):
        blk = cols[r * n:(r + 1) * n, :]
        q_ref[r] = blk[:, :D_MODEL]
        k2_ref[r] = blk[:, D_MODEL:D_MODEL + 2 * D_KV]
        v2_ref[r] = blk[:, D_MODEL + 2 * D_KV:]


def _proj(x, gq, gk, wq, wkv, tabs, perm, *, dil, win):
    nb, seq, _ = x.shape
    tm = ROW_TILE
    nt = seq // tm
    cache_rows = min(win, tm)
    first_cache_tile = nt - win // cache_rows
    n = tm // dil
    tab = pl.BlockSpec((tm, LANES), lambda b, i: (i, 0))
    folded = lambda w: pl.BlockSpec((None, dil, n, w), lambda b, i: (b, 0, i, 0))
    in_specs = [pl.BlockSpec((None, tm, D_MODEL), lambda b, i: (b, i, 0)),
                _resident((1, D_MODEL)), _resident((1, D_MODEL)),
                _resident(wq.shape), _resident(wkv.shape), tab, tab, tab]
    args = [x, gq, gk, wq, wkv, *tabs]
    if dil > 1:
        in_specs.append(_resident(perm.shape))
        args.append(perm)
    return pl.pallas_call(
        functools.partial(_proj_kernel, dil=dil, cache_rows=cache_rows),
        grid=(nb, nt),
        in_specs=in_specs,
        out_specs=[folded(D_MODEL),
                   pl.BlockSpec((None, cache_rows, 2 * D_KV),
                                lambda b, i: (b, jnp.maximum(i - first_cache_tile, 0), 0)),
                   folded(2 * D_KV), folded(2 * D_KV)],
        out_shape=[jax.ShapeDtypeStruct((nb, dil, seq // dil, D_MODEL), BF16),
                   jax.ShapeDtypeStruct((nb, win, 2 * D_KV), F32),
                   jax.ShapeDtypeStruct((nb, dil, seq // dil, 2 * D_KV), BF16),
                   jax.ShapeDtypeStruct((nb, dil, seq // dil, 2 * D_KV), BF16)],
        compiler_params=_params(("parallel", "arbitrary")),
        name="proj",
    )(*args)


def _proj_decode_kernel(*refs):
    q_ref, new_ref, newt_ref = refs[-3:]
    qs, kvs = _project(*refs[:8])
    for c, t in enumerate(qs):
        q_ref[:, c * LANES:(c + 1) * LANES] = t
    for c, t in enumerate(kvs):
        new_ref[:, c * LANES:(c + 1) * LANES] = t
        newt_ref[c * LANES:(c + 1) * LANES, :] = t.T


def _proj_decode(x, gq, gk, wq, wkv, tabs):
    nreq = x.shape[0]
    full = lambda w: pl.BlockSpec((nreq, w), lambda i: (0, 0))
    return pl.pallas_call(
        _proj_decode_kernel,
        grid=(1,),
        in_specs=[full(D_MODEL), _resident((1, D_MODEL)), _resident((1, D_MODEL)),
                  _resident(wq.shape), _resident(wkv.shape), full(LANES), full(LANES), full(LANES)],
        out_specs=[full(D_MODEL), full(2 * D_KV), pl.BlockSpec((2 * D_KV, nreq), lambda i: (0, 0))],
        out_shape=[jax.ShapeDtypeStruct((nreq, D_MODEL), BF16),
                   jax.ShapeDtypeStruct((nreq, 2 * D_KV), F32),
                   jax.ShapeDtypeStruct((2 * D_KV, nreq), F32)],
        compiler_params=_params(("arbitrary",)),
        name="proj_decode",
    )(x, gq, gk, wq, wkv, *tabs)


def _lse_lane(head):
    return (head % 2) * HEAD_DIM + head - head % 2


def _band_kernel(*refs, has_sink, has_lse, nq):
    refs = list(refs)
    sink_ref = refs.pop(0) if has_sink else None
    q_ref, kp_ref, kc_ref, vp_ref, vc_ref, bias_ref, o_ref = refs[:7]
    lse_ref = refs[7] if has_lse else None

    lo = jax.lax.broadcasted_iota(jnp.int32, (2 * QBLK, LANES), 1) < HEAD_DIM
    lane_q = jax.lax.broadcasted_iota(jnp.int32, (QBLK, LANES), 1)
    top = jax.lax.broadcasted_iota(jnp.int32, (2 * QBLK, 1), 0) < QBLK
    zero = jnp.zeros((), BF16)
    krow_lo = jax.lax.broadcasted_iota(jnp.int32, (4 * QBLK, LANES), 0) < 2 * QBLK
    klane_lo = jax.lax.broadcasted_iota(jnp.int32, (4 * QBLK, LANES), 1) < HEAD_DIM
    ones_sel = jnp.where(krow_lo == klane_lo, 1.0, 0.0).astype(BF16)
    first_step = pl.program_id(2) == 0

    for t in range(nq):
        rows = slice(t * QBLK, (t + 1) * QBLK)
        bias = bias_ref[1] if t else bias_ref[jnp.where(first_step, 0, 1)]
        lse_tile = jnp.zeros((QBLK, LANES), F32)
        for h in range(N_KV):
            ksl = slice(h * LANES, (h + 1) * LANES)
            kprev = kc_ref[(t - 1) * QBLK:t * QBLK, ksl] if t else kp_ref[:, ksl]
            vprev = vc_ref[(t - 1) * QBLK:t * QBLK, ksl] if t else vp_ref[:, ksl]
            k2 = jnp.concatenate([kprev, kc_ref[rows, ksl]], axis=0)
            v2 = jnp.concatenate([vprev, vc_ref[rows, ksl]], axis=0)
            kbd = jnp.concatenate([jnp.where(lo, k2, zero), jnp.where(lo, zero, k2)], axis=0)
            vbd = jnp.concatenate([jnp.where(lo, v2, zero), jnp.where(lo, zero, v2)], axis=0)
            vext = jnp.concatenate([vbd, ones_sel], axis=1)
            qq = jnp.concatenate([q_ref[rows, (2 * h) * LANES:(2 * h + 1) * LANES],
                                  q_ref[rows, (2 * h + 1) * LANES:(2 * h + 2) * LANES]], axis=0)
            s = jax.lax.dot_general(qq, kbd, (((1,), (1,)), ((), ())),
                                    preferred_element_type=F32)
            ps, ms, sks = [], [], []
            for c in range(2):
                sc = s[:, c * 2 * QBLK:(c + 1) * 2 * QBLK] + bias
                m = jnp.max(sc, axis=-1, keepdims=True)
                if has_sink:
                    sk = jnp.where(top, sink_ref[4 * h + c], sink_ref[4 * h + 2 + c]) * LOG2E
                    m = jnp.maximum(m, sk)
                    sks.append(sk)
                ps.append(jnp.exp2(sc - m).astype(BF16))
                ms.append(m)
            ov = jnp.dot(jnp.concatenate(ps, axis=1), vext, preferred_element_type=F32)
            l = ov[:, LANES:]
            if has_sink:
                l = l + jnp.where(lo, jnp.exp2(sks[0] - ms[0]), jnp.exp2(sks[1] - ms[1]))
            o = ov[:, :LANES] / l
            o_ref[rows, (2 * h) * LANES:(2 * h + 1) * LANES] = o[:QBLK].astype(o_ref.dtype)
            o_ref[rows, (2 * h + 1) * LANES:(2 * h + 2) * LANES] = o[QBLK:].astype(o_ref.dtype)
            if has_lse:
                lse = jnp.where(lo, ms[0], ms[1]) * LN2 + jnp.log(l)
                for j in range(2):
                    pair = (lane_q == _lse_lane(4 * h + 2 * j)) | (lane_q == _lse_lane(4 * h + 2 * j + 1))
                    lse_tile = jnp.where(pair, lse[j * QBLK:(j + 1) * QBLK], lse_tile)
        if has_lse:
            lse_ref[rows, :] = lse_tile


def _band_attention(q, k2, v2, bias, sink, *, has_lse):
    nb, dil, length, _ = q.shape
    nq = min(MAX_QBLKS, length // QBLK)
    grid = (nb, dil, length // (nq * QBLK))
    has_sink = sink is not None
    cur = lambda w: pl.BlockSpec((None, None, nq * QBLK, w), lambda b, r, i: (b, r, i, 0))
    prev = lambda w: pl.BlockSpec((None, None, QBLK, w),
                                  lambda b, r, i: (b, r, jnp.maximum(i * nq - 1, 0), 0))
    in_specs = [cur(D_MODEL), prev(2 * D_KV), cur(2 * D_KV), prev(2 * D_KV), cur(2 * D_KV),
                _resident(bias.shape)]
    args = [q, k2, k2, v2, v2, bias]
    if has_sink:
        in_specs = [pl.BlockSpec(memory_space=pltpu.SMEM)] + in_specs
        args = [sink] + args
    out_specs = [cur(D_MODEL)]
    out_shape = [jax.ShapeDtypeStruct((nb, dil, length, D_MODEL), BF16)]
    if has_lse:
        out_specs.append(cur(LANES))
        out_shape.append(jax.ShapeDtypeStruct((nb, dil, length, LANES), F32))
    return pl.pallas_call(
        functools.partial(_band_kernel, has_sink=has_sink, has_lse=has_lse, nq=nq),
        grid=grid, in_specs=in_specs, out_specs=out_specs, out_shape=out_shape,
        compiler_params=_params(("parallel", "parallel", "parallel")),
        name="band_attn",
    )(*args)


def _decode_kernel(*refs, has_sink, rb):
    refs = list(refs)
    sink_ref = refs.pop(0) if has_sink else None
    qbd_ref, cache_ref, new_ref, newt_ref, tap_ref, o_ref, lse_ref, cout_ref = refs
    win = cache_ref.shape[2]
    base = pl.program_id(0) * rb
    lane = jax.lax.broadcasted_iota(jnp.int32, (2 * D_KV, LANES), 1)
    tap_bias = tap_ref[...]

    def body(b, carry):
        qb = qbd_ref[b]
        cache = cache_ref[b]
        kn = new_ref[b, :, 0:D_KV]
        vn = new_ref[b, :, D_KV:2 * D_KV]
        s = jnp.dot(qb.astype(BF16), cache[0:D_KV].astype(BF16),
                    preferred_element_type=F32) + tap_bias
        sn = jnp.sum(qb * kn, axis=-1, keepdims=True)
        m = jnp.maximum(jnp.max(s, axis=-1, keepdims=True), sn)
        if has_sink:
            sk = sink_ref[...] * LOG2E
            m = jnp.maximum(m, sk)
        p = jnp.exp2(s - m)
        pn = jnp.exp2(sn - m)
        l = jnp.sum(p, axis=-1, keepdims=True) + pn
        if has_sink:
            l = l + jnp.exp2(sk - m)
        o = jax.lax.dot_general(p.astype(BF16), cache[D_KV:2 * D_KV].astype(BF16),
                                (((1,), (1,)), ((), ())), preferred_element_type=F32) + pn * vn
        o_ref[b] = o / l
        lse_ref[b] = jnp.broadcast_to(m * LN2 + jnp.log(l), (N_HEADS, LANES))
        shifted = pltpu.roll(cache, win - 1, 1)
        newcol = pltpu.roll(newt_ref[...], LANES - 1 - (base + b), 1)
        cout_ref[b] = shifted
        cout_ref[b, :, win - LANES:win] = jnp.where(lane == LANES - 1, newcol, shifted[:, win - LANES:win])
        return carry

    jax.lax.fori_loop(0, rb, body, 0)


def _decode_attention(qbd, cache, new, newt, tap_bias, sink_rows):
    nreq, feat, win = cache.shape
    rb = max(1, DECODE_BLOCK_BYTES // (feat * win * 4))
    has_sink = sink_rows is not None
    req = lambda *tail: pl.BlockSpec((rb,) + tail, lambda i: (i,) + (0,) * len(tail))
    in_specs = [req(N_HEADS, D_KV), req(feat, win), req(1, 2 * D_KV),
                pl.BlockSpec((feat, nreq), lambda i: (0, 0)),
                pl.BlockSpec((1, win), lambda i: (0, 0))]
    args = [qbd, cache, new, newt, tap_bias]
    if has_sink:
        in_specs = [pl.BlockSpec((N_HEADS, 1), lambda i: (0, 0))] + in_specs
        args = [sink_rows] + args
    return pl.pallas_call(
        functools.partial(_decode_kernel, has_sink=has_sink, rb=rb),
        grid=(nreq // rb,), in_specs=in_specs,
        out_specs=[req(N_HEADS, D_KV), req(N_HEADS, LANES), req(feat, win)],
        out_shape=[jax.ShapeDtypeStruct((nreq, N_HEADS, D_KV), F32),
                   jax.ShapeDtypeStruct((nreq, N_HEADS, LANES), F32),
                   jax.ShapeDtypeStruct(cache.shape, F32)],
        compiler_params=_params(("parallel",)),
        name="decode_attn",
    )(*args)


def _oproj_kernel(*refs, n_groups):
    o_refs = refs[:n_groups]
    lse_refs = refs[n_groups:2 * n_groups] if n_groups > 1 else ()
    x_ref, wo_ref, g_ref, out_ref = refs[-4:]
    if n_groups == 1:
        o = o_refs[0][...].astype(BF16)
    else:
        lses = [r[...] for r in lse_refs]
        m = functools.reduce(jnp.maximum, lses)
        es = [jnp.exp(l - m) for l in lses]
        den = functools.reduce(jnp.add, es)
        o = functools.reduce(jnp.add, [(e / den) * r[...].astype(F32)
                                       for e, r in zip(es, o_refs)]).astype(BF16)
    mix = jnp.dot(o, wo_ref[...], preferred_element_type=F32)
    out_ref[...] = x_ref[...] + _rms_scale(mix) * g_ref[...]


def _oproj(os_, lses, x, wo, g):
    t = x.shape[0]
    tm = min(ROW_TILE, t)
    row = pl.BlockSpec((tm, D_MODEL), lambda i: (i, 0))
    n_groups = len(os_)
    return pl.pallas_call(
        functools.partial(_oproj_kernel, n_groups=n_groups),
        grid=(t // tm,),
        in_specs=[row] * (n_groups + len(lses) + 1) + [_resident(wo.shape), _resident((1, D_MODEL))],
        out_specs=row,
        out_shape=jax.ShapeDtypeStruct((t, D_MODEL), F32),
        compiler_params=_params(("parallel",)),
        name="oproj",
    )(*os_, *lses, x, wo, g)


def _oproj_folded_kernel(*refs, dils):
    ng = len(dils)
    o_refs, lse_refs = refs[:ng], refs[ng:2 * ng]
    unperm_refs = refs[2 * ng:2 * ng + sum(d > 1 for d in dils)]
    expand_ref, x_ref, wo_ref, g_ref, out_ref = refs[-5:]
    os_, lses, k = [], [], 0
    for gi, dil in enumerate(dils):
        o = jnp.concatenate([o_refs[gi][r] for r in range(dil)], axis=0)
        lse = jnp.concatenate([lse_refs[gi][r] for r in range(dil)], axis=0)
        if dil > 1:
            sel = unperm_refs[k][...]
            k += 1
            o = jnp.dot(sel, o, preferred_element_type=F32)
            lse = _onehot_dot(sel, lse, 3)
        else:
            o = o.astype(F32)
        os_.append(o)
        lses.append(lse)
    m = functools.reduce(jnp.maximum, lses)
    es = [jnp.exp(l - m) for l in lses]
    den = functools.reduce(jnp.add, es)
    ws = [_onehot_dot_rhs(e / den, expand_ref[...]) for e in es]
    o = functools.reduce(jnp.add, [w * o for w, o in zip(ws, os_)]).astype(BF16)
    mix = jnp.dot(o, wo_ref[...], preferred_element_type=F32)
    out_ref[...] = x_ref[...] + _rms_scale(mix) * g_ref[...]


def _onehot_dot_rhs(x, sel):
    return functools.reduce(
        jnp.add, [jnp.dot(p, sel, preferred_element_type=F32) for p in _split_bf16(x, 2)])


def _oproj_folded(os_, lses, unperms, expand, x, wo, g):
    nb, seq, _ = x.shape
    tm = ROW_TILE
    dils = tuple(o.shape[1] for o in os_)
    folded = lambda dil, w: pl.BlockSpec((None, dil, tm // dil, w), lambda b, i: (b, 0, i, 0))
    row = pl.BlockSpec((None, tm, D_MODEL), lambda b, i: (b, i, 0))
    in_specs = ([folded(d, D_MODEL) for d in dils] + [folded(d, LANES) for d in dils]
                + [_resident(u.shape) for u in unperms]
                + [_resident(expand.shape), row, _resident(wo.shape), _resident((1, D_MODEL))])
    return pl.pallas_call(
        functools.partial(_oproj_folded_kernel, dils=dils),
        grid=(nb, seq // tm),
        in_specs=in_specs,
        out_specs=row,
        out_shape=jax.ShapeDtypeStruct((nb, seq, D_MODEL), F32),
        compiler_params=_params(("parallel", "parallel")),
        name="oproj_folded",
    )(*os_, *lses, *unperms, expand, x, wo, g)


def _rope_tabs(pos):
    half = ROT_DIM // 2
    inv = ROPE_THETA ** (-jnp.arange(0, ROT_DIM, 2, dtype=F32) / ROT_DIM)
    ang = pos[:, None] * inv[None, :]
    cos, sin = jnp.cos(ang), jnp.sin(ang)
    n = pos.shape[0]
    pad = jnp.zeros((n, HEAD_DIM - ROT_DIM), F32)
    zer = jnp.zeros((n, half), F32)
    c = jnp.concatenate([cos, cos, pad + 1.0], axis=1)
    s1 = jnp.concatenate([-sin, zer, pad], axis=1)
    s2 = jnp.concatenate([zer, sin, pad], axis=1)
    return tuple(jnp.tile(t, (1, LANES // HEAD_DIM)) for t in (c, s1, s2))


def _band_bias():
    i = jnp.arange(QBLK)[:, None]
    j = jnp.arange(QBLK)[None, :]
    prev_ok = j >= i
    cur_ok = j <= i
    later = jnp.concatenate([prev_ok, cur_ok], axis=1)
    first = jnp.concatenate([jnp.zeros_like(prev_ok), cur_ok], axis=1)
    b = jnp.where(jnp.stack([first, later]), 0.0, NEG).astype(F32)
    return jnp.tile(b, (1, 2, 1))


def _fold_perm(dil):
    n = ROW_TILE // dil
    dst = jnp.arange(ROW_TILE)
    src = (dst % n) * dil + dst // n
    return (src[:, None] == jnp.arange(ROW_TILE)[None, :]).astype(BF16)


def _head_expand():
    return (jnp.arange(LANES)[:, None] == _lse_lane(jnp.arange(D_MODEL)[None, :] // HEAD_DIM)).astype(BF16)


def _heads_ig(w):
    k = w.shape[0]
    return w.reshape(k, N_KV, N_HEADS // N_KV, HEAD_DIM).transpose(0, 2, 1, 3).reshape(k, D_MODEL)


def _block_diag_q(q):
    r = q.shape[0]
    q4 = q.astype(F32).reshape(r, 4, 1, N_KV, HEAD_DIM)
    eye = jnp.eye(N_KV, dtype=F32)[None, None, :, :, None]
    return (q4 * eye).reshape(r, N_HEADS, D_KV)


def _own_head(o):
    r = o.shape[0]
    o5 = o.reshape(r, 4, N_KV, N_KV, HEAD_DIM)
    d = jnp.einsum('rigkd,gk->rigd', o5, jnp.eye(N_KV, dtype=o.dtype))
    return d.transpose(0, 2, 1, 3).reshape(r, D_MODEL)


def _expand_lse(lse):
    r = lse.shape[0]
    l = lse[:, :, 0].reshape(r, 4, N_KV).transpose(0, 2, 1).reshape(r, N_HEADS)
    return jnp.repeat(l, HEAD_DIM, axis=1)


def _window_minor(cache):
    r, w = cache.shape[:2]
    return cache.transpose(0, 2, 3, 4, 1).reshape(r, 2 * D_KV, w)


def _window_major(cache_t):
    r, _, w = cache_t.shape
    return cache_t.reshape(r, 2, N_KV, HEAD_DIM, w).transpose(0, 4, 1, 2, 3)


def kernel(x_prompt, x_sample, cache_a_kv, cache_b_kv_w128, cache_b_kv_w512, cache_b_kv_w2048,
           norm_g, w_ffn_gu, w_ffn_dn, w_qkv_a, sink_a, w_o_a, g_kv_b, w_kv_b, w_q_b, w_o_b):
    nb, seq, _ = x_prompt.shape
    nreq = x_sample.shape[0]
    tp = nb * seq
    b_caches = (cache_b_kv_w128, cache_b_kv_w512, cache_b_kv_w2048)

    def gain(l, k):
        return norm_g[l, k].reshape(1, D_MODEL)

    def ffn_w(l, k):
        gu = w_ffn_gu[l, k].astype(BF16)
        return gu[:, :D_FF], gu[:, D_FF:], w_ffn_dn[l, k].astype(BF16)

    bias = _band_bias()
    tabs_p = _rope_tabs(jnp.arange(seq, dtype=F32))
    tabs_s = _rope_tabs(jnp.zeros((nreq,), F32) + PAST_LEN)
    perms = {dil: _fold_perm(dil) for _, dil in B_GROUPS if dil > 1}

    wq_a = w_qkv_a[0][:, :D_MODEL]
    wkv_a = w_qkv_a[0][:, D_MODEL:].astype(BF16)
    wo_a = w_o_a[0].astype(BF16)
    wo_b = w_o_b[0].astype(BF16)
    g_kv = g_kv_b.reshape(1, D_MODEL)
    sink = sink_a[0].astype(F32)
    sink_rows = sink.reshape(N_KV, 4).T.reshape(N_HEADS, 1)

    x = x_prompt.reshape(tp, D_MODEL)
    x = _ffn(x, gain(0, 0), *ffn_w(0, 0), gain(0, 1))
    q, kv_a, k2, v2 = _proj(x.reshape(nb, seq, D_MODEL), gain(0, 2), gain(0, 2),
                            wq_a.astype(BF16), wkv_a, tabs_p, None, dil=1, win=WIN_A)
    o, = _band_attention(q, k2, v2, bias, sink, has_lse=False)
    x = _oproj([o.reshape(tp, D_MODEL)], [], x, wo_a, gain(0, 3))
    x = _ffn(x, gain(0, 4), *ffn_w(0, 1), gain(0, 5))
    x = _ffn(x, gain(1, 0), *ffn_w(1, 0), gain(1, 1)).reshape(nb, seq, D_MODEL)
    os_, lses, kv_b = [], [], []
    for gi, (win, dil) in enumerate(B_GROUPS):
        wq = w_q_b[0][:, gi * D_MODEL:(gi + 1) * D_MODEL].astype(BF16)
        wkv = w_kv_b[:, gi * 2 * D_KV:(gi + 1) * 2 * D_KV].astype(BF16)
        q, kvc, k2, v2 = _proj(x, gain(1, 2), g_kv, wq, wkv, tabs_p, perms.get(dil), dil=dil, win=win)
        o, lse = _band_attention(q, k2, v2, bias, None, has_lse=True)
        os_.append(o)
        lses.append(lse)
        kv_b.append(kvc.reshape(nb, win, 2, N_KV, HEAD_DIM))
    unperms = [perms[d].T for _, d in B_GROUPS if d > 1]
    x = _oproj_folded(os_, lses, unperms, _head_expand(), x, wo_b, gain(1, 3))
    y_p = _ffn(x.reshape(tp, D_MODEL), gain(1, 4), *ffn_w(1, 1), gain(1, 5)).reshape(nb, seq, D_MODEL)
    a_p = kv_a.reshape(1, nb, WIN_A, 2, N_KV, HEAD_DIM)

    def tap_bias(win, dil):
        return jnp.where(jnp.arange(win) % dil == 0, 0.0, NEG).astype(F32).reshape(1, win)

    xs = x_sample.reshape(nreq, D_MODEL)
    xs = _ffn(xs, gain(0, 0), *ffn_w(0, 0), gain(0, 1))
    q, new, newt = _proj_decode(xs, gain(0, 2), gain(0, 2), _heads_ig(wq_a).astype(BF16), wkv_a, tabs_s)
    o, _, cache_a_new = _decode_attention(_block_diag_q(q), _window_minor(cache_a_kv[0]),
                                          new.reshape(nreq, 1, 2 * D_KV), newt,
                                          tap_bias(WIN_A, 1), sink_rows)
    xs = _oproj([_own_head(o)], [], xs, wo_a, gain(0, 3))
    xs = _ffn(xs, gain(0, 4), *ffn_w(0, 1), gain(0, 5))
    xs = _ffn(xs, gain(1, 0), *ffn_w(1, 0), gain(1, 1))
    os_, lses, b_s = [], [], []
    for gi, (win, dil) in enumerate(B_GROUPS):
        wq = _heads_ig(w_q_b[0][:, gi * D_MODEL:(gi + 1) * D_MODEL]).astype(BF16)
        wkv = w_kv_b[:, gi * 2 * D_KV:(gi + 1) * 2 * D_KV].astype(BF16)
        q, new, newt = _proj_decode(xs, gain(1, 2), g_kv, wq, wkv, tabs_s)
        o, lse, cache_new = _decode_attention(_block_diag_q(q), _window_minor(b_caches[gi]),
                                              new.reshape(nreq, 1, 2 * D_KV), newt,
                                              tap_bias(win, dil), None)
        os_.append(_own_head(o))
        lses.append(_expand_lse(lse))
        b_s.append(_window_major(cache_new))
    xs = _oproj(os_, lses, xs, wo_b, gain(1, 3))
    y_s = _ffn(xs, gain(1, 4), *ffn_w(1, 1), gain(1, 5)).reshape(nreq, 1, D_MODEL)
    a_s = _window_major(cache_a_new)[None]

    return (y_p, y_s, a_p, kv_b[0], kv_b[1], kv_b[2], a_s, b_s[0], b_s[1], b_s[2])
```

```python
import functools

import jax
import jax.numpy as jnp
from jax.experimental import pallas as pl
from jax.experimental.pallas import tpu as pltpu

F32 = jnp.float32
BF16 = jnp.bfloat16

D_MODEL = 1024
HEAD_DIM = 64
ROT_DIM = HEAD_DIM // 4
ROPE_THETA = 500000.0
N_HEADS = 16
N_KV = 4
D_KV = N_KV * HEAD_DIM
D_FF = 2816
EPS = 1e-6
PAST_LEN = 8192
WIN_A = 128
B_GROUPS = ((128, 1), (512, 4), (2048, 16))

LANES = 128
FF_CHUNK = 256
ROW_TILE = 512
QBLK = 128
NEG = -1e30
LOG2E = 1.4426950408889634
LN2 = 0.6931471805599453
Q_SCALE = HEAD_DIM ** -0.5 * LOG2E
MAX_QBLKS = 4
VMEM_LIMIT = 56 * 1024 * 1024
DECODE_BLOCK_BYTES = 4 * 1024 * 1024
SHIFT_ROWS = 64


def _params(sem):
    return pltpu.CompilerParams(dimension_semantics=sem, vmem_limit_bytes=VMEM_LIMIT)


def _resident(shape):
    nd = len(shape)
    return pl.BlockSpec(shape, lambda *_: (0,) * nd, pipeline_mode=pl.Buffered(1))


def _rms_scale(x):
    return x * jax.lax.rsqrt(jnp.mean(x * x, axis=-1, keepdims=True) + EPS)


def _split_bf16(x, parts):
    out = []
    for _ in range(parts - 1):
        hi = x.astype(BF16)
        out.append(hi)
        x = x - hi.astype(F32)
    out.append(x.astype(BF16))
    return out


def _onehot_dot(sel, x, parts):
    return functools.reduce(
        jnp.add, [jnp.dot(sel, p, preferred_element_type=F32) for p in _split_bf16(x, parts)])


def _ffn_kernel(x_ref, gpre_ref, wg_ref, wu_ref, wd_ref, gpost_ref, o_ref, side_work=()):
    x = x_ref[...]
    xn = (_rms_scale(x) * gpre_ref[...]).astype(BF16)
    acc = None
    n_chunks = D_FF // FF_CHUNK
    for c in range(n_chunks):
        sl = slice(c * FF_CHUNK, (c + 1) * FF_CHUNK)
        gate = jnp.dot(xn, wg_ref[:, sl], preferred_element_type=F32)
        up = jnp.dot(xn, wu_ref[:, sl], preferred_element_type=F32)
        act = (gate * jax.nn.sigmoid(gate) * up).astype(BF16)
        part = jnp.dot(act, wd_ref[sl, :], preferred_element_type=F32)
        acc = part if acc is None else acc + part
        for work in side_work[c::n_chunks]:
            work()
    o_ref[...] = x + 0.5 * (_rms_scale(acc) * gpost_ref[...])


def _ffn_specs(t, wg, wu, wd):
    tm = min(ROW_TILE, t)
    row = pl.BlockSpec((tm, D_MODEL), lambda i: (i, 0))
    in_specs = [row, _resident((1, D_MODEL)), _resident(wg.shape), _resident(wu.shape),
                _resident(wd.shape), _resident((1, D_MODEL))]
    return t // tm, in_specs, row


def _ffn(x, gpre, wg, wu, wd, gpost):
    t = x.shape[0]
    steps, in_specs, row = _ffn_specs(t, wg, wu, wd)
    return pl.pallas_call(
        _ffn_kernel,
        grid=(steps,),
        in_specs=in_specs,
        out_specs=row,
        out_shape=jax.ShapeDtypeStruct((t, D_MODEL), F32),
        compiler_params=_params(("parallel",)),
        name="ffn",
    )(x, gpre, wg, wu, wd, gpost)


def _ffn_decode_kernel(*refs, n_jobs, req0, has_prev):
    per_job = 6 if has_prev else 5
    ffn_in = refs[:6]
    job_in = [refs[6 + j * per_job:6 + (j + 1) * per_job] for j in range(n_jobs)]
    outs = refs[6 + n_jobs * per_job:]
    req = req0 + pl.program_id(0)
    side_work = []
    for j in range(n_jobs):
        qbd_ref, cache_ref, new_ref, newt_ref, tap_ref = job_in[j][:5]
        o_ref, lse_ref, cout_ref = outs[1 + 3 * j:4 + 3 * j]
        side_work.append(functools.partial(
            _decode_attend, qbd_ref, cache_ref, 0, new_ref, tap_ref, None, o_ref, lse_ref))
        for r0 in range(0, 2 * D_KV, SHIFT_ROWS):
            side_work.append(functools.partial(
                _shift_rows, cache_ref, 0, newt_ref, req, cout_ref, r0, r0 + SHIFT_ROWS))
    _ffn_kernel(*ffn_in, outs[0], side_work=side_work)


def _ffn_decode(x, gpre, wg, wu, wd, gpost, jobs, req0, prev_caches):
    t = x.shape[0]
    steps, in_specs, row = _ffn_specs(t, wg, wu, wd)
    has_prev = prev_caches is not None
    args = [x, gpre, wg, wu, wd, gpost]
    out_specs, out_shape, aliases = [row], [jax.ShapeDtypeStruct((t, D_MODEL), F32)], {}
    for j, (qbd, cache, new, newt, tap) in enumerate(jobs):
        _, feat, win = cache.shape
        req = lambda *tail: pl.BlockSpec((1,) + tail, lambda i: (req0 + i,) + (0,) * len(tail))
        loc = lambda *tail: pl.BlockSpec((1,) + tail, lambda i: (i,) + (0,) * len(tail))
        in_specs += [req(N_HEADS, D_KV), req(feat, win), req(1, 2 * D_KV),
                     _resident(newt.shape), _resident(tap.shape)]
        args += [qbd, cache, new, newt, tap]
        if has_prev:
            aliases[len(args)] = 3 + 3 * j
            in_specs.append(pl.BlockSpec(memory_space=pl.ANY))
            args.append(prev_caches[j])
        out_specs += [loc(N_HEADS, D_KV), loc(N_HEADS, LANES), req(feat, win)]
        out_shape += [jax.ShapeDtypeStruct((steps, N_HEADS, D_KV), F32),
                      jax.ShapeDtypeStruct((steps, N_HEADS, LANES), F32),
                      jax.ShapeDtypeStruct(cache.shape, F32)]
    res = pl.pallas_call(
        functools.partial(_ffn_decode_kernel, n_jobs=len(jobs), req0=req0, has_prev=has_prev),
        grid=(steps,),
        in_specs=in_specs, out_specs=out_specs, out_shape=out_shape,
        input_output_aliases=aliases,
        compiler_params=_params(("arbitrary",)),
        name="ffn_decode",
    )(*args)
    return res[0], list(res[1::3]), list(res[2::3]), list(res[3::3])


def _project(x_ref, gq_ref, gk_ref, wq_ref, wkv_ref, rc_ref, rs1_ref, rs2_ref):
    xh = _rms_scale(x_ref[...])
    hq = (xh * gq_ref[...]).astype(BF16)
    hk = (xh * gk_ref[...]).astype(BF16)
    rc, rs1, rs2 = rc_ref[...], rs1_ref[...], rs2_ref[...]

    def rope(t):
        return t * rc + pltpu.roll(t, LANES - ROT_DIM // 2, 1) * rs1 + pltpu.roll(t, ROT_DIM // 2, 1) * rs2

    q = jnp.dot(hq, wq_ref[...], preferred_element_type=F32)
    qs = [(rope(q[:, c * LANES:(c + 1) * LANES]) * Q_SCALE).astype(BF16)
          for c in range(D_MODEL // LANES)]
    kv = jnp.dot(hk, wkv_ref[...], preferred_element_type=F32)
    kvs = []
    for c in range(2 * D_KV // LANES):
        t = kv[:, c * LANES:(c + 1) * LANES]
        kvs.append(rope(t) if c < D_KV // LANES else t)
    return qs, kvs


def _dup_heads(kvs):
    rows = kvs[0].shape[0]
    lo = jax.lax.broadcasted_iota(jnp.int32, (rows, LANES), 1) < HEAD_DIM
    out = []
    for t in kvs:
        sw = pltpu.roll(t, HEAD_DIM, 1)
        out.append(jnp.where(lo, t, sw).astype(BF16))
        out.append(jnp.where(lo, sw, t).astype(BF16))
    return out


def _proj_kernel(*refs, dil, cache_rows):
    ins = refs[:8]
    perm_ref = refs[8] if dil > 1 else None
    q_ref, kv_ref, k2_ref, v2_ref = refs[-4:]
    tm = ins[0].shape[0]
    qs, kvs = _project(*ins)
    for c, t in enumerate(kvs):
        kv_ref[:, c * LANES:(c + 1) * LANES] = t[tm - cache_rows:, :]
    dup = _dup_heads(kvs)
    cols = jnp.concatenate(qs + dup, axis=1)
    if dil > 1:
        cols = jnp.dot(perm_ref[...], cols, preferred_element_type=F32).astype(BF16)
    n = tm // dil
    for r in range(dil):
        blk = cols[r * n:(r + 1) * n, :]
        q_ref[r] = blk[:, :D_MODEL]
        k2_ref[r] = blk[:, D_MODEL:D_MODEL + 2 * D_KV]
        v2_ref[r] = blk[:, D_MODEL + 2 * D_KV:]


def _proj(x, gq, gk, wq, wkv, tabs, perm, *, dil, win):
    nb, seq, _ = x.shape
    tm = ROW_TILE
    nt = seq // tm
    cache_rows = min(win, tm)
    first_cache_tile = nt - win // cache_rows
    n = tm // dil
    tab = pl.BlockSpec((tm, LANES), lambda b, i: (i, 0))
    folded = lambda w: pl.BlockSpec((None, dil, n, w), lambda b, i: (b, 0, i, 0))
    in_specs = [pl.BlockSpec((None, tm, D_MODEL), lambda b, i: (b, i, 0)),
                _resident((1, D_MODEL)), _resident((1, D_MODEL)),
                _resident(wq.shape), _resident(wkv.shape), tab, tab, tab]
    args = [x, gq, gk, wq, wkv, *tabs]
    if dil > 1:
        in_specs.append(_resident(perm.shape))
        args.append(perm)
    return pl.pallas_call(
        functools.partial(_proj_kernel, dil=dil, cache_rows=cache_rows),
        grid=(nb, nt),
        in_specs=in_specs,
        out_specs=[folded(D_MODEL),
                   pl.BlockSpec((None, cache_rows, 2 * D_KV),
                                lambda b, i: (b, jnp.maximum(i - first_cache_tile, 0), 0)),
                   folded(2 * D_KV), folded(2 * D_KV)],
        out_shape=[jax.ShapeDtypeStruct((nb, dil, seq // dil, D_MODEL), BF16),
                   jax.ShapeDtypeStruct((nb, win, 2 * D_KV), F32),
                   jax.ShapeDtypeStruct((nb, dil, seq // dil, 2 * D_KV), BF16),
                   jax.ShapeDtypeStruct((nb, dil, seq // dil, 2 * D_KV), BF16)],
        compiler_params=_params(("parallel", "arbitrary")),
        name="proj",
    )(*args)


def _proj_decode_kernel(*refs):
    q_ref, new_ref, newt_ref = refs[-3:]
    qs, kvs = _project(*refs[:8])
    for c, t in enumerate(qs):
        q_ref[:, c * LANES:(c + 1) * LANES] = t
    for c, t in enumerate(kvs):
        new_ref[:, c * LANES:(c + 1) * LANES] = t
        newt_ref[c * LANES:(c + 1) * LANES, :] = t.T


def _proj_decode(x, gq, gk, wq, wkv, tabs):
    nreq = x.shape[0]
    full = lambda w: pl.BlockSpec((nreq, w), lambda i: (0, 0))
    return pl.pallas_call(
        _proj_decode_kernel,
        grid=(1,),
        in_specs=[full(D_MODEL), _resident((1, D_MODEL)), _resident((1, D_MODEL)),
                  _resident(wq.shape), _resident(wkv.shape), full(LANES), full(LANES), full(LANES)],
        out_specs=[full(D_MODEL), full(2 * D_KV), pl.BlockSpec((2 * D_KV, nreq), lambda i: (0, 0))],
        out_shape=[jax.ShapeDtypeStruct((nreq, D_MODEL), BF16),
                   jax.ShapeDtypeStruct((nreq, 2 * D_KV), F32),
                   jax.ShapeDtypeStruct((2 * D_KV, nreq), F32)],
        compiler_params=_params(("arbitrary",)),
        name="proj_decode",
    )(x, gq, gk, wq, wkv, *tabs)


def _lse_lane(head):
    return (head % 2) * HEAD_DIM + head - head % 2


def _band_kernel(*refs, has_sink, has_lse, nq):
    refs = list(refs)
    sink_ref = refs.pop(0) if has_sink else None
    q_ref, kp_ref, kc_ref, vp_ref, vc_ref, bias_ref, o_ref = refs[:7]
    lse_ref = refs[7] if has_lse else None

    lo = jax.lax.broadcasted_iota(jnp.int32, (2 * QBLK, LANES), 1) < HEAD_DIM
    lane_q = jax.lax.broadcasted_iota(jnp.int32, (QBLK, LANES), 1)
    top = jax.lax.broadcasted_iota(jnp.int32, (2 * QBLK, 1), 0) < QBLK
    zero = jnp.zeros((), BF16)
    krow_lo = jax.lax.broadcasted_iota(jnp.int32, (4 * QBLK, LANES), 0) < 2 * QBLK
    klane_lo = jax.lax.broadcasted_iota(jnp.int32, (4 * QBLK, LANES), 1) < HEAD_DIM
    ones_sel = jnp.where(krow_lo == klane_lo, 1.0, 0.0).astype(BF16)
    first_step = pl.program_id(2) == 0

    for t in range(nq):
        rows = slice(t * QBLK, (t + 1) * QBLK)
        bias = bias_ref[1] if t else bias_ref[jnp.where(first_step, 0, 1)]
        lse_tile = jnp.zeros((QBLK, LANES), F32)
        for h in range(N_KV):
            ksl = slice(h * LANES, (h + 1) * LANES)
            kprev = kc_ref[(t - 1) * QBLK:t * QBLK, ksl] if t else kp_ref[:, ksl]
            vprev = vc_ref[(t - 1) * QBLK:t * QBLK, ksl] if t else vp_ref[:, ksl]
            k2 = jnp.concatenate([kprev, kc_ref[rows, ksl]], axis=0)
            v2 = jnp.concatenate([vprev, vc_ref[rows, ksl]], axis=0)
            kbd = jnp.concatenate([jnp.where(lo, k2, zero), jnp.where(lo, zero, k2)], axis=0)
            vbd = jnp.concatenate([jnp.where(lo, v2, zero), jnp.where(lo, zero, v2)], axis=0)
            vext = jnp.concatenate([vbd, ones_sel], axis=1)
            qq = jnp.concatenate([q_ref[rows, (2 * h) * LANES:(2 * h + 1) * LANES],
                                  q_ref[rows, (2 * h + 1) * LANES:(2 * h + 2) * LANES]], axis=0)
            s = jax.lax.dot_general(qq, kbd, (((1,), (1,)), ((), ())),
                                    preferred_element_type=F32)
            ps, ms, sks = [], [], []
            for c in range(2):
                sc = s[:, c * 2 * QBLK:(c + 1) * 2 * QBLK] + bias
                m = jnp.max(sc, axis=-1, keepdims=True)
                if has_sink:
                    sk = jnp.where(top, sink_ref[4 * h + c], sink_ref[4 * h + 2 + c]) * LOG2E
                    m = jnp.maximum(m, sk)
                    sks.append(sk)
                ps.append(jnp.exp2(sc - m).astype(BF16))
                ms.append(m)
            ov = jnp.dot(jnp.concatenate(ps, axis=1), vext, preferred_element_type=F32)
            l = ov[:, LANES:]
            if has_sink:
                l = l + jnp.where(lo, jnp.exp2(sks[0] - ms[0]), jnp.exp2(sks[1] - ms[1]))
            o = ov[:, :LANES] / l
            o_ref[rows, (2 * h) * LANES:(2 * h + 1) * LANES] = o[:QBLK].astype(o_ref.dtype)
            o_ref[rows, (2 * h + 1) * LANES:(2 * h + 2) * LANES] = o[QBLK:].astype(o_ref.dtype)
            if has_lse:
                lse = jnp.where(lo, ms[0], ms[1]) * LN2 + jnp.log(l)
                for j in range(2):
                    pair = (lane_q == _lse_lane(4 * h + 2 * j)) | (lane_q == _lse_lane(4 * h + 2 * j + 1))
                    lse_tile = jnp.where(pair, lse[j * QBLK:(j + 1) * QBLK], lse_tile)
        if has_lse:
            lse_ref[rows, :] = lse_tile


def _band_attention(q, k2, v2, bias, sink, *, has_lse):
    nb, dil, length, _ = q.shape
    nq = min(MAX_QBLKS, length // QBLK)
    grid = (nb, dil, length // (nq * QBLK))
    has_sink = sink is not None
    cur = lambda w: pl.BlockSpec((None, None, nq * QBLK, w), lambda b, r, i: (b, r, i, 0))
    prev = lambda w: pl.BlockSpec((None, None, QBLK, w),
                                  lambda b, r, i: (b, r, jnp.maximum(i * nq - 1, 0), 0))
    in_specs = [cur(D_MODEL), prev(2 * D_KV), cur(2 * D_KV), prev(2 * D_KV), cur(2 * D_KV),
                _resident(bias.shape)]
    args = [q, k2, k2, v2, v2, bias]
    if has_sink:
        in_specs = [pl.BlockSpec(memory_space=pltpu.SMEM)] + in_specs
        args = [sink] + args
    out_specs = [cur(D_MODEL)]
    out_shape = [jax.ShapeDtypeStruct((nb, dil, length, D_MODEL), BF16)]
    if has_lse:
        out_specs.append(cur(LANES))
        out_shape.append(jax.ShapeDtypeStruct((nb, dil, length, LANES), F32))
    return pl.pallas_call(
        functools.partial(_band_kernel, has_sink=has_sink, has_lse=has_lse, nq=nq),
        grid=grid, in_specs=in_specs, out_specs=out_specs, out_shape=out_shape,
        compiler_params=_params(("parallel", "parallel", "parallel")),
        name="band_attn",
    )(*args)


def _decode_attend(qbd_ref, cache_ref, b, new_ref, tap_ref, sink_ref, o_ref, lse_ref):
    qb = qbd_ref[b]
    new = new_ref[b]
    kn = new[:, 0:D_KV]
    vn = new[:, D_KV:2 * D_KV]
    s = jnp.dot(qb.astype(BF16), cache_ref[b, 0:D_KV, :].astype(BF16),
                preferred_element_type=F32) + tap_ref[...]
    sn = jnp.sum(qb * kn, axis=-1, keepdims=True)
    m = jnp.maximum(jnp.max(s, axis=-1, keepdims=True), sn)
    if sink_ref is not None:
        sk = sink_ref[...] * LOG2E
        m = jnp.maximum(m, sk)
    p = jnp.exp2(s - m)
    pn = jnp.exp2(sn - m)
    l = jnp.sum(p, axis=-1, keepdims=True) + pn
    if sink_ref is not None:
        l = l + jnp.exp2(sk - m)
    o = jax.lax.dot_general(p.astype(BF16), cache_ref[b, D_KV:2 * D_KV, :].astype(BF16),
                            (((1,), (1,)), ((), ())), preferred_element_type=F32) + pn * vn
    o_ref[b] = o / l
    lse_ref[b] = jnp.broadcast_to(m * LN2 + jnp.log(l), (N_HEADS, LANES))


def _shift_rows(cache_ref, b, newt_ref, req, cout_ref, r0, r1):
    win = cache_ref.shape[2]
    lane = jax.lax.broadcasted_iota(jnp.int32, (r1 - r0, LANES), 1)
    shifted = pltpu.roll(cache_ref[b, r0:r1, :], win - 1, 1)
    newcol = pltpu.roll(newt_ref[r0:r1, :], LANES - 1 - req, 1)
    cout_ref[b, r0:r1, :] = shifted
    cout_ref[b, r0:r1, win - LANES:win] = jnp.where(lane == LANES - 1, newcol, shifted[:, win - LANES:win])


def _decode_kernel(*refs, has_sink, rb):
    refs = list(refs)
    sink_ref = refs.pop(0) if has_sink else None
    qbd_ref, cache_ref, new_ref, newt_ref, tap_ref, o_ref, lse_ref, cout_ref = refs
    base = pl.program_id(0) * rb

    def body(b, carry):
        _decode_attend(qbd_ref, cache_ref, b, new_ref, tap_ref, sink_ref, o_ref, lse_ref)
        _shift_rows(cache_ref, b, newt_ref, base + b, cout_ref, 0, 2 * D_KV)
        return carry

    jax.lax.fori_loop(0, rb, body, 0)


def _decode_attention(qbd, cache, new, newt, tap_bias, sink_rows):
    nreq, feat, win = cache.shape
    rb = max(1, DECODE_BLOCK_BYTES // (feat * win * 4))
    has_sink = sink_rows is not None
    req = lambda *tail: pl.BlockSpec((rb,) + tail, lambda i: (i,) + (0,) * len(tail))
    in_specs = [req(N_HEADS, D_KV), req(feat, win), req(1, 2 * D_KV),
                pl.BlockSpec((feat, nreq), lambda i: (0, 0)),
                pl.BlockSpec((1, win), lambda i: (0, 0))]
    args = [qbd, cache, new, newt, tap_bias]
    if has_sink:
        in_specs = [pl.BlockSpec((N_HEADS, 1), lambda i: (0, 0))] + in_specs
        args = [sink_rows] + args
    return pl.pallas_call(
        functools.partial(_decode_kernel, has_sink=has_sink, rb=rb),
        grid=(nreq // rb,), in_specs=in_specs,
        out_specs=[req(N_HEADS, D_KV), req(N_HEADS, LANES), req(feat, win)],
        out_shape=[jax.ShapeDtypeStruct((nreq, N_HEADS, D_KV), F32),
                   jax.ShapeDtypeStruct((nreq, N_HEADS, LANES), F32),
                   jax.ShapeDtypeStruct(cache.shape, F32)],
        compiler_params=_params(("parallel",)),
        name="decode_attn",
    )(*args)


def _oproj_kernel(*refs, n_groups):
    o_refs = refs[:n_groups]
    lse_refs = refs[n_groups:2 * n_groups] if n_groups > 1 else ()
    x_ref, wo_ref, g_ref, out_ref = refs[-4:]
    if n_groups == 1:
        o = o_refs[0][...].astype(BF16)
    else:
        lses = [r[...] for r in lse_refs]
        m = functools.reduce(jnp.maximum, lses)
        es = [jnp.exp(l - m) for l in lses]
        den = functools.reduce(jnp.add, es)
        o = functools.reduce(jnp.add, [(e / den) * r[...].astype(F32)
                                       for e, r in zip(es, o_refs)]).astype(BF16)
    mix = jnp.dot(o, wo_ref[...], preferred_element_type=F32)
    out_ref[...] = x_ref[...] + _rms_scale(mix) * g_ref[...]


def _oproj(os_, lses, x, wo, g):
    t = x.shape[0]
    tm = min(ROW_TILE, t)
    row = pl.BlockSpec((tm, D_MODEL), lambda i: (i, 0))
    n_groups = len(os_)
    return pl.pallas_call(
        functools.partial(_oproj_kernel, n_groups=n_groups),
        grid=(t // tm,),
        in_specs=[row] * (n_groups + len(lses) + 1) + [_resident(wo.shape), _resident((1, D_MODEL))],
        out_specs=row,
        out_shape=jax.ShapeDtypeStruct((t, D_MODEL), F32),
        compiler_params=_params(("parallel",)),
        name="oproj",
    )(*os_, *lses, x, wo, g)


def _oproj_folded_kernel(*refs, dils):
    ng = len(dils)
    o_refs, lse_refs = refs[:ng], refs[ng:2 * ng]
    unperm_refs = refs[2 * ng:2 * ng + sum(d > 1 for d in dils)]
    expand_ref, x_ref, wo_ref, g_ref, out_ref = refs[-5:]
    os_, lses, k = [], [], 0
    for gi, dil in enumerate(dils):
        o = jnp.concatenate([o_refs[gi][r] for r in range(dil)], axis=0)
        lse = jnp.concatenate([lse_refs[gi][r] for r in range(dil)], axis=0)
        if dil > 1:
            sel = unperm_refs[k][...]
            k += 1
            o = jnp.dot(sel, o, preferred_element_type=F32)
            lse = _onehot_dot(sel, lse, 3)
        else:
            o = o.astype(F32)
        os_.append(o)
        lses.append(lse)
    m = functools.reduce(jnp.maximum, lses)
    es = [jnp.exp(l - m) for l in lses]
    den = functools.reduce(jnp.add, es)
    ws = [_onehot_dot_rhs(e / den, expand_ref[...]) for e in es]
    o = functools.reduce(jnp.add, [w * o for w, o in zip(ws, os_)]).astype(BF16)
    mix = jnp.dot(o, wo_ref[...], preferred_element_type=F32)
    out_ref[...] = x_ref[...] + _rms_scale(mix) * g_ref[...]


def _onehot_dot_rhs(x, sel):
    return functools.reduce(
        jnp.add, [jnp.dot(p, sel, preferred_element_type=F32) for p in _split_bf16(x, 2)])


def _oproj_folded(os_, lses, unperms, expand, x, wo, g):
    nb, seq, _ = x.shape
    tm = ROW_TILE
    dils = tuple(o.shape[1] for o in os_)
    folded = lambda dil, w: pl.BlockSpec((None, dil, tm // dil, w), lambda b, i: (b, 0, i, 0))
    row = pl.BlockSpec((None, tm, D_MODEL), lambda b, i: (b, i, 0))
    in_specs = ([folded(d, D_MODEL) for d in dils] + [folded(d, LANES) for d in dils]
                + [_resident(u.shape) for u in unperms]
                + [_resident(expand.shape), row, _resident(wo.shape), _resident((1, D_MODEL))])
    return pl.pallas_call(
        functools.partial(_oproj_folded_kernel, dils=dils),
        grid=(nb, seq // tm),
        in_specs=in_specs,
        out_specs=row,
        out_shape=jax.ShapeDtypeStruct((nb, seq, D_MODEL), F32),
        compiler_params=_params(("parallel", "parallel")),
        name="oproj_folded",
    )(*os_, *lses, *unperms, expand, x, wo, g)


def _rope_tabs(pos):
    half = ROT_DIM // 2
    inv = ROPE_THETA ** (-jnp.arange(0, ROT_DIM, 2, dtype=F32) / ROT_DIM)
    ang = pos[:, None] * inv[None, :]
    cos, sin = jnp.cos(ang), jnp.sin(ang)
    n = pos.shape[0]
    pad = jnp.zeros((n, HEAD_DIM - ROT_DIM), F32)
    zer = jnp.zeros((n, half), F32)
    c = jnp.concatenate([cos, cos, pad + 1.0], axis=1)
    s1 = jnp.concatenate([-sin, zer, pad], axis=1)
    s2 = jnp.concatenate([zer, sin, pad], axis=1)
    return tuple(jnp.tile(t, (1, LANES // HEAD_DIM)) for t in (c, s1, s2))


def _band_bias():
    i = jnp.arange(QBLK)[:, None]
    j = jnp.arange(QBLK)[None, :]
    prev_ok = j >= i
    cur_ok = j <= i
    later = jnp.concatenate([prev_ok, cur_ok], axis=1)
    first = jnp.concatenate([jnp.zeros_like(prev_ok), cur_ok], axis=1)
    b = jnp.where(jnp.stack([first, later]), 0.0, NEG).astype(F32)
    return jnp.tile(b, (1, 2, 1))


def _fold_perm(dil):
    n = ROW_TILE // dil
    dst = jnp.arange(ROW_TILE)
    src = (dst % n) * dil + dst // n
    return (src[:, None] == jnp.arange(ROW_TILE)[None, :]).astype(BF16)


def _head_expand():
    return (jnp.arange(LANES)[:, None] == _lse_lane(jnp.arange(D_MODEL)[None, :] // HEAD_DIM)).astype(BF16)


def _heads_ig(w):
    k = w.shape[0]
    return w.reshape(k, N_KV, N_HEADS // N_KV, HEAD_DIM).transpose(0, 2, 1, 3).reshape(k, D_MODEL)


def _block_diag_q(q):
    r = q.shape[0]
    q4 = q.astype(F32).reshape(r, 4, 1, N_KV, HEAD_DIM)
    eye = jnp.eye(N_KV, dtype=F32)[None, None, :, :, None]
    return (q4 * eye).reshape(r, N_HEADS, D_KV)


def _own_head(o):
    r = o.shape[0]
    o5 = o.reshape(r, 4, N_KV, N_KV, HEAD_DIM)
    d = jnp.einsum('rigkd,gk->rigd', o5, jnp.eye(N_KV, dtype=o.dtype))
    return d.transpose(0, 2, 1, 3).reshape(r, D_MODEL)


def _expand_lse(lse):
    r = lse.shape[0]
    l = lse[:, :, 0].reshape(r, 4, N_KV).transpose(0, 2, 1).reshape(r, N_HEADS)
    return jnp.repeat(l, HEAD_DIM, axis=1)


def _window_minor(cache):
    r, w = cache.shape[:2]
    return cache.transpose(0, 2, 3, 4, 1).reshape(r, 2 * D_KV, w)


def _window_major(cache_t):
    r, _, w = cache_t.shape
    return cache_t.reshape(r, 2, N_KV, HEAD_DIM, w).transpose(0, 4, 1, 2, 3)


def kernel(x_prompt, x_sample, cache_a_kv, cache_b_kv_w128, cache_b_kv_w512, cache_b_kv_w2048,
           norm_g, w_ffn_gu, w_ffn_dn, w_qkv_a, sink_a, w_o_a, g_kv_b, w_kv_b, w_q_b, w_o_b):
    nb, seq, _ = x_prompt.shape
    nreq = x_sample.shape[0]
    tp = nb * seq
    b_caches = (cache_b_kv_w128, cache_b_kv_w512, cache_b_kv_w2048)

    def gain(l, k):
        return norm_g[l, k].reshape(1, D_MODEL)

    def ffn_w(l, k):
        gu = w_ffn_gu[l, k].astype(BF16)
        return gu[:, :D_FF], gu[:, D_FF:], w_ffn_dn[l, k].astype(BF16)

    bias = _band_bias()
    tabs_p = _rope_tabs(jnp.arange(seq, dtype=F32))
    tabs_s = _rope_tabs(jnp.zeros((nreq,), F32) + PAST_LEN)
    perms = {dil: _fold_perm(dil) for _, dil in B_GROUPS if dil > 1}

    wq_a = w_qkv_a[0][:, :D_MODEL]
    wkv_a = w_qkv_a[0][:, D_MODEL:].astype(BF16)
    wo_a = w_o_a[0].astype(BF16)
    wo_b = w_o_b[0].astype(BF16)
    g_kv = g_kv_b.reshape(1, D_MODEL)
    sink = sink_a[0].astype(F32)
    sink_rows = sink.reshape(N_KV, 4).T.reshape(N_HEADS, 1)

    def tap_bias(win, dil):
        return jnp.where(jnp.arange(win) % dil == 0, 0.0, NEG).astype(F32).reshape(1, win)

    xs = x_sample.reshape(nreq, D_MODEL)
    xs = _ffn(xs, gain(0, 0), *ffn_w(0, 0), gain(0, 1))
    q, new, newt = _proj_decode(xs, gain(0, 2), gain(0, 2), _heads_ig(wq_a).astype(BF16), wkv_a, tabs_s)
    o, _, cache_a_new = _decode_attention(_block_diag_q(q), _window_minor(cache_a_kv[0]),
                                          new.reshape(nreq, 1, 2 * D_KV), newt,
                                          tap_bias(WIN_A, 1), sink_rows)
    xs = _oproj([_own_head(o)], [], xs, wo_a, gain(0, 3))
    xs = _ffn(xs, gain(0, 4), *ffn_w(0, 1), gain(0, 5))
    xs = _ffn(xs, gain(1, 0), *ffn_w(1, 0), gain(1, 1))
    jobs = []
    for gi, (win, dil) in enumerate(B_GROUPS):
        wq = _heads_ig(w_q_b[0][:, gi * D_MODEL:(gi + 1) * D_MODEL]).astype(BF16)
        wkv = w_kv_b[:, gi * 2 * D_KV:(gi + 1) * 2 * D_KV].astype(BF16)
        q, new, newt = _proj_decode(xs, gain(1, 2), g_kv, wq, wkv, tabs_s)
        jobs.append((_block_diag_q(q), _window_minor(b_caches[gi]), new.reshape(nreq, 1, 2 * D_KV), newt,
                     tap_bias(win, dil)))

    dec_o, dec_lse, dec_caches = [], [], None

    def ffn_with_decode(x, l, k):
        nonlocal dec_caches
        req0 = len(dec_o) * (x.shape[0] // ROW_TILE)
        x, o, lse, dec_caches = _ffn_decode(x, gain(l, 4 * k), *ffn_w(l, k), gain(l, 4 * k + 1),
                                            jobs, req0, dec_caches)
        dec_o.append(o)
        dec_lse.append(lse)
        return x

    x = x_prompt.reshape(tp, D_MODEL)
    x = ffn_with_decode(x, 0, 0)
    q, kv_a, k2, v2 = _proj(x.reshape(nb, seq, D_MODEL), gain(0, 2), gain(0, 2),
                            wq_a.astype(BF16), wkv_a, tabs_p, None, dil=1, win=WIN_A)
    o, = _band_attention(q, k2, v2, bias, sink, has_lse=False)
    x = _oproj([o.reshape(tp, D_MODEL)], [], x, wo_a, gain(0, 3))
    x = ffn_with_decode(x, 0, 1)
    x = ffn_with_decode(x, 1, 0).reshape(nb, seq, D_MODEL)
    os_, lses, kv_b = [], [], []
    for gi, (win, dil) in enumerate(B_GROUPS):
        wq = w_q_b[0][:, gi * D_MODEL:(gi + 1) * D_MODEL].astype(BF16)
        wkv = w_kv_b[:, gi * 2 * D_KV:(gi + 1) * 2 * D_KV].astype(BF16)
        q, kvc, k2, v2 = _proj(x, gain(1, 2), g_kv, wq, wkv, tabs_p, perms.get(dil), dil=dil, win=win)
        o, lse = _band_attention(q, k2, v2, bias, None, has_lse=True)
        os_.append(o)
        lses.append(lse)
        kv_b.append(kvc.reshape(nb, win, 2, N_KV, HEAD_DIM))
    unperms = [perms[d].T for _, d in B_GROUPS if d > 1]
    x = _oproj_folded(os_, lses, unperms, _head_expand(), x, wo_b, gain(1, 3))
    y_p = ffn_with_decode(x.reshape(tp, D_MODEL), 1, 1).reshape(nb, seq, D_MODEL)
    a_p = kv_a.reshape(1, nb, WIN_A, 2, N_KV, HEAD_DIM)

    assert len(dec_o) * (tp // ROW_TILE) == nreq
    os_ = [_own_head(jnp.concatenate([o[gi] for o in dec_o], axis=0)) for gi in range(len(B_GROUPS))]
    lses = [_expand_lse(jnp.concatenate([l[gi] for l in dec_lse], axis=0)) for gi in range(len(B_GROUPS))]
    b_s = [_window_major(c) for c in dec_caches]
    xs = _oproj(os_, lses, xs, wo_b, gain(1, 3))
    y_s = _ffn(xs, gain(1, 4), *ffn_w(1, 1), gain(1, 5)).reshape(nreq, 1, D_MODEL)
    a_s = _window_major(cache_a_new)[None]

    return (y_p, y_s, a_p, kv_b[0], kv_b[1], kv_b[2], a_s, b_s[0], b_s[1], b_s[2])
```

```python
import functools

import jax
import jax.numpy as jnp
import numpy as np
from jax.experimental import pallas as pl
from jax.experimental.pallas import tpu as pltpu

F32 = jnp.float32
BF16 = jnp.bfloat16

D_MODEL = 1024
HEAD_DIM = 64
ROT_DIM = HEAD_DIM // 4
ROPE_THETA = 500000.0
N_HEADS = 16
N_KV = 4
D_KV = N_KV * HEAD_DIM
D_FF = 2816
EPS = 1e-6
PAST_LEN = 8192
WIN_A = 128
B_GROUPS = ((128, 1), (512, 4), (2048, 16))

LANES = 128
FF_CHUNK = 256
ROW_TILE = 512
QBLK = 128
NEG = -1e30
LOG2E = 1.4426950408889634
LN2 = 0.6931471805599453
Q_SCALE = HEAD_DIM ** -0.5 * LOG2E
MAX_QBLKS = 4
VMEM_LIMIT = 56 * 1024 * 1024
DECODE_BLOCK_BYTES = 4 * 1024 * 1024
SHIFT_ROWS = 64


def _params(sem):
    return pltpu.CompilerParams(dimension_semantics=sem, vmem_limit_bytes=VMEM_LIMIT)


def _resident(shape):
    nd = len(shape)
    return pl.BlockSpec(shape, lambda *_: (0,) * nd, pipeline_mode=pl.Buffered(1))


def _rms_scale(x):
    return x * jax.lax.rsqrt(jnp.mean(x * x, axis=-1, keepdims=True) + EPS)


def _split_bf16(x, parts):
    out = []
    for _ in range(parts - 1):
        hi = x.astype(BF16)
        out.append(hi)
        x = x - hi.astype(F32)
    out.append(x.astype(BF16))
    return out


def _onehot_dot(sel, x, parts):
    return functools.reduce(
        jnp.add, [jnp.dot(sel, p, preferred_element_type=F32) for p in _split_bf16(x, parts)])


def _ffn_kernel(x_ref, gpre_ref, wgu_ref, wd_ref, gpost_ref, o_ref, side_work=()):
    x = x_ref[...]
    xn = (_rms_scale(x) * gpre_ref[...]).astype(BF16)
    acc = None
    n_chunks = D_FF // FF_CHUNK
    for c in range(n_chunks):
        sl = slice(c * FF_CHUNK, (c + 1) * FF_CHUNK)
        gate = jnp.dot(xn, wgu_ref[:, sl], preferred_element_type=F32)
        up = jnp.dot(xn, wgu_ref[:, D_FF + c * FF_CHUNK:D_FF + (c + 1) * FF_CHUNK],
                     preferred_element_type=F32)
        act = (gate * jax.nn.sigmoid(gate) * up).astype(BF16)
        part = jnp.dot(act, wd_ref[sl, :], preferred_element_type=F32)
        acc = part if acc is None else acc + part
        for work in side_work[c::n_chunks]:
            work()
    o_ref[...] = x + 0.5 * (_rms_scale(acc) * gpost_ref[...])


def _ffn_specs(t, wgu, wd, l, k):
    tm = min(ROW_TILE, t)
    row = pl.BlockSpec((tm, D_MODEL), lambda i: (i, 0))
    weight = lambda w: pl.BlockSpec((None, None) + w.shape[2:], lambda i: (l, k, 0, 0),
                                    pipeline_mode=pl.Buffered(1))
    in_specs = [row, _resident((1, D_MODEL)), weight(wgu), weight(wd), _resident((1, D_MODEL))]
    return t // tm, in_specs, row


def _ffn(x, gpre, wgu, wd, gpost, l, k):
    t = x.shape[0]
    steps, in_specs, row = _ffn_specs(t, wgu, wd, l, k)
    return pl.pallas_call(
        _ffn_kernel,
        grid=(steps,),
        in_specs=in_specs,
        out_specs=row,
        out_shape=jax.ShapeDtypeStruct((t, D_MODEL), F32),
        compiler_params=_params(("parallel",)),
        name="ffn",
    )(x, gpre, wgu, wd, gpost)


def _ffn_decode_kernel(*refs, n_jobs, req0, has_prev):
    per_job = 6 if has_prev else 5
    ffn_in = refs[:5]
    job_in = [refs[5 + j * per_job:5 + (j + 1) * per_job] for j in range(n_jobs)]
    outs = refs[5 + n_jobs * per_job:]
    req = req0 + pl.program_id(0)
    side_work = []
    for j in range(n_jobs):
        qbd_ref, cache_ref, new_ref, newt_ref, tap_ref = job_in[j][:5]
        o_ref, lse_ref, cout_ref = outs[1 + 3 * j:4 + 3 * j]
        side_work.append(functools.partial(
            _decode_attend, qbd_ref, cache_ref, 0, new_ref, tap_ref, None, o_ref, lse_ref))
        for r0 in range(0, 2 * D_KV, SHIFT_ROWS):
            side_work.append(functools.partial(
                _shift_rows, cache_ref, 0, newt_ref, req, cout_ref, r0, r0 + SHIFT_ROWS))
    _ffn_kernel(*ffn_in, outs[0], side_work=side_work)


def _ffn_decode(x, gpre, wgu, wd, gpost, l, k, jobs, req0, prev_caches):
    t = x.shape[0]
    steps, in_specs, row = _ffn_specs(t, wgu, wd, l, k)
    has_prev = prev_caches is not None
    args = [x, gpre, wgu, wd, gpost]
    out_specs, out_shape, aliases = [row], [jax.ShapeDtypeStruct((t, D_MODEL), F32)], {}
    for j, (qbd, cache, new, newt, tap) in enumerate(jobs):
        _, feat, win = cache.shape
        req = lambda *tail: pl.BlockSpec((1,) + tail, lambda i: (req0 + i,) + (0,) * len(tail))
        loc = lambda *tail: pl.BlockSpec((1,) + tail, lambda i: (i,) + (0,) * len(tail))
        in_specs += [req(N_HEADS, D_KV), req(feat, win), req(1, 2 * D_KV),
                     _resident(newt.shape), _resident(tap.shape)]
        args += [qbd, cache, new, newt, tap]
        if has_prev:
            aliases[len(args)] = 3 + 3 * j
            in_specs.append(pl.BlockSpec(memory_space=pl.ANY))
            args.append(prev_caches[j])
        out_specs += [loc(N_HEADS, D_KV), loc(N_HEADS, LANES), req(feat, win)]
        out_shape += [jax.ShapeDtypeStruct((steps, N_HEADS, D_KV), F32),
                      jax.ShapeDtypeStruct((steps, N_HEADS, LANES), F32),
                      jax.ShapeDtypeStruct(cache.shape, F32)]
    res = pl.pallas_call(
        functools.partial(_ffn_decode_kernel, n_jobs=len(jobs), req0=req0, has_prev=has_prev),
        grid=(steps,),
        in_specs=in_specs, out_specs=out_specs, out_shape=out_shape,
        input_output_aliases=aliases,
        compiler_params=_params(("arbitrary",)),
        name="ffn_decode",
    )(*args)
    return res[0], list(res[1::3]), list(res[2::3]), list(res[3::3])


def _project(x_ref, gq_ref, gk_ref, wq_ref, wkv_ref, rc_ref, rs1_ref, rs2_ref):
    xh = _rms_scale(x_ref[...])
    hq = (xh * gq_ref[...]).astype(BF16)
    hk = (xh * gk_ref[...]).astype(BF16)
    rc, rs1, rs2 = rc_ref[...], rs1_ref[...], rs2_ref[...]

    def rope(t):
        return t * rc + pltpu.roll(t, LANES - ROT_DIM // 2, 1) * rs1 + pltpu.roll(t, ROT_DIM // 2, 1) * rs2

    q = jnp.dot(hq, wq_ref[...], preferred_element_type=F32)
    qs = [(rope(q[:, c * LANES:(c + 1) * LANES]) * Q_SCALE).astype(BF16)
          for c in range(D_MODEL // LANES)]
    kv = jnp.dot(hk, wkv_ref[...], preferred_element_type=F32)
    kvs = []
    for c in range(2 * D_KV // LANES):
        t = kv[:, c * LANES:(c + 1) * LANES]
        kvs.append(rope(t) if c < D_KV // LANES else t)
    return qs, kvs


def _dup_heads(t):
    lo = jax.lax.broadcasted_iota(jnp.int32, t.shape, 1) < HEAD_DIM
    sw = pltpu.roll(t, HEAD_DIM, 1)
    return [jnp.where(lo, t, sw).astype(BF16), jnp.where(lo, sw, t).astype(BF16)]


def _proj_kernel(*refs, dil, cache_rows):
    ins = refs[:8]
    perm_ref = refs[8] if dil > 1 else None
    q_ref, kv_ref, k2_ref, v2_ref = refs[-4:]
    tm = ins[0].shape[0]
    qs, kvs = _project(*ins)
    for c, t in enumerate(kvs):
        kv_ref[:, c * LANES:(c + 1) * LANES] = t[tm - cache_rows:, :]
    nk = D_KV // LANES
    if dil == 1:
        q_ref[0] = jnp.concatenate(qs, axis=1)
        k2_ref[0] = jnp.concatenate([d for t in kvs[:nk] for d in _dup_heads(t)], axis=1)
        v2_ref[0] = jnp.concatenate([d for t in kvs[nk:] for d in _dup_heads(t)], axis=1)
        return
    cols = jnp.concatenate(qs + [t.astype(BF16) for t in kvs], axis=1)
    half = tm // 2
    n = half // dil
    for hh in range(2):
        f = jnp.dot(perm_ref[...], cols[hh * half:(hh + 1) * half, :], preferred_element_type=F32)
        qf = f[:, :D_MODEL].astype(BF16)
        kf = jnp.concatenate([d for c in range(nk)
                              for d in _dup_heads(f[:, D_MODEL + c * LANES:D_MODEL + (c + 1) * LANES])], axis=1)
        vf = jnp.concatenate([d for c in range(nk, 2 * nk)
                              for d in _dup_heads(f[:, D_MODEL + c * LANES:D_MODEL + (c + 1) * LANES])], axis=1)
        for r in range(dil):
            dst = slice(hh * n, (hh + 1) * n)
            q_ref[r, dst, :] = qf[r * n:(r + 1) * n, :]
            k2_ref[r, dst, :] = kf[r * n:(r + 1) * n, :]
            v2_ref[r, dst, :] = vf[r * n:(r + 1) * n, :]


def _proj(x, gq, gk, wq, wq_spec, wkv, wkv_spec, tabs, perm, *, dil, win):
    nb, seq, _ = x.shape
    tm = ROW_TILE
    nt = seq // tm
    cache_rows = min(win, tm)
    first_cache_tile = nt - win // cache_rows
    n = tm // dil
    tab = pl.BlockSpec((tm, LANES), lambda b, i: (i, 0))
    folded = lambda w: pl.BlockSpec((None, dil, n, w), lambda b, i: (b, 0, i, 0))
    in_specs = [pl.BlockSpec((None, tm, D_MODEL), lambda b, i: (b, i, 0)),
                _resident((1, D_MODEL)), _resident((1, D_MODEL)),
                wq_spec, wkv_spec, tab, tab, tab]
    args = [x, gq, gk, wq, wkv, *tabs]
    if dil > 1:
        in_specs.append(_resident(perm.shape))
        args.append(perm)
    return pl.pallas_call(
        functools.partial(_proj_kernel, dil=dil, cache_rows=cache_rows),
        grid=(nb, nt),
        in_specs=in_specs,
        out_specs=[folded(D_MODEL),
                   pl.BlockSpec((None, cache_rows, 2 * D_KV),
                                lambda b, i: (b, jnp.maximum(i - first_cache_tile, 0), 0)),
                   folded(2 * D_KV), folded(2 * D_KV)],
        out_shape=[jax.ShapeDtypeStruct((nb, dil, seq // dil, D_MODEL), BF16),
                   jax.ShapeDtypeStruct((nb, win, 2 * D_KV), F32),
                   jax.ShapeDtypeStruct((nb, dil, seq // dil, 2 * D_KV), BF16),
                   jax.ShapeDtypeStruct((nb, dil, seq // dil, 2 * D_KV), BF16)],
        compiler_params=_params(("parallel", "arbitrary")),
        name="proj",
    )(*args)


def _proj_decode_kernel(*refs):
    q_ref, new_ref, newt_ref = refs[-3:]
    qs, kvs = _project(*refs[:8])
    for c, t in enumerate(qs):
        q_ref[:, c * LANES:(c + 1) * LANES] = t
    for c, t in enumerate(kvs):
        new_ref[:, c * LANES:(c + 1) * LANES] = t
        newt_ref[c * LANES:(c + 1) * LANES, :] = t.T


def _proj_decode(x, gq, gk, wq, wkv, tabs):
    nreq = x.shape[0]
    full = lambda w: pl.BlockSpec((nreq, w), lambda i: (0, 0))
    return pl.pallas_call(
        _proj_decode_kernel,
        grid=(1,),
        in_specs=[full(D_MODEL), _resident((1, D_MODEL)), _resident((1, D_MODEL)),
                  _resident(wq.shape), _resident(wkv.shape), full(LANES), full(LANES), full(LANES)],
        out_specs=[full(D_MODEL), full(2 * D_KV), pl.BlockSpec((2 * D_KV, nreq), lambda i: (0, 0))],
        out_shape=[jax.ShapeDtypeStruct((nreq, D_MODEL), BF16),
                   jax.ShapeDtypeStruct((nreq, 2 * D_KV), F32),
                   jax.ShapeDtypeStruct((2 * D_KV, nreq), F32)],
        compiler_params=_params(("arbitrary",)),
        name="proj_decode",
    )(x, gq, gk, wq, wkv, *tabs)


def _lse_lane(head):
    return (head % 2) * HEAD_DIM + head - head % 2


def _band_kernel(*refs, has_sink, has_lse, nq):
    refs = list(refs)
    sink_ref = refs.pop(0) if has_sink else None
    q_ref, kp_ref, kc_ref, vp_ref, vc_ref, bias_ref, o_ref = refs[:7]
    lse_ref = refs[7] if has_lse else None

    lo = jax.lax.broadcasted_iota(jnp.int32, (2 * QBLK, LANES), 1) < HEAD_DIM
    lane_q = jax.lax.broadcasted_iota(jnp.int32, (QBLK, LANES), 1)
    top = jax.lax.broadcasted_iota(jnp.int32, (2 * QBLK, 1), 0) < QBLK
    zero = jnp.zeros((), BF16)
    krow_lo = jax.lax.broadcasted_iota(jnp.int32, (4 * QBLK, LANES), 0) < 2 * QBLK
    klane_lo = jax.lax.broadcasted_iota(jnp.int32, (4 * QBLK, LANES), 1) < HEAD_DIM
    ones_sel = jnp.where(krow_lo == klane_lo, 1.0, 0.0).astype(BF16)
    first_step = pl.program_id(2) == 0

    for t in range(nq):
        rows = slice(t * QBLK, (t + 1) * QBLK)
        bias = bias_ref[1] if t else bias_ref[jnp.where(first_step, 0, 1)]
        lse_tile = jnp.zeros((QBLK, LANES), F32)
        for h in range(N_KV):
            ksl = slice(h * LANES, (h + 1) * LANES)
            kprev = kc_ref[(t - 1) * QBLK:t * QBLK, ksl] if t else kp_ref[:, ksl]
            vprev = vc_ref[(t - 1) * QBLK:t * QBLK, ksl] if t else vp_ref[:, ksl]
            k2 = jnp.concatenate([kprev, kc_ref[rows, ksl]], axis=0)
            v2 = jnp.concatenate([vprev, vc_ref[rows, ksl]], axis=0)
            kbd = jnp.concatenate([jnp.where(lo, k2, zero), jnp.where(lo, zero, k2)], axis=0)
            vbd = jnp.concatenate([jnp.where(lo, v2, zero), jnp.where(lo, zero, v2)], axis=0)
            vext = jnp.concatenate([vbd, ones_sel], axis=1)
            qq = jnp.concatenate([q_ref[rows, (2 * h) * LANES:(2 * h + 1) * LANES],
                                  q_ref[rows, (2 * h + 1) * LANES:(2 * h + 2) * LANES]], axis=0)
            s = jax.lax.dot_general(qq, kbd, (((1,), (1,)), ((), ())),
                                    preferred_element_type=F32)
            ps, ms, sks = [], [], []
            for c in range(2):
                sc = s[:, c * 2 * QBLK:(c + 1) * 2 * QBLK] + bias
                m = jnp.max(sc, axis=-1, keepdims=True)
                if has_sink:
                    sk = jnp.where(top, sink_ref[4 * h + c], sink_ref[4 * h + 2 + c]) * LOG2E
                    m = jnp.maximum(m, sk)
                    sks.append(sk)
                ps.append(jnp.exp2(sc - m).astype(BF16))
                ms.append(m)
            ov = jnp.dot(jnp.concatenate(ps, axis=1), vext, preferred_element_type=F32)
            l = ov[:, LANES:]
            if has_sink:
                l = l + jnp.where(lo, jnp.exp2(sks[0] - ms[0]), jnp.exp2(sks[1] - ms[1]))
            o = ov[:, :LANES] / l
            o_ref[rows, (2 * h) * LANES:(2 * h + 1) * LANES] = o[:QBLK].astype(o_ref.dtype)
            o_ref[rows, (2 * h + 1) * LANES:(2 * h + 2) * LANES] = o[QBLK:].astype(o_ref.dtype)
            if has_lse:
                lse = jnp.where(lo, ms[0], ms[1]) * LN2 + jnp.log(l)
                for j in range(2):
                    pair = (lane_q == _lse_lane(4 * h + 2 * j)) | (lane_q == _lse_lane(4 * h + 2 * j + 1))
                    lse_tile = jnp.where(pair, lse[j * QBLK:(j + 1) * QBLK], lse_tile)
        if has_lse:
            lse_ref[rows, :] = lse_tile


def _band_attention(q, k2, v2, bias, sink, *, has_lse):
    nb, dil, length, _ = q.shape
    nq = min(MAX_QBLKS, length // QBLK)
    grid = (nb, dil, length // (nq * QBLK))
    has_sink = sink is not None
    cur = lambda w: pl.BlockSpec((None, None, nq * QBLK, w), lambda b, r, i: (b, r, i, 0))
    prev = lambda w: pl.BlockSpec((None, None, QBLK, w),
                                  lambda b, r, i: (b, r, jnp.maximum(i * nq - 1, 0), 0))
    in_specs = [cur(D_MODEL), prev(2 * D_KV), cur(2 * D_KV), prev(2 * D_KV), cur(2 * D_KV),
                _resident(bias.shape)]
    args = [q, k2, k2, v2, v2, bias]
    if has_sink:
        in_specs = [pl.BlockSpec(memory_space=pltpu.SMEM)] + in_specs
        args = [sink] + args
    out_specs = [cur(D_MODEL)]
    out_shape = [jax.ShapeDtypeStruct((nb, dil, length, D_MODEL), BF16)]
    if has_lse:
        out_specs.append(cur(LANES))
        out_shape.append(jax.ShapeDtypeStruct((nb, dil, length, LANES), F32))
    return pl.pallas_call(
        functools.partial(_band_kernel, has_sink=has_sink, has_lse=has_lse, nq=nq),
        grid=grid, in_specs=in_specs, out_specs=out_specs, out_shape=out_shape,
        compiler_params=_params(("parallel", "parallel", "parallel")),
        name="band_attn",
    )(*args)


def _decode_attend(qbd_ref, cache_ref, b, new_ref, tap_ref, sink_ref, o_ref, lse_ref):
    qb = qbd_ref[b]
    new = new_ref[b]
    kn = new[:, 0:D_KV]
    vn = new[:, D_KV:2 * D_KV]
    s = jnp.dot(qb.astype(BF16), cache_ref[b, 0:D_KV, :].astype(BF16),
                preferred_element_type=F32) + tap_ref[...]
    sn = jnp.sum(qb * kn, axis=-1, keepdims=True)
    m = jnp.maximum(jnp.max(s, axis=-1, keepdims=True), sn)
    if sink_ref is not None:
        sk = sink_ref[...] * LOG2E
        m = jnp.maximum(m, sk)
    p = jnp.exp2(s - m)
    pn = jnp.exp2(sn - m)
    l = jnp.sum(p, axis=-1, keepdims=True) + pn
    if sink_ref is not None:
        l = l + jnp.exp2(sk - m)
    o = jax.lax.dot_general(p.astype(BF16), cache_ref[b, D_KV:2 * D_KV, :].astype(BF16),
                            (((1,), (1,)), ((), ())), preferred_element_type=F32) + pn * vn
    o_ref[b] = o / l
    lse_ref[b] = jnp.broadcast_to(m * LN2 + jnp.log(l), (N_HEADS, LANES))


def _shift_rows(cache_ref, b, newt_ref, req, cout_ref, r0, r1):
    win = cache_ref.shape[2]
    lane = jax.lax.broadcasted_iota(jnp.int32, (r1 - r0, LANES), 1)
    shifted = pltpu.roll(cache_ref[b, r0:r1, :], win - 1, 1)
    newcol = pltpu.roll(newt_ref[r0:r1, :], LANES - 1 - req, 1)
    cout_ref[b, r0:r1, :] = shifted
    cout_ref[b, r0:r1, win - LANES:win] = jnp.where(lane == LANES - 1, newcol, shifted[:, win - LANES:win])


def _decode_kernel(*refs, has_sink, rb):
    refs = list(refs)
    sink_ref = refs.pop(0) if has_sink else None
    qbd_ref, cache_ref, new_ref, newt_ref, tap_ref, o_ref, lse_ref, cout_ref = refs
    base = pl.program_id(0) * rb

    def body(b, carry):
        _decode_attend(qbd_ref, cache_ref, b, new_ref, tap_ref, sink_ref, o_ref, lse_ref)
        _shift_rows(cache_ref, b, newt_ref, base + b, cout_ref, 0, 2 * D_KV)
        return carry

    jax.lax.fori_loop(0, rb, body, 0)


def _decode_attention(qbd, cache, new, newt, tap_bias, sink_rows):
    nreq, feat, win = cache.shape
    rb = max(1, DECODE_BLOCK_BYTES // (feat * win * 4))
    has_sink = sink_rows is not None
    req = lambda *tail: pl.BlockSpec((rb,) + tail, lambda i: (i,) + (0,) * len(tail))
    in_specs = [req(N_HEADS, D_KV), req(feat, win), req(1, 2 * D_KV),
                pl.BlockSpec((feat, nreq), lambda i: (0, 0)),
                pl.BlockSpec((1, win), lambda i: (0, 0))]
    args = [qbd, cache, new, newt, tap_bias]
    if has_sink:
        in_specs = [pl.BlockSpec((N_HEADS, 1), lambda i: (0, 0))] + in_specs
        args = [sink_rows] + args
    return pl.pallas_call(
        functools.partial(_decode_kernel, has_sink=has_sink, rb=rb),
        grid=(nreq // rb,), in_specs=in_specs,
        out_specs=[req(N_HEADS, D_KV), req(N_HEADS, LANES), req(feat, win)],
        out_shape=[jax.ShapeDtypeStruct((nreq, N_HEADS, D_KV), F32),
                   jax.ShapeDtypeStruct((nreq, N_HEADS, LANES), F32),
                   jax.ShapeDtypeStruct(cache.shape, F32)],
        compiler_params=_params(("parallel",)),
        name="decode_attn",
    )(*args)


def _oproj_kernel(*refs, n_groups):
    o_refs = refs[:n_groups]
    lse_refs = refs[n_groups:2 * n_groups] if n_groups > 1 else ()
    x_ref, wo_ref, g_ref, out_ref = refs[-4:]
    if n_groups == 1:
        o = o_refs[0][...].astype(BF16)
    else:
        lses = [r[...] for r in lse_refs]
        m = functools.reduce(jnp.maximum, lses)
        es = [jnp.exp(l - m) for l in lses]
        den = functools.reduce(jnp.add, es)
        o = functools.reduce(jnp.add, [(e / den) * r[...].astype(F32)
                                       for e, r in zip(es, o_refs)]).astype(BF16)
    mix = jnp.dot(o, wo_ref[...], preferred_element_type=F32)
    out_ref[...] = x_ref[...] + _rms_scale(mix) * g_ref[...]


def _oproj(os_, lses, x, wo, g):
    t = x.shape[0]
    tm = min(ROW_TILE, t)
    row = pl.BlockSpec((tm, D_MODEL), lambda i: (i, 0))
    n_groups = len(os_)
    return pl.pallas_call(
        functools.partial(_oproj_kernel, n_groups=n_groups),
        grid=(t // tm,),
        in_specs=[row] * (n_groups + len(lses) + 1) + [_resident(wo.shape), _resident((1, D_MODEL))],
        out_specs=row,
        out_shape=jax.ShapeDtypeStruct((t, D_MODEL), F32),
        compiler_params=_params(("parallel",)),
        name="oproj",
    )(*os_, *lses, x, wo, g)


def _oproj_folded_kernel(*refs, dils):
    ng = len(dils)
    o_refs, lse_refs = refs[:ng], refs[ng:2 * ng]
    unperm_refs = refs[2 * ng:2 * ng + sum(d > 1 for d in dils)]
    expand_ref, x_ref, wo_ref, g_ref, out_ref = refs[-5:]
    tm = x_ref.shape[0]
    os_, lses, k = [], [], 0
    for gi, dil in enumerate(dils):
        if dil == 1:
            os_.append(o_refs[gi][0].astype(F32))
            lses.append(lse_refs[gi][0])
            continue
        sel = unperm_refs[k][...]
        k += 1
        n = tm // 2 // dil
        o_halves, lse_halves = [], []
        for hh in range(2):
            rows = slice(hh * n, (hh + 1) * n)
            o = jnp.concatenate([o_refs[gi][r, rows, :] for r in range(dil)], axis=0)
            lse = jnp.concatenate([lse_refs[gi][r, rows, :] for r in range(dil)], axis=0)
            o_halves.append(jnp.dot(sel, o, preferred_element_type=F32))
            lse_halves.append(_onehot_dot(sel, lse, 3))
        os_.append(jnp.concatenate(o_halves, axis=0))
        lses.append(jnp.concatenate(lse_halves, axis=0))
    m = functools.reduce(jnp.maximum, lses)
    es = [jnp.exp(l - m) for l in lses]
    den = functools.reduce(jnp.add, es)
    ws = [jnp.dot((e / den).astype(BF16), expand_ref[...], preferred_element_type=F32) for e in es]
    o = functools.reduce(jnp.add, [w * o for w, o in zip(ws, os_)]).astype(BF16)
    mix = jnp.dot(o, wo_ref[...], preferred_element_type=F32)
    out_ref[...] = x_ref[...] + _rms_scale(mix) * g_ref[...]


def _oproj_folded(os_, lses, unperms, expand, x, wo, g):
    nb, seq, _ = x.shape
    tm = ROW_TILE
    dils = tuple(o.shape[1] for o in os_)
    folded = lambda dil, w: pl.BlockSpec((None, dil, tm // dil, w), lambda b, i: (b, 0, i, 0))
    row = pl.BlockSpec((None, tm, D_MODEL), lambda b, i: (b, i, 0))
    in_specs = ([folded(d, D_MODEL) for d in dils] + [folded(d, LANES) for d in dils]
                + [_resident(u.shape) for u in unperms]
                + [_resident(expand.shape), row, _resident(wo.shape), _resident((1, D_MODEL))])
    return pl.pallas_call(
        functools.partial(_oproj_folded_kernel, dils=dils),
        grid=(nb, seq // tm),
        in_specs=in_specs,
        out_specs=row,
        out_shape=jax.ShapeDtypeStruct((nb, seq, D_MODEL), F32),
        compiler_params=_params(("parallel", "parallel")),
        name="oproj_folded",
    )(*os_, *lses, *unperms, expand, x, wo, g)


def _rope_tabs(pos):
    half = ROT_DIM // 2
    pos = np.asarray(pos, np.float32)
    inv = np.float32(ROPE_THETA) ** (-np.arange(0, ROT_DIM, 2, dtype=np.float32) / np.float32(ROT_DIM))
    ang = pos[:, None] * inv[None, :]
    cos, sin = np.cos(ang), np.sin(ang)
    n = pos.shape[0]
    pad = np.zeros((n, HEAD_DIM - ROT_DIM), np.float32)
    zer = np.zeros((n, half), np.float32)
    c = np.concatenate([cos, cos, pad + 1.0], axis=1)
    s1 = np.concatenate([-sin, zer, pad], axis=1)
    s2 = np.concatenate([zer, sin, pad], axis=1)
    return tuple(jnp.asarray(np.tile(t, (1, LANES // HEAD_DIM)), F32) for t in (c, s1, s2))


def _band_bias():
    i = np.arange(QBLK)[:, None]
    j = np.arange(QBLK)[None, :]
    prev_ok = j >= i
    cur_ok = j <= i
    later = np.concatenate([prev_ok, cur_ok], axis=1)
    first = np.concatenate([np.zeros_like(prev_ok), cur_ok], axis=1)
    b = np.where(np.stack([first, later]), 0.0, NEG).astype(np.float32)
    return jnp.asarray(np.tile(b, (1, 2, 1)))


def _fold_perm(dil):
    half = ROW_TILE // 2
    n = half // dil
    dst = np.arange(half)
    src = (dst % n) * dil + dst // n
    return (src[:, None] == np.arange(half)[None, :]).astype(np.float32)


def _head_expand():
    sel = np.arange(LANES)[:, None] == _lse_lane(np.arange(D_MODEL)[None, :] // HEAD_DIM)
    return jnp.asarray(sel.astype(np.float32), BF16)


def _heads_ig(w):
    k = w.shape[0]
    return w.reshape(k, N_KV, N_HEADS // N_KV, HEAD_DIM).transpose(0, 2, 1, 3).reshape(k, D_MODEL)


def _block_diag_q(q):
    r = q.shape[0]
    q4 = q.astype(F32).reshape(r, 4, 1, N_KV, HEAD_DIM)
    eye = jnp.eye(N_KV, dtype=F32)[None, None, :, :, None]
    return (q4 * eye).reshape(r, N_HEADS, D_KV)


def _own_head(o):
    r = o.shape[0]
    o5 = o.reshape(r, 4, N_KV, N_KV, HEAD_DIM)
    d = jnp.einsum('rigkd,gk->rigd', o5, jnp.eye(N_KV, dtype=o.dtype))
    return d.transpose(0, 2, 1, 3).reshape(r, D_MODEL)


def _expand_lse(lse):
    r = lse.shape[0]
    l = lse[:, :, 0].reshape(r, 4, N_KV).transpose(0, 2, 1).reshape(r, N_HEADS)
    return jnp.repeat(l, HEAD_DIM, axis=1)


def _window_minor(cache):
    r, w = cache.shape[:2]
    return cache.transpose(0, 2, 3, 4, 1).reshape(r, 2 * D_KV, w)


def _window_major(cache_t):
    r, _, w = cache_t.shape
    return cache_t.reshape(r, 2, N_KV, HEAD_DIM, w).transpose(0, 4, 1, 2, 3)


def kernel(x_prompt, x_sample, cache_a_kv, cache_b_kv_w128, cache_b_kv_w512, cache_b_kv_w2048,
           norm_g, w_ffn_gu, w_ffn_dn, w_qkv_a, sink_a, w_o_a, g_kv_b, w_kv_b, w_q_b, w_o_b):
    nb, seq, _ = x_prompt.shape
    nreq = x_sample.shape[0]
    tp = nb * seq
    b_caches = (cache_b_kv_w128, cache_b_kv_w512, cache_b_kv_w2048)

    def gain(l, k):
        return norm_g[l, k].reshape(1, D_MODEL)

    wgu = w_ffn_gu.astype(BF16)
    wdn = w_ffn_dn.astype(BF16)

    def ffn(x, l, k):
        return _ffn(x, gain(l, 4 * k), wgu, wdn, gain(l, 4 * k + 1), l, k)

    bias = _band_bias()
    tabs_p = _rope_tabs(np.arange(seq))
    tabs_s = _rope_tabs(np.full((nreq,), PAST_LEN))
    perms = {dil: _fold_perm(dil) for _, dil in B_GROUPS if dil > 1}

    wqkv_a = w_qkv_a.astype(BF16)
    wq_b = w_q_b.astype(BF16)
    wkv_b = w_kv_b.astype(BF16)
    col3 = lambda w, j: pl.BlockSpec((None, D_MODEL, w), lambda *_: (0, 0, j), pipeline_mode=pl.Buffered(1))
    col2 = lambda w, j: pl.BlockSpec((D_MODEL, w), lambda *_: (0, j), pipeline_mode=pl.Buffered(1))
    wq_a = w_qkv_a[0][:, :D_MODEL]
    wkv_a = wqkv_a[0][:, D_MODEL:]
    wo_a = w_o_a[0].astype(BF16)
    wo_b = w_o_b[0].astype(BF16)
    g_kv = g_kv_b.reshape(1, D_MODEL)
    sink = sink_a[0].astype(F32)
    sink_rows = sink.reshape(N_KV, 4).T.reshape(N_HEADS, 1)

    def tap_bias(win, dil):
        return jnp.asarray(np.where(np.arange(win) % dil == 0, 0.0, NEG).astype(np.float32).reshape(1, win))

    xs = x_sample.reshape(nreq, D_MODEL)
    xs = ffn(xs, 0, 0)
    q, new, newt = _proj_decode(xs, gain(0, 2), gain(0, 2), _heads_ig(wq_a).astype(BF16), wkv_a, tabs_s)
    o, _, cache_a_new = _decode_attention(_block_diag_q(q), _window_minor(cache_a_kv[0]),
                                          new.reshape(nreq, 1, 2 * D_KV), newt,
                                          tap_bias(WIN_A, 1), sink_rows)
    xs = _oproj([_own_head(o)], [], xs, wo_a, gain(0, 3))
    xs = ffn(xs, 0, 1)
    xs = ffn(xs, 1, 0)
    jobs = []
    for gi, (win, dil) in enumerate(B_GROUPS):
        wq = _heads_ig(w_q_b[0][:, gi * D_MODEL:(gi + 1) * D_MODEL]).astype(BF16)
        wkv = wkv_b[:, gi * 2 * D_KV:(gi + 1) * 2 * D_KV]
        q, new, newt = _proj_decode(xs, gain(1, 2), g_kv, wq, wkv, tabs_s)
        jobs.append((_block_diag_q(q), _window_minor(b_caches[gi]), new.reshape(nreq, 1, 2 * D_KV), newt,
                     tap_bias(win, dil)))

    dec_o, dec_lse, dec_caches = [], [], None

    def ffn_with_decode(x, l, k):
        nonlocal dec_caches
        req0 = len(dec_o) * (x.shape[0] // ROW_TILE)
        x, o, lse, dec_caches = _ffn_decode(x, gain(l, 4 * k), wgu, wdn, gain(l, 4 * k + 1), l, k,
                                            jobs, req0, dec_caches)
        dec_o.append(o)
        dec_lse.append(lse)
        return x

    x = x_prompt.reshape(tp, D_MODEL)
    x = ffn_with_decode(x, 0, 0)
    q, kv_a, k2, v2 = _proj(x.reshape(nb, seq, D_MODEL), gain(0, 2), gain(0, 2),
                            wqkv_a, col3(D_MODEL, 0), wqkv_a, col3(2 * D_KV, 2), tabs_p, None,
                            dil=1, win=WIN_A)
    o, = _band_attention(q, k2, v2, bias, sink, has_lse=False)
    x = _oproj([o.reshape(tp, D_MODEL)], [], x, wo_a, gain(0, 3))
    x = ffn_with_decode(x, 0, 1)
    x = ffn_with_decode(x, 1, 0).reshape(nb, seq, D_MODEL)
    os_, lses, kv_b = [], [], []
    for gi, (win, dil) in enumerate(B_GROUPS):
        perm = jnp.asarray(perms[dil], BF16) if dil > 1 else None
        q, kvc, k2, v2 = _proj(x, gain(1, 2), g_kv, wq_b, col3(D_MODEL, gi), wkv_b, col2(2 * D_KV, gi),
                               tabs_p, perm, dil=dil, win=win)
        o, lse = _band_attention(q, k2, v2, bias, None, has_lse=True)
        os_.append(o)
        lses.append(lse)
        kv_b.append(kvc.reshape(nb, win, 2, N_KV, HEAD_DIM))
    unperms = [jnp.asarray(perms[d].T, BF16) for _, d in B_GROUPS if d > 1]
    x = _oproj_folded(os_, lses, unperms, _head_expand(), x, wo_b, gain(1, 3))
    y_p = ffn_with_decode(x.reshape(tp, D_MODEL), 1, 1).reshape(nb, seq, D_MODEL)
    a_p = kv_a.reshape(1, nb, WIN_A, 2, N_KV, HEAD_DIM)

    assert len(dec_o) * (tp // ROW_TILE) == nreq
    os_ = [_own_head(jnp.concatenate([o[gi] for o in dec_o], axis=0)) for gi in range(len(B_GROUPS))]
    lses = [_expand_lse(jnp.concatenate([l[gi] for l in dec_lse], axis=0)) for gi in range(len(B_GROUPS))]
    b_s = [_window_major(c) for c in dec_caches]
    xs = _oproj(os_, lses, xs, wo_b, gain(1, 3))
    y_s = ffn(xs, 1, 1).reshape(nreq, 1, D_MODEL)
    a_s = _window_major(cache_a_new)[None]

    return (y_p, y_s, a_p, kv_b[0], kv_b[1], kv_b[2], a_s, b_s[0], b_s[1], b_s[2])
```

```python
import functools

import jax
import jax.numpy as jnp
import numpy as np
from jax.experimental import pallas as pl
from jax.experimental.pallas import tpu as pltpu

F32 = jnp.float32
BF16 = jnp.bfloat16

D_MODEL = 1024
HEAD_DIM = 64
ROT_DIM = HEAD_DIM // 4
ROPE_THETA = 500000.0
N_HEADS = 16
N_KV = 4
D_KV = N_KV * HEAD_DIM
D_FF = 2816
EPS = 1e-6
PAST_LEN = 8192
WIN_A = 128
B_GROUPS = ((128, 1), (512, 4), (2048, 16))

LANES = 128
FF_CHUNK = 256
ROW_TILE = 512
QBLK = 128
NEG = -1e30
LOG2E = 1.4426950408889634
LN2 = 0.6931471805599453
Q_SCALE = HEAD_DIM ** -0.5 * LOG2E
MAX_QBLKS = 4
VMEM_LIMIT = 56 * 1024 * 1024
DECODE_BLOCK_BYTES = 4 * 1024 * 1024
SHIFT_ROWS = 64


def _params(sem):
    return pltpu.CompilerParams(dimension_semantics=sem, vmem_limit_bytes=VMEM_LIMIT)


def _resident(shape):
    nd = len(shape)
    return pl.BlockSpec(shape, lambda *_: (0,) * nd, pipeline_mode=pl.Buffered(1))


def _rms_scale(x):
    return x * jax.lax.rsqrt(jnp.mean(x * x, axis=-1, keepdims=True) + EPS)


def _split_bf16(x, parts):
    out = []
    for _ in range(parts - 1):
        hi = x.astype(BF16)
        out.append(hi)
        x = x - hi.astype(F32)
    out.append(x.astype(BF16))
    return out


def _onehot_dot(sel, x, parts):
    return functools.reduce(
        jnp.add, [jnp.dot(sel, p, preferred_element_type=F32) for p in _split_bf16(x, parts)])


def _ffn_kernel(x_ref, gpre_ref, wgu_ref, wd_ref, gpost_ref, o_ref, side_work=()):
    x = x_ref[...]
    xn = (_rms_scale(x) * gpre_ref[...]).astype(BF16)
    acc = None
    n_chunks = D_FF // FF_CHUNK
    for c in range(n_chunks):
        sl = slice(c * FF_CHUNK, (c + 1) * FF_CHUNK)
        gate = jnp.dot(xn, wgu_ref[:, sl], preferred_element_type=F32)
        up = jnp.dot(xn, wgu_ref[:, D_FF + c * FF_CHUNK:D_FF + (c + 1) * FF_CHUNK],
                     preferred_element_type=F32)
        act = (gate * jax.nn.sigmoid(gate) * up).astype(BF16)
        part = jnp.dot(act, wd_ref[sl, :], preferred_element_type=F32)
        acc = part if acc is None else acc + part
        for work in side_work[c::n_chunks]:
            work()
    o_ref[...] = x + 0.5 * (_rms_scale(acc) * gpost_ref[...])


def _ffn_specs(t, wgu, wd, l, k):
    tm = min(ROW_TILE, t)
    row = pl.BlockSpec((tm, D_MODEL), lambda i: (i, 0))
    weight = lambda w: pl.BlockSpec((None, None) + w.shape[2:], lambda i: (l, k, 0, 0),
                                    pipeline_mode=pl.Buffered(1))
    in_specs = [row, _resident((1, D_MODEL)), weight(wgu), weight(wd), _resident((1, D_MODEL))]
    return t // tm, in_specs, row


def _ffn(x, gpre, wgu, wd, gpost, l, k):
    t = x.shape[0]
    steps, in_specs, row = _ffn_specs(t, wgu, wd, l, k)
    return pl.pallas_call(
        _ffn_kernel,
        grid=(steps,),
        in_specs=in_specs,
        out_specs=row,
        out_shape=jax.ShapeDtypeStruct((t, D_MODEL), F32),
        compiler_params=_params(("parallel",)),
        name="ffn",
    )(x, gpre, wgu, wd, gpost)


def _ffn_decode_kernel(*refs, n_jobs, req0, has_prev):
    per_job = 5 if has_prev else 4
    ffn_in = refs[:5]
    job_in = [refs[5 + j * per_job:5 + (j + 1) * per_job] for j in range(n_jobs)]
    outs = refs[5 + n_jobs * per_job:]
    req = req0 + pl.program_id(0)
    side_work = []
    for j in range(n_jobs):
        qbd_ref, cache_ref, new_ref, newt_ref = job_in[j][:4]
        o_ref, lse_ref, cout_ref = outs[1 + 3 * j:4 + 3 * j]
        side_work.append(functools.partial(
            _decode_attend, qbd_ref, cache_ref, 0, new_ref, None, o_ref, lse_ref))
        for r0 in range(0, 2 * D_KV, SHIFT_ROWS):
            side_work.append(functools.partial(
                _shift_rows, cache_ref, 0, newt_ref, req, cout_ref, r0, r0 + SHIFT_ROWS))
    _ffn_kernel(*ffn_in, outs[0], side_work=side_work)


def _ffn_decode(x, gpre, wgu, wd, gpost, l, k, jobs, req0, prev_caches):
    t = x.shape[0]
    steps, in_specs, row = _ffn_specs(t, wgu, wd, l, k)
    has_prev = prev_caches is not None
    args = [x, gpre, wgu, wd, gpost]
    out_specs, out_shape, aliases = [row], [jax.ShapeDtypeStruct((t, D_MODEL), F32)], {}
    for j, (qbd, cache, new, newt) in enumerate(jobs):
        _, feat, win = cache.shape
        req = lambda *tail: pl.BlockSpec((1,) + tail, lambda i: (req0 + i,) + (0,) * len(tail))
        loc = lambda *tail: pl.BlockSpec((1,) + tail, lambda i: (i,) + (0,) * len(tail))
        in_specs += [req(N_HEADS, D_KV), req(feat, win), req(1, 2 * D_KV), _resident(newt.shape)]
        args += [qbd, cache, new, newt]
        if has_prev:
            aliases[len(args)] = 3 + 3 * j
            in_specs.append(pl.BlockSpec(memory_space=pl.ANY))
            args.append(prev_caches[j])
        out_specs += [loc(N_HEADS, D_KV), loc(N_HEADS, LANES), req(feat, win)]
        out_shape += [jax.ShapeDtypeStruct((steps, N_HEADS, D_KV), F32),
                      jax.ShapeDtypeStruct((steps, N_HEADS, LANES), F32),
                      jax.ShapeDtypeStruct(cache.shape, F32)]
    res = pl.pallas_call(
        functools.partial(_ffn_decode_kernel, n_jobs=len(jobs), req0=req0, has_prev=has_prev),
        grid=(steps,),
        in_specs=in_specs, out_specs=out_specs, out_shape=out_shape,
        input_output_aliases=aliases,
        compiler_params=_params(("arbitrary",)),
        name="ffn_decode",
    )(*args)
    return res[0], list(res[1::3]), list(res[2::3]), list(res[3::3])


def _project(x_ref, gq_ref, gk_ref, wq_ref, wkv_ref, rc_ref, rs1_ref, rs2_ref):
    xh = _rms_scale(x_ref[...])
    hq = (xh * gq_ref[...]).astype(BF16)
    hk = (xh * gk_ref[...]).astype(BF16)
    rc, rs1, rs2 = rc_ref[...], rs1_ref[...], rs2_ref[...]

    def rope(t):
        return t * rc + pltpu.roll(t, LANES - ROT_DIM // 2, 1) * rs1 + pltpu.roll(t, ROT_DIM // 2, 1) * rs2

    q = jnp.dot(hq, wq_ref[...], preferred_element_type=F32)
    qs = [(rope(q[:, c * LANES:(c + 1) * LANES]) * Q_SCALE).astype(BF16)
          for c in range(D_MODEL // LANES)]
    kv = jnp.dot(hk, wkv_ref[...], preferred_element_type=F32)
    kvs = []
    for c in range(2 * D_KV // LANES):
        t = kv[:, c * LANES:(c + 1) * LANES]
        kvs.append(rope(t) if c < D_KV // LANES else t)
    return qs, kvs


def _dup_heads(t):
    lo = jax.lax.broadcasted_iota(jnp.int32, t.shape, 1) < HEAD_DIM
    sw = pltpu.roll(t, HEAD_DIM, 1)
    return [jnp.where(lo, t, sw).astype(BF16), jnp.where(lo, sw, t).astype(BF16)]


def _proj_kernel(*refs, dil, cache_rows):
    ins = refs[:8]
    perm_ref = refs[8] if dil > 1 else None
    q_ref, kv_ref, k2_ref, v2_ref = refs[-4:]
    tm = ins[0].shape[0]
    qs, kvs = _project(*ins)
    for c, t in enumerate(kvs):
        kv_ref[:, c * LANES:(c + 1) * LANES] = t[tm - cache_rows:, :]
    nk = D_KV // LANES
    if dil == 1:
        q_ref[0] = jnp.concatenate(qs, axis=1)
        k2_ref[0] = jnp.concatenate([d for t in kvs[:nk] for d in _dup_heads(t)], axis=1)
        v2_ref[0] = jnp.concatenate([d for t in kvs[nk:] for d in _dup_heads(t)], axis=1)
        return
    cols = jnp.concatenate(qs + [t.astype(BF16) for t in kvs], axis=1)
    half = tm // 2
    n = half // dil
    for hh in range(2):
        f = jnp.dot(perm_ref[...], cols[hh * half:(hh + 1) * half, :], preferred_element_type=F32)
        qf = f[:, :D_MODEL].astype(BF16)
        kf = jnp.concatenate([d for c in range(nk)
                              for d in _dup_heads(f[:, D_MODEL + c * LANES:D_MODEL + (c + 1) * LANES])], axis=1)
        vf = jnp.concatenate([d for c in range(nk, 2 * nk)
                              for d in _dup_heads(f[:, D_MODEL + c * LANES:D_MODEL + (c + 1) * LANES])], axis=1)
        for r in range(dil):
            dst = slice(hh * n, (hh + 1) * n)
            q_ref[r, dst, :] = qf[r * n:(r + 1) * n, :]
            k2_ref[r, dst, :] = kf[r * n:(r + 1) * n, :]
            v2_ref[r, dst, :] = vf[r * n:(r + 1) * n, :]


def _proj(x, gq, gk, wq, wq_spec, wkv, wkv_spec, tabs, perm, *, dil, win):
    nb, seq, _ = x.shape
    tm = ROW_TILE
    nt = seq // tm
    cache_rows = min(win, tm)
    first_cache_tile = nt - win // cache_rows
    n = tm // dil
    tab = pl.BlockSpec((tm, LANES), lambda b, i: (i, 0))
    folded = lambda w: pl.BlockSpec((None, dil, n, w), lambda b, i: (b, 0, i, 0))
    in_specs = [pl.BlockSpec((None, tm, D_MODEL), lambda b, i: (b, i, 0)),
                _resident((1, D_MODEL)), _resident((1, D_MODEL)),
                wq_spec, wkv_spec, tab, tab, tab]
    args = [x, gq, gk, wq, wkv, *tabs]
    if dil > 1:
        in_specs.append(_resident(perm.shape))
        args.append(perm)
    return pl.pallas_call(
        functools.partial(_proj_kernel, dil=dil, cache_rows=cache_rows),
        grid=(nb, nt),
        in_specs=in_specs,
        out_specs=[folded(D_MODEL),
                   pl.BlockSpec((None, cache_rows, 2 * D_KV),
                                lambda b, i: (b, jnp.maximum(i - first_cache_tile, 0), 0)),
                   folded(2 * D_KV), folded(2 * D_KV)],
        out_shape=[jax.ShapeDtypeStruct((nb, dil, seq // dil, D_MODEL), BF16),
                   jax.ShapeDtypeStruct((nb, win, 2 * D_KV), F32),
                   jax.ShapeDtypeStruct((nb, dil, seq // dil, 2 * D_KV), BF16),
                   jax.ShapeDtypeStruct((nb, dil, seq // dil, 2 * D_KV), BF16)],
        compiler_params=_params(("parallel", "arbitrary")),
        name="proj",
    )(*args)


def _proj_decode_kernel(*refs):
    q_ref, new_ref, newt_ref = refs[-3:]
    qs, kvs = _project(*refs[:8])
    for c, t in enumerate(qs):
        q_ref[:, c * LANES:(c + 1) * LANES] = t
    for c, t in enumerate(kvs):
        new_ref[:, c * LANES:(c + 1) * LANES] = t
        newt_ref[c * LANES:(c + 1) * LANES, :] = t.T


def _proj_decode(x, gq, gk, wq, wkv, tabs):
    nreq = x.shape[0]
    full = lambda w: pl.BlockSpec((nreq, w), lambda i: (0, 0))
    return pl.pallas_call(
        _proj_decode_kernel,
        grid=(1,),
        in_specs=[full(D_MODEL), _resident((1, D_MODEL)), _resident((1, D_MODEL)),
                  _resident(wq.shape), _resident(wkv.shape), full(LANES), full(LANES), full(LANES)],
        out_specs=[full(D_MODEL), full(2 * D_KV), pl.BlockSpec((2 * D_KV, nreq), lambda i: (0, 0))],
        out_shape=[jax.ShapeDtypeStruct((nreq, D_MODEL), BF16),
                   jax.ShapeDtypeStruct((nreq, 2 * D_KV), F32),
                   jax.ShapeDtypeStruct((2 * D_KV, nreq), F32)],
        compiler_params=_params(("arbitrary",)),
        name="proj_decode",
    )(x, gq, gk, wq, wkv, *tabs)


def _lse_lane(head):
    return (head % 2) * HEAD_DIM + head - head % 2


def _band_kernel(*refs, has_sink, has_lse, nq):
    refs = list(refs)
    sink_ref = refs.pop(0) if has_sink else None
    q_ref, kp_ref, kc_ref, vp_ref, vc_ref, bias_ref, o_ref = refs[:7]
    lse_ref = refs[7] if has_lse else None

    lo = jax.lax.broadcasted_iota(jnp.int32, (2 * QBLK, LANES), 1) < HEAD_DIM
    lane_q = jax.lax.broadcasted_iota(jnp.int32, (QBLK, LANES), 1)
    top = jax.lax.broadcasted_iota(jnp.int32, (2 * QBLK, 1), 0) < QBLK
    zero = jnp.zeros((), BF16)
    krow_lo = jax.lax.broadcasted_iota(jnp.int32, (4 * QBLK, LANES), 0) < 2 * QBLK
    klane_lo = jax.lax.broadcasted_iota(jnp.int32, (4 * QBLK, LANES), 1) < HEAD_DIM
    ones_sel = jnp.where(krow_lo == klane_lo, 1.0, 0.0).astype(BF16)
    first_step = pl.program_id(2) == 0

    for t in range(nq):
        rows = slice(t * QBLK, (t + 1) * QBLK)
        bias = bias_ref[1] if t else bias_ref[jnp.where(first_step, 0, 1)]
        lse_tile = jnp.zeros((QBLK, LANES), F32)
        for h in range(N_KV):
            ksl = slice(h * LANES, (h + 1) * LANES)
            kprev = kc_ref[(t - 1) * QBLK:t * QBLK, ksl] if t else kp_ref[:, ksl]
            vprev = vc_ref[(t - 1) * QBLK:t * QBLK, ksl] if t else vp_ref[:, ksl]
            k2 = jnp.concatenate([kprev, kc_ref[rows, ksl]], axis=0)
            v2 = jnp.concatenate([vprev, vc_ref[rows, ksl]], axis=0)
            kbd = jnp.concatenate([jnp.where(lo, k2, zero), jnp.where(lo, zero, k2)], axis=0)
            vbd = jnp.concatenate([jnp.where(lo, v2, zero), jnp.where(lo, zero, v2)], axis=0)
            vext = jnp.concatenate([vbd, ones_sel], axis=1)
            qq = jnp.concatenate([q_ref[rows, (2 * h) * LANES:(2 * h + 1) * LANES],
                                  q_ref[rows, (2 * h + 1) * LANES:(2 * h + 2) * LANES]], axis=0)
            s = jax.lax.dot_general(qq, kbd, (((1,), (1,)), ((), ())),
                                    preferred_element_type=F32)
            ps, ms, sks = [], [], []
            for c in range(2):
                sc = s[:, c * 2 * QBLK:(c + 1) * 2 * QBLK] + bias
                m = jnp.max(sc, axis=-1, keepdims=True)
                if has_sink:
                    sk = jnp.where(top, sink_ref[4 * h + c], sink_ref[4 * h + 2 + c]) * LOG2E
                    m = jnp.maximum(m, sk)
                    sks.append(sk)
                ps.append(jnp.exp2(sc - m).astype(BF16))
                ms.append(m)
            ov = jnp.dot(jnp.concatenate(ps, axis=1), vext, preferred_element_type=F32)
            l = ov[:, LANES:]
            if has_sink:
                l = l + jnp.where(lo, jnp.exp2(sks[0] - ms[0]), jnp.exp2(sks[1] - ms[1]))
            o = ov[:, :LANES] / l
            o_ref[rows, (2 * h) * LANES:(2 * h + 1) * LANES] = o[:QBLK].astype(o_ref.dtype)
            o_ref[rows, (2 * h + 1) * LANES:(2 * h + 2) * LANES] = o[QBLK:].astype(o_ref.dtype)
            if has_lse:
                lse = jnp.where(lo, ms[0], ms[1]) * LN2 + jnp.log(l)
                for j in range(2):
                    pair = (lane_q == _lse_lane(4 * h + 2 * j)) | (lane_q == _lse_lane(4 * h + 2 * j + 1))
                    lse_tile = jnp.where(pair, lse[j * QBLK:(j + 1) * QBLK], lse_tile)
        if has_lse:
            lse_ref[rows, :] = lse_tile


def _band_attention(q, k2, v2, bias, sink, *, has_lse):
    nb, dil, length, _ = q.shape
    nq = min(MAX_QBLKS, length // QBLK)
    grid = (nb, dil, length // (nq * QBLK))
    has_sink = sink is not None
    cur = lambda w: pl.BlockSpec((None, None, nq * QBLK, w), lambda b, r, i: (b, r, i, 0))
    prev = lambda w: pl.BlockSpec((None, None, QBLK, w),
                                  lambda b, r, i: (b, r, jnp.maximum(i * nq - 1, 0), 0))
    in_specs = [cur(D_MODEL), prev(2 * D_KV), cur(2 * D_KV), prev(2 * D_KV), cur(2 * D_KV),
                _resident(bias.shape)]
    args = [q, k2, k2, v2, v2, bias]
    if has_sink:
        in_specs = [pl.BlockSpec(memory_space=pltpu.SMEM)] + in_specs
        args = [sink] + args
    out_specs = [cur(D_MODEL)]
    out_shape = [jax.ShapeDtypeStruct((nb, dil, length, D_MODEL), BF16)]
    if has_lse:
        out_specs.append(cur(LANES))
        out_shape.append(jax.ShapeDtypeStruct((nb, dil, length, LANES), F32))
    return pl.pallas_call(
        functools.partial(_band_kernel, has_sink=has_sink, has_lse=has_lse, nq=nq),
        grid=grid, in_specs=in_specs, out_specs=out_specs, out_shape=out_shape,
        compiler_params=_params(("parallel", "parallel", "parallel")),
        name="band_attn",
    )(*args)


def _tap_columns(cache_ref, b, r0):
    dil = cache_ref.shape[2] // LANES
    if dil == 1:
        return cache_ref[b, r0:r0 + D_KV, :]
    lane = jax.lax.broadcasted_iota(jnp.int32, (D_KV, LANES), 1)
    is_tap = (lane & (dil - 1)) == 0
    acc = None
    for t in range(dil):
        part = jnp.where(is_tap, cache_ref[b, r0:r0 + D_KV, t * LANES:(t + 1) * LANES], 0.0)
        if t:
            part = pltpu.roll(part, t, 1)
        acc = part if acc is None else acc + part
    return acc


def _decode_attend(qbd_ref, cache_ref, b, new_ref, sink_ref, o_ref, lse_ref):
    qb = qbd_ref[b]
    new = new_ref[b]
    kn = new[:, 0:D_KV]
    vn = new[:, D_KV:2 * D_KV]
    s = jnp.dot(qb.astype(BF16), _tap_columns(cache_ref, b, 0).astype(BF16),
                preferred_element_type=F32)
    sn = jnp.sum(qb * kn, axis=-1, keepdims=True)
    m = jnp.maximum(jnp.max(s, axis=-1, keepdims=True), sn)
    if sink_ref is not None:
        sk = sink_ref[...] * LOG2E
        m = jnp.maximum(m, sk)
    p = jnp.exp2(s - m)
    pn = jnp.exp2(sn - m)
    l = jnp.sum(p, axis=-1, keepdims=True) + pn
    if sink_ref is not None:
        l = l + jnp.exp2(sk - m)
    o = jax.lax.dot_general(p.astype(BF16), _tap_columns(cache_ref, b, D_KV).astype(BF16),
                            (((1,), (1,)), ((), ())), preferred_element_type=F32) + pn * vn
    o_ref[b] = o / l
    lse_ref[b] = jnp.broadcast_to(m * LN2 + jnp.log(l), (N_HEADS, LANES))


def _shift_rows(cache_ref, b, newt_ref, req, cout_ref, r0, r1):
    win = cache_ref.shape[2]
    lane = jax.lax.broadcasted_iota(jnp.int32, (r1 - r0, LANES), 1)
    shifted = pltpu.roll(cache_ref[b, r0:r1, :], win - 1, 1)
    newcol = pltpu.roll(newt_ref[r0:r1, :], LANES - 1 - req, 1)
    cout_ref[b, r0:r1, :] = shifted
    cout_ref[b, r0:r1, win - LANES:win] = jnp.where(lane == LANES - 1, newcol, shifted[:, win - LANES:win])


def _decode_kernel(*refs, has_sink, rb):
    refs = list(refs)
    sink_ref = refs.pop(0) if has_sink else None
    qbd_ref, cache_ref, new_ref, newt_ref, o_ref, lse_ref, cout_ref = refs
    base = pl.program_id(0) * rb

    def body(b, carry):
        _decode_attend(qbd_ref, cache_ref, b, new_ref, sink_ref, o_ref, lse_ref)
        _shift_rows(cache_ref, b, newt_ref, base + b, cout_ref, 0, 2 * D_KV)
        return carry

    jax.lax.fori_loop(0, rb, body, 0)


def _decode_attention(qbd, cache, new, newt, sink_rows):
    nreq, feat, win = cache.shape
    rb = max(1, DECODE_BLOCK_BYTES // (feat * win * 4))
    has_sink = sink_rows is not None
    req = lambda *tail: pl.BlockSpec((rb,) + tail, lambda i: (i,) + (0,) * len(tail))
    in_specs = [req(N_HEADS, D_KV), req(feat, win), req(1, 2 * D_KV),
                pl.BlockSpec((feat, nreq), lambda i: (0, 0))]
    args = [qbd, cache, new, newt]
    if has_sink:
        in_specs = [pl.BlockSpec((N_HEADS, 1), lambda i: (0, 0))] + in_specs
        args = [sink_rows] + args
    return pl.pallas_call(
        functools.partial(_decode_kernel, has_sink=has_sink, rb=rb),
        grid=(nreq // rb,), in_specs=in_specs,
        out_specs=[req(N_HEADS, D_KV), req(N_HEADS, LANES), req(feat, win)],
        out_shape=[jax.ShapeDtypeStruct((nreq, N_HEADS, D_KV), F32),
                   jax.ShapeDtypeStruct((nreq, N_HEADS, LANES), F32),
                   jax.ShapeDtypeStruct(cache.shape, F32)],
        compiler_params=_params(("parallel",)),
        name="decode_attn",
    )(*args)


def _oproj_kernel(*refs, n_groups):
    o_refs = refs[:n_groups]
    lse_refs = refs[n_groups:2 * n_groups] if n_groups > 1 else ()
    x_ref, wo_ref, g_ref, out_ref = refs[-4:]
    if n_groups == 1:
        o = o_refs[0][...].astype(BF16)
    else:
        lses = [r[...] for r in lse_refs]
        m = functools.reduce(jnp.maximum, lses)
        es = [jnp.exp(l - m) for l in lses]
        den = functools.reduce(jnp.add, es)
        o = functools.reduce(jnp.add, [(e / den) * r[...].astype(F32)
                                       for e, r in zip(es, o_refs)]).astype(BF16)
    mix = jnp.dot(o, wo_ref[...], preferred_element_type=F32)
    out_ref[...] = x_ref[...] + _rms_scale(mix) * g_ref[...]


def _oproj(os_, lses, x, wo, g):
    t = x.shape[0]
    tm = min(ROW_TILE, t)
    row = pl.BlockSpec((tm, D_MODEL), lambda i: (i, 0))
    n_groups = len(os_)
    return pl.pallas_call(
        functools.partial(_oproj_kernel, n_groups=n_groups),
        grid=(t // tm,),
        in_specs=[row] * (n_groups + len(lses) + 1) + [_resident(wo.shape), _resident((1, D_MODEL))],
        out_specs=row,
        out_shape=jax.ShapeDtypeStruct((t, D_MODEL), F32),
        compiler_params=_params(("parallel",)),
        name="oproj",
    )(*os_, *lses, x, wo, g)


def _oproj_folded_kernel(*refs, dils):
    ng = len(dils)
    o_refs, lse_refs = refs[:ng], refs[ng:2 * ng]
    unperm_refs = refs[2 * ng:2 * ng + sum(d > 1 for d in dils)]
    expand_ref, x_ref, wo_ref, g_ref, out_ref = refs[-5:]
    tm = x_ref.shape[0]
    os_, lses, k = [], [], 0
    for gi, dil in enumerate(dils):
        if dil == 1:
            os_.append(o_refs[gi][0].astype(F32))
            lses.append(lse_refs[gi][0])
            continue
        sel = unperm_refs[k][...]
        k += 1
        n = tm // 2 // dil
        o_halves, lse_halves = [], []
        for hh in range(2):
            rows = slice(hh * n, (hh + 1) * n)
            o = jnp.concatenate([o_refs[gi][r, rows, :] for r in range(dil)], axis=0)
            lse = jnp.concatenate([lse_refs[gi][r, rows, :] for r in range(dil)], axis=0)
            o_halves.append(jnp.dot(sel, o, preferred_element_type=F32))
            lse_halves.append(_onehot_dot(sel, lse, 3))
        os_.append(jnp.concatenate(o_halves, axis=0))
        lses.append(jnp.concatenate(lse_halves, axis=0))
    m = functools.reduce(jnp.maximum, lses)
    es = [jnp.exp(l - m) for l in lses]
    den = functools.reduce(jnp.add, es)
    ws = [jnp.dot((e / den).astype(BF16), expand_ref[...], preferred_element_type=F32) for e in es]
    o = functools.reduce(jnp.add, [w * o for w, o in zip(ws, os_)]).astype(BF16)
    mix = jnp.dot(o, wo_ref[...], preferred_element_type=F32)
    out_ref[...] = x_ref[...] + _rms_scale(mix) * g_ref[...]


def _oproj_folded(os_, lses, unperms, expand, x, wo, g):
    nb, seq, _ = x.shape
    tm = ROW_TILE
    dils = tuple(o.shape[1] for o in os_)
    folded = lambda dil, w: pl.BlockSpec((None, dil, tm // dil, w), lambda b, i: (b, 0, i, 0))
    row = pl.BlockSpec((None, tm, D_MODEL), lambda b, i: (b, i, 0))
    in_specs = ([folded(d, D_MODEL) for d in dils] + [folded(d, LANES) for d in dils]
                + [_resident(u.shape) for u in unperms]
                + [_resident(expand.shape), row, _resident(wo.shape), _resident((1, D_MODEL))])
    return pl.pallas_call(
        functools.partial(_oproj_folded_kernel, dils=dils),
        grid=(nb, seq // tm),
        in_specs=in_specs,
        out_specs=row,
        out_shape=jax.ShapeDtypeStruct((nb, seq, D_MODEL), F32),
        compiler_params=_params(("parallel", "parallel")),
        name="oproj_folded",
    )(*os_, *lses, *unperms, expand, x, wo, g)


def _rope_tabs(pos):
    half = ROT_DIM // 2
    pos = np.asarray(pos, np.float32)
    inv = np.float32(ROPE_THETA) ** (-np.arange(0, ROT_DIM, 2, dtype=np.float32) / np.float32(ROT_DIM))
    ang = pos[:, None] * inv[None, :]
    cos, sin = np.cos(ang), np.sin(ang)
    n = pos.shape[0]
    pad = np.zeros((n, HEAD_DIM - ROT_DIM), np.float32)
    zer = np.zeros((n, half), np.float32)
    c = np.concatenate([cos, cos, pad + 1.0], axis=1)
    s1 = np.concatenate([-sin, zer, pad], axis=1)
    s2 = np.concatenate([zer, sin, pad], axis=1)
    return tuple(jnp.asarray(np.tile(t, (1, LANES // HEAD_DIM)), F32) for t in (c, s1, s2))


def _band_bias():
    i = np.arange(QBLK)[:, None]
    j = np.arange(QBLK)[None, :]
    prev_ok = j >= i
    cur_ok = j <= i
    later = np.concatenate([prev_ok, cur_ok], axis=1)
    first = np.concatenate([np.zeros_like(prev_ok), cur_ok], axis=1)
    b = np.where(np.stack([first, later]), 0.0, NEG).astype(np.float32)
    return jnp.asarray(np.tile(b, (1, 2, 1)))


def _fold_perm(dil):
    half = ROW_TILE // 2
    n = half // dil
    dst = np.arange(half)
    src = (dst % n) * dil + dst // n
    return (src[:, None] == np.arange(half)[None, :]).astype(np.float32)


def _head_expand():
    sel = np.arange(LANES)[:, None] == _lse_lane(np.arange(D_MODEL)[None, :] // HEAD_DIM)
    return jnp.asarray(sel.astype(np.float32), BF16)


def _heads_ig(w):
    k = w.shape[0]
    return w.reshape(k, N_KV, N_HEADS // N_KV, HEAD_DIM).transpose(0, 2, 1, 3).reshape(k, D_MODEL)


def _block_diag_q(q):
    r = q.shape[0]
    q4 = q.astype(F32).reshape(r, 4, 1, N_KV, HEAD_DIM)
    eye = jnp.eye(N_KV, dtype=F32)[None, None, :, :, None]
    return (q4 * eye).reshape(r, N_HEADS, D_KV)


def _own_head(o):
    r = o.shape[0]
    o5 = o.reshape(r, 4, N_KV, N_KV, HEAD_DIM)
    d = jnp.einsum('rigkd,gk->rigd', o5, jnp.eye(N_KV, dtype=o.dtype))
    return d.transpose(0, 2, 1, 3).reshape(r, D_MODEL)


def _expand_lse(lse):
    r = lse.shape[0]
    l = lse[:, :, 0].reshape(r, 4, N_KV).transpose(0, 2, 1).reshape(r, N_HEADS)
    return jnp.repeat(l, HEAD_DIM, axis=1)


def _window_minor(cache):
    r, w = cache.shape[:2]
    return cache.transpose(0, 2, 3, 4, 1).reshape(r, 2 * D_KV, w)


def _window_major(cache_t):
    r, _, w = cache_t.shape
    return cache_t.reshape(r, 2, N_KV, HEAD_DIM, w).transpose(0, 4, 1, 2, 3)


def kernel(x_prompt, x_sample, cache_a_kv, cache_b_kv_w128, cache_b_kv_w512, cache_b_kv_w2048,
           norm_g, w_ffn_gu, w_ffn_dn, w_qkv_a, sink_a, w_o_a, g_kv_b, w_kv_b, w_q_b, w_o_b):
    nb, seq, _ = x_prompt.shape
    nreq = x_sample.shape[0]
    tp = nb * seq
    b_caches = (cache_b_kv_w128, cache_b_kv_w512, cache_b_kv_w2048)

    def gain(l, k):
        return norm_g[l, k].reshape(1, D_MODEL)

    wgu = w_ffn_gu.astype(BF16)
    wdn = w_ffn_dn.astype(BF16)

    def ffn(x, l, k):
        return _ffn(x, gain(l, 4 * k), wgu, wdn, gain(l, 4 * k + 1), l, k)

    bias = _band_bias()
    tabs_p = _rope_tabs(np.arange(seq))
    tabs_s = _rope_tabs(np.full((nreq,), PAST_LEN))
    perms = {dil: _fold_perm(dil) for _, dil in B_GROUPS if dil > 1}

    wqkv_a = w_qkv_a.astype(BF16)
    wq_b = w_q_b.astype(BF16)
    wkv_b = w_kv_b.astype(BF16)
    col3 = lambda w, j: pl.BlockSpec((None, D_MODEL, w), lambda *_: (0, 0, j), pipeline_mode=pl.Buffered(1))
    col2 = lambda w, j: pl.BlockSpec((D_MODEL, w), lambda *_: (0, j), pipeline_mode=pl.Buffered(1))
    wq_a = w_qkv_a[0][:, :D_MODEL]
    wkv_a = wqkv_a[0][:, D_MODEL:]
    wo_a = w_o_a[0].astype(BF16)
    wo_b = w_o_b[0].astype(BF16)
    g_kv = g_kv_b.reshape(1, D_MODEL)
    sink = sink_a[0].astype(F32)
    sink_rows = sink.reshape(N_KV, 4).T.reshape(N_HEADS, 1)

    assert all(win == dil * LANES for win, dil in B_GROUPS + ((WIN_A, 1),))
    xs = x_sample.reshape(nreq, D_MODEL)
    xs = ffn(xs, 0, 0)
    q, new, newt = _proj_decode(xs, gain(0, 2), gain(0, 2), _heads_ig(wq_a).astype(BF16), wkv_a, tabs_s)
    o, _, cache_a_new = _decode_attention(_block_diag_q(q), _window_minor(cache_a_kv[0]),
                                          new.reshape(nreq, 1, 2 * D_KV), newt, sink_rows)
    xs = _oproj([_own_head(o)], [], xs, wo_a, gain(0, 3))
    xs = ffn(xs, 0, 1)
    xs = ffn(xs, 1, 0)
    jobs = []
    for gi, (win, dil) in enumerate(B_GROUPS):
        wq = _heads_ig(w_q_b[0][:, gi * D_MODEL:(gi + 1) * D_MODEL]).astype(BF16)
        wkv = wkv_b[:, gi * 2 * D_KV:(gi + 1) * 2 * D_KV]
        q, new, newt = _proj_decode(xs, gain(1, 2), g_kv, wq, wkv, tabs_s)
        jobs.append((_block_diag_q(q), _window_minor(b_caches[gi]), new.reshape(nreq, 1, 2 * D_KV), newt))

    dec_o, dec_lse, dec_caches = [], [], None

    def ffn_with_decode(x, l, k):
        nonlocal dec_caches
        req0 = len(dec_o) * (x.shape[0] // ROW_TILE)
        x, o, lse, dec_caches = _ffn_decode(x, gain(l, 4 * k), wgu, wdn, gain(l, 4 * k + 1), l, k,
                                            jobs, req0, dec_caches)
        dec_o.append(o)
        dec_lse.append(lse)
        return x

    x = x_prompt.reshape(tp, D_MODEL)
    x = ffn_with_decode(x, 0, 0)
    q, kv_a, k2, v2 = _proj(x.reshape(nb, seq, D_MODEL), gain(0, 2), gain(0, 2),
                            wqkv_a, col3(D_MODEL, 0), wqkv_a, col3(2 * D_KV, 2), tabs_p, None,
                            dil=1, win=WIN_A)
    o, = _band_attention(q, k2, v2, bias, sink, has_lse=False)
    x = _oproj([o.reshape(tp, D_MODEL)], [], x, wo_a, gain(0, 3))
    x = ffn_with_decode(x, 0, 1)
    x = ffn_with_decode(x, 1, 0).reshape(nb, seq, D_MODEL)
    os_, lses, kv_b = [], [], []
    for gi, (win, dil) in enumerate(B_GROUPS):
        perm = jnp.asarray(perms[dil], BF16) if dil > 1 else None
        q, kvc, k2, v2 = _proj(x, gain(1, 2), g_kv, wq_b, col3(D_MODEL, gi), wkv_b, col2(2 * D_KV, gi),
                               tabs_p, perm, dil=dil, win=win)
        o, lse = _band_attention(q, k2, v2, bias, None, has_lse=True)
        os_.append(o)
        lses.append(lse)
        kv_b.append(kvc.reshape(nb, win, 2, N_KV, HEAD_DIM))
    unperms = [jnp.asarray(perms[d].T, BF16) for _, d in B_GROUPS if d > 1]
    x = _oproj_folded(os_, lses, unperms, _head_expand(), x, wo_b, gain(1, 3))
    y_p = ffn_with_decode(x.reshape(tp, D_MODEL), 1, 1).reshape(nb, seq, D_MODEL)
    a_p = kv_a.reshape(1, nb, WIN_A, 2, N_KV, HEAD_DIM)

    assert len(dec_o) * (tp // ROW_TILE) == nreq
    os_ = [_own_head(jnp.concatenate([o[gi] for o in dec_o], axis=0)) for gi in range(len(B_GROUPS))]
    lses = [_expand_lse(jnp.concatenate([l[gi] for l in dec_lse], axis=0)) for gi in range(len(B_GROUPS))]
    b_s = [_window_major(c) for c in dec_caches]
    xs = _oproj(os_, lses, xs, wo_b, gain(1, 3))
    y_s = ffn(xs, 1, 1).reshape(nreq, 1, D_MODEL)
    a_s = _window_major(cache_a_new)[None]

    return (y_p, y_s, a_p, kv_b[0], kv_b[1], kv_b[2], a_s, b_s[0], b_s[1], b_s[2])
```

```python
import functools

import jax
import jax.numpy as jnp
import numpy as np
from jax.experimental import pallas as pl
from jax.experimental.pallas import tpu as pltpu

F32 = jnp.float32
BF16 = jnp.bfloat16

D_MODEL = 1024
HEAD_DIM = 64
ROT_DIM = HEAD_DIM // 4
ROPE_THETA = 500000.0
N_HEADS = 16
N_KV = 4
D_KV = N_KV * HEAD_DIM
D_FF = 2816
EPS = 1e-6
PAST_LEN = 8192
WIN_A = 128
B_GROUPS = ((128, 1), (512, 4), (2048, 16))

LANES = 128
FF_CHUNK = 256
ROW_TILE = 512
QBLK = 128
NEG = -1e30
LOG2E = 1.4426950408889634
LN2 = 0.6931471805599453
Q_SCALE = HEAD_DIM ** -0.5 * LOG2E
MAX_QBLKS = 4
VMEM_LIMIT = 56 * 1024 * 1024
VMEM_LIMIT_FUSED = 60 * 1024 * 1024
DECODE_BLOCK_BYTES = 4 * 1024 * 1024
SHIFT_ROWS = 128
DECODE_UNROLL = 4


def _params(sem, vmem_limit=VMEM_LIMIT):
    return pltpu.CompilerParams(dimension_semantics=sem, vmem_limit_bytes=vmem_limit)


def _resident(shape):
    nd = len(shape)
    return pl.BlockSpec(shape, lambda *_: (0,) * nd, pipeline_mode=pl.Buffered(1))


def _rms_scale(x):
    return x * jax.lax.rsqrt(jnp.mean(x * x, axis=-1, keepdims=True) + EPS)


def _split_bf16(x, parts):
    out = []
    for _ in range(parts - 1):
        hi = x.astype(BF16)
        out.append(hi)
        x = x - hi.astype(F32)
    out.append(x.astype(BF16))
    return out


def _onehot_dot(sel, x, parts):
    return functools.reduce(
        jnp.add, [jnp.dot(sel, p, preferred_element_type=F32) for p in _split_bf16(x, parts)])


def _ffn_kernel(x_ref, gpre_ref, wgu_ref, wd_ref, gpost_ref, o_ref, side_work=(), mix=None):
    x = x_ref[...]
    if mix is not None:
        x = x + mix()
    xn = (_rms_scale(x) * gpre_ref[...]).astype(BF16)
    acc = None
    n_chunks = D_FF // FF_CHUNK
    for c in range(n_chunks):
        sl = slice(c * FF_CHUNK, (c + 1) * FF_CHUNK)
        gate = jnp.dot(xn, wgu_ref[:, sl], preferred_element_type=F32)
        up = jnp.dot(xn, wgu_ref[:, D_FF + c * FF_CHUNK:D_FF + (c + 1) * FF_CHUNK],
                     preferred_element_type=F32)
        act = (gate * jax.nn.sigmoid(gate) * up).astype(BF16)
        part = jnp.dot(act, wd_ref[sl, :], preferred_element_type=F32)
        acc = part if acc is None else acc + part
        for work in side_work[c::n_chunks]:
            work()
    o_ref[...] = x + 0.5 * (_rms_scale(acc) * gpost_ref[...])


def _ffn_specs(t, wgu, wd, l, k):
    tm = min(ROW_TILE, t)
    row = pl.BlockSpec((tm, D_MODEL), lambda i: (i, 0))
    weight = lambda w: pl.BlockSpec((None, None) + w.shape[2:], lambda i: (l, k, 0, 0),
                                    pipeline_mode=pl.Buffered(1))
    in_specs = [row, _resident((1, D_MODEL)), weight(wgu), weight(wd), _resident((1, D_MODEL))]
    return t // tm, in_specs, row


def _ffn(x, gpre, wgu, wd, gpost, l, k):
    t = x.shape[0]
    steps, in_specs, row = _ffn_specs(t, wgu, wd, l, k)
    return pl.pallas_call(
        _ffn_kernel,
        grid=(steps,),
        in_specs=in_specs,
        out_specs=row,
        out_shape=jax.ShapeDtypeStruct((t, D_MODEL), F32),
        compiler_params=_params(("parallel",)),
        name="ffn",
    )(x, gpre, wgu, wd, gpost)


def _ffn_decode_kernel(*refs, n_jobs, req0, has_prev, mix_dils):
    per_job = 5 if has_prev else 4
    ffn_in = refs[:5]
    n_mix = 0
    mix = None
    if mix_dils is not None:
        ng = len(mix_dils)
        n_mix = 2 * ng + sum(d > 1 for d in mix_dils) + 3 if ng > 1 else 3
        mix_refs = refs[5:5 + n_mix]
        wo_ref, g_ref = mix_refs[-2:]

        def mix():
            if ng == 1:
                o = mix_refs[0][...]
            else:
                o = _merge_groups(mix_refs[:ng], mix_refs[ng:2 * ng], mix_refs[2 * ng:-3], mix_refs[-3],
                                  ffn_in[0].shape[0], mix_dils)
            return _rms_scale(jnp.dot(o, wo_ref[...], preferred_element_type=F32)) * g_ref[...]

    first = 5 + n_mix
    job_in = [refs[first + j * per_job:first + (j + 1) * per_job] for j in range(n_jobs)]
    outs = refs[first + n_jobs * per_job:]
    req = req0 + pl.program_id(0)
    side_work = []
    for j in range(n_jobs):
        qbd_ref, cache_ref, new_ref, newt_ref = job_in[j][:4]
        o_ref, lse_ref, cout_ref = outs[1 + 3 * j:4 + 3 * j]
        side_work.append(functools.partial(
            _decode_attend, qbd_ref, cache_ref, 0, new_ref, None, o_ref, lse_ref))
        for r0 in range(0, 2 * D_KV, SHIFT_ROWS):
            side_work.append(functools.partial(
                _shift_rows, cache_ref, 0, newt_ref, req, cout_ref, r0, r0 + SHIFT_ROWS))
    _ffn_kernel(*ffn_in, outs[0], side_work=side_work, mix=mix)


def _ffn_decode(x, gpre, wgu, wd, gpost, l, k, jobs, req0, prev_caches, mixer=None):
    t = x.shape[0]
    steps, in_specs, row = _ffn_specs(t, wgu, wd, l, k)
    has_prev = prev_caches is not None
    args = [x, gpre, wgu, wd, gpost]
    mix_dils = None
    if mixer is not None:
        os_, lses, unperms, expand, wo, g = mixer
        mix_dils = tuple(o.shape[1] for o in os_)
        tiles = os_[0].shape[2] * mix_dils[0] // ROW_TILE
        folded = lambda dil, w: pl.BlockSpec((None, dil, ROW_TILE // dil, w),
                                             lambda i: (i // tiles, 0, i % tiles, 0))
        if len(os_) == 1:
            in_specs += [pl.BlockSpec((None, None, ROW_TILE, D_MODEL), lambda i: (i // tiles, 0, i % tiles, 0))]
            args += [os_[0]]
        else:
            in_specs += ([folded(d, D_MODEL) for d in mix_dils] + [folded(d, LANES) for d in mix_dils]
                         + [_resident(u.shape) for u in unperms] + [_resident(expand.shape)])
            args += [*os_, *lses, *unperms, expand]
        in_specs += [_resident(wo.shape), _resident((1, D_MODEL))]
        args += [wo, g]
    out_specs, out_shape, aliases = [row], [jax.ShapeDtypeStruct((t, D_MODEL), F32)], {}
    for j, (qbd, cache, new, newt) in enumerate(jobs):
        _, feat, win = cache.shape
        req = lambda *tail: pl.BlockSpec((1,) + tail, lambda i: (req0 + i,) + (0,) * len(tail))
        loc = lambda *tail: pl.BlockSpec((1,) + tail, lambda i: (i,) + (0,) * len(tail))
        in_specs += [req(N_HEADS, D_KV), req(feat, win), req(1, 2 * D_KV), _resident(newt.shape)]
        args += [qbd, cache, new, newt]
        if has_prev:
            aliases[len(args)] = 3 + 3 * j
            in_specs.append(pl.BlockSpec(memory_space=pl.ANY))
            args.append(prev_caches[j])
        out_specs += [loc(N_HEADS, D_KV), loc(N_HEADS, LANES), req(feat, win)]
        out_shape += [jax.ShapeDtypeStruct((steps, N_HEADS, D_KV), F32),
                      jax.ShapeDtypeStruct((steps, N_HEADS, LANES), F32),
                      jax.ShapeDtypeStruct(cache.shape, F32)]
    res = pl.pallas_call(
        functools.partial(_ffn_decode_kernel, n_jobs=len(jobs), req0=req0, has_prev=has_prev,
                          mix_dils=mix_dils),
        grid=(steps,),
        in_specs=in_specs, out_specs=out_specs, out_shape=out_shape,
        input_output_aliases=aliases,
        compiler_params=_params(("arbitrary",), VMEM_LIMIT_FUSED),
        name="ffn_decode",
    )(*args)
    return res[0], list(res[1::3]), list(res[2::3]), list(res[3::3])


def _project(x_ref, gq_ref, gk_ref, wq_ref, wkv_ref, rc_ref, rs1_ref, rs2_ref):
    xh = _rms_scale(x_ref[...])
    hq = (xh * gq_ref[...]).astype(BF16)
    hk = (xh * gk_ref[...]).astype(BF16)
    rc, rs1, rs2 = rc_ref[...], rs1_ref[...], rs2_ref[...]

    def rope(t):
        return t * rc + pltpu.roll(t, LANES - ROT_DIM // 2, 1) * rs1 + pltpu.roll(t, ROT_DIM // 2, 1) * rs2

    q = jnp.dot(hq, wq_ref[...], preferred_element_type=F32)
    qs = [(rope(q[:, c * LANES:(c + 1) * LANES]) * Q_SCALE).astype(BF16)
          for c in range(D_MODEL // LANES)]
    kv = jnp.dot(hk, wkv_ref[...], preferred_element_type=F32)
    kvs = []
    for c in range(2 * D_KV // LANES):
        t = kv[:, c * LANES:(c + 1) * LANES]
        kvs.append(rope(t) if c < D_KV // LANES else t)
    return qs, kvs


def _dup_heads(t):
    lo = jax.lax.broadcasted_iota(jnp.int32, t.shape, 1) < HEAD_DIM
    sw = pltpu.roll(t, HEAD_DIM, 1)
    return [jnp.where(lo, t, sw).astype(BF16), jnp.where(lo, sw, t).astype(BF16)]


def _proj_kernel(*refs, dil, cache_rows):
    ins = refs[:8]
    perm_ref = refs[8] if dil > 1 else None
    q_ref, kv_ref, k2_ref, v2_ref = refs[-4:]
    tm = ins[0].shape[0]
    qs, kvs = _project(*ins)
    for c, t in enumerate(kvs):
        kv_ref[:, c * LANES:(c + 1) * LANES] = t[tm - cache_rows:, :]
    nk = D_KV // LANES
    if dil == 1:
        q_ref[0] = jnp.concatenate(qs, axis=1)
        k2_ref[0] = jnp.concatenate([d for t in kvs[:nk] for d in _dup_heads(t)], axis=1)
        v2_ref[0] = jnp.concatenate([d for t in kvs[nk:] for d in _dup_heads(t)], axis=1)
        return
    cols = jnp.concatenate(qs + [t.astype(BF16) for t in kvs], axis=1)
    half = tm // 2
    n = half // dil
    for hh in range(2):
        f = jnp.dot(perm_ref[...], cols[hh * half:(hh + 1) * half, :], preferred_element_type=F32)
        qf = f[:, :D_MODEL].astype(BF16)
        kf = jnp.concatenate([d for c in range(nk)
                              for d in _dup_heads(f[:, D_MODEL + c * LANES:D_MODEL + (c + 1) * LANES])], axis=1)
        vf = jnp.concatenate([d for c in range(nk, 2 * nk)
                              for d in _dup_heads(f[:, D_MODEL + c * LANES:D_MODEL + (c + 1) * LANES])], axis=1)
        for r in range(dil):
            dst = slice(hh * n, (hh + 1) * n)
            q_ref[r, dst, :] = qf[r * n:(r + 1) * n, :]
            k2_ref[r, dst, :] = kf[r * n:(r + 1) * n, :]
            v2_ref[r, dst, :] = vf[r * n:(r + 1) * n, :]


def _proj(x, gq, gk, wq, wq_spec, wkv, wkv_spec, tabs, perm, *, dil, win):
    nb, seq, _ = x.shape
    tm = ROW_TILE
    nt = seq // tm
    cache_rows = min(win, tm)
    first_cache_tile = nt - win // cache_rows
    n = tm // dil
    tab = pl.BlockSpec((tm, LANES), lambda b, i: (i, 0))
    folded = lambda w: pl.BlockSpec((None, dil, n, w), lambda b, i: (b, 0, i, 0))
    in_specs = [pl.BlockSpec((None, tm, D_MODEL), lambda b, i: (b, i, 0)),
                _resident((1, D_MODEL)), _resident((1, D_MODEL)),
                wq_spec, wkv_spec, tab, tab, tab]
    args = [x, gq, gk, wq, wkv, *tabs]
    if dil > 1:
        in_specs.append(_resident(perm.shape))
        args.append(perm)
    return pl.pallas_call(
        functools.partial(_proj_kernel, dil=dil, cache_rows=cache_rows),
        grid=(nb, nt),
        in_specs=in_specs,
        out_specs=[folded(D_MODEL),
                   pl.BlockSpec((None, cache_rows, 2 * D_KV),
                                lambda b, i: (b, jnp.maximum(i - first_cache_tile, 0), 0)),
                   folded(2 * D_KV), folded(2 * D_KV)],
        out_shape=[jax.ShapeDtypeStruct((nb, dil, seq // dil, D_MODEL), BF16),
                   jax.ShapeDtypeStruct((nb, win, 2 * D_KV), F32),
                   jax.ShapeDtypeStruct((nb, dil, seq // dil, 2 * D_KV), BF16),
                   jax.ShapeDtypeStruct((nb, dil, seq // dil, 2 * D_KV), BF16)],
        compiler_params=_params(("parallel", "arbitrary")),
        name="proj",
    )(*args)


def _proj_decode_kernel(*refs):
    q_ref, new_ref, newt_ref = refs[-3:]
    qs, kvs = _project(*refs[:8])
    for c, t in enumerate(qs):
        q_ref[:, c * LANES:(c + 1) * LANES] = t
    for c, t in enumerate(kvs):
        new_ref[:, c * LANES:(c + 1) * LANES] = t
        newt_ref[c * LANES:(c + 1) * LANES, :] = t.T


def _proj_decode(x, gq, gk, wq, wkv, tabs):
    nreq = x.shape[0]
    full = lambda w: pl.BlockSpec((nreq, w), lambda i: (0, 0))
    return pl.pallas_call(
        _proj_decode_kernel,
        grid=(1,),
        in_specs=[full(D_MODEL), _resident((1, D_MODEL)), _resident((1, D_MODEL)),
                  _resident(wq.shape), _resident(wkv.shape), full(LANES), full(LANES), full(LANES)],
        out_specs=[full(D_MODEL), full(2 * D_KV), pl.BlockSpec((2 * D_KV, nreq), lambda i: (0, 0))],
        out_shape=[jax.ShapeDtypeStruct((nreq, D_MODEL), BF16),
                   jax.ShapeDtypeStruct((nreq, 2 * D_KV), F32),
                   jax.ShapeDtypeStruct((2 * D_KV, nreq), F32)],
        compiler_params=_params(("arbitrary",)),
        name="proj_decode",
    )(x, gq, gk, wq, wkv, *tabs)


def _lse_lane(head):
    return (head % 2) * HEAD_DIM + head - head % 2


def _band_kernel(*refs, has_sink, has_lse, nq):
    refs = list(refs)
    sink_ref = refs.pop(0) if has_sink else None
    q_ref, kp_ref, kc_ref, vp_ref, vc_ref, bias_ref, o_ref = refs[:7]
    lse_ref = refs[7] if has_lse else None

    lo = jax.lax.broadcasted_iota(jnp.int32, (2 * QBLK, LANES), 1) < HEAD_DIM
    lane_q = jax.lax.broadcasted_iota(jnp.int32, (QBLK, LANES), 1)
    top = jax.lax.broadcasted_iota(jnp.int32, (2 * QBLK, 1), 0) < QBLK
    zero = jnp.zeros((), BF16)
    krow_lo = jax.lax.broadcasted_iota(jnp.int32, (4 * QBLK, LANES), 0) < 2 * QBLK
    klane_lo = jax.lax.broadcasted_iota(jnp.int32, (4 * QBLK, LANES), 1) < HEAD_DIM
    ones_sel = jnp.where(krow_lo == klane_lo, 1.0, 0.0).astype(BF16)
    first_step = pl.program_id(2) == 0

    for t in range(nq):
        rows = slice(t * QBLK, (t + 1) * QBLK)
        bias = bias_ref[1] if t else bias_ref[jnp.where(first_step, 0, 1)]
        lse_tile = jnp.zeros((QBLK, LANES), F32)
        for h in range(N_KV):
            ksl = slice(h * LANES, (h + 1) * LANES)
            kprev = kc_ref[(t - 1) * QBLK:t * QBLK, ksl] if t else kp_ref[:, ksl]
            vprev = vc_ref[(t - 1) * QBLK:t * QBLK, ksl] if t else vp_ref[:, ksl]
            k2 = jnp.concatenate([kprev, kc_ref[rows, ksl]], axis=0)
            v2 = jnp.concatenate([vprev, vc_ref[rows, ksl]], axis=0)
            kbd = jnp.concatenate([jnp.where(lo, k2, zero), jnp.where(lo, zero, k2)], axis=0)
            vbd = jnp.concatenate([jnp.where(lo, v2, zero), jnp.where(lo, zero, v2)], axis=0)
            vext = jnp.concatenate([vbd, ones_sel], axis=1)
            qq = jnp.concatenate([q_ref[rows, (2 * h) * LANES:(2 * h + 1) * LANES],
                                  q_ref[rows, (2 * h + 1) * LANES:(2 * h + 2) * LANES]], axis=0)
            s = jax.lax.dot_general(qq, kbd, (((1,), (1,)), ((), ())),
                                    preferred_element_type=F32)
            ps, ms, sks = [], [], []
            for c in range(2):
                sc = s[:, c * 2 * QBLK:(c + 1) * 2 * QBLK] + bias
                m = jnp.max(sc, axis=-1, keepdims=True)
                if has_sink:
                    sk = jnp.where(top, sink_ref[4 * h + c], sink_ref[4 * h + 2 + c]) * LOG2E
                    m = jnp.maximum(m, sk)
                    sks.append(sk)
                ps.append(jnp.exp2(sc - m).astype(BF16))
                ms.append(m)
            ov = jnp.dot(jnp.concatenate(ps, axis=1), vext, preferred_element_type=F32)
            l = ov[:, LANES:]
            if has_sink:
                l = l + jnp.where(lo, jnp.exp2(sks[0] - ms[0]), jnp.exp2(sks[1] - ms[1]))
            o = ov[:, :LANES] / l
            o_ref[rows, (2 * h) * LANES:(2 * h + 1) * LANES] = o[:QBLK].astype(o_ref.dtype)
            o_ref[rows, (2 * h + 1) * LANES:(2 * h + 2) * LANES] = o[QBLK:].astype(o_ref.dtype)
            if has_lse:
                lse = jnp.where(lo, ms[0], ms[1]) * LN2 + jnp.log(l)
                for j in range(2):
                    pair = (lane_q == _lse_lane(4 * h + 2 * j)) | (lane_q == _lse_lane(4 * h + 2 * j + 1))
                    lse_tile = jnp.where(pair, lse[j * QBLK:(j + 1) * QBLK], lse_tile)
        if has_lse:
            lse_ref[rows, :] = lse_tile


def _band_attention(q, k2, v2, bias, sink, *, has_lse):
    nb, dil, length, _ = q.shape
    nq = min(MAX_QBLKS, length // QBLK)
    grid = (nb, dil, length // (nq * QBLK))
    has_sink = sink is not None
    cur = lambda w: pl.BlockSpec((None, None, nq * QBLK, w), lambda b, r, i: (b, r, i, 0))
    prev = lambda w: pl.BlockSpec((None, None, QBLK, w),
                                  lambda b, r, i: (b, r, jnp.maximum(i * nq - 1, 0), 0))
    in_specs = [cur(D_MODEL), prev(2 * D_KV), cur(2 * D_KV), prev(2 * D_KV), cur(2 * D_KV),
                _resident(bias.shape)]
    args = [q, k2, k2, v2, v2, bias]
    if has_sink:
        in_specs = [pl.BlockSpec(memory_space=pltpu.SMEM)] + in_specs
        args = [sink] + args
    out_specs = [cur(D_MODEL)]
    out_shape = [jax.ShapeDtypeStruct((nb, dil, length, D_MODEL), BF16)]
    if has_lse:
        out_specs.append(cur(LANES))
        out_shape.append(jax.ShapeDtypeStruct((nb, dil, length, LANES), F32))
    return pl.pallas_call(
        functools.partial(_band_kernel, has_sink=has_sink, has_lse=has_lse, nq=nq),
        grid=grid, in_specs=in_specs, out_specs=out_specs, out_shape=out_shape,
        compiler_params=_params(("parallel", "parallel", "parallel")),
        name="band_attn",
    )(*args)


def _tap_columns(cache_ref, b, r0):
    dil = cache_ref.shape[2] // LANES
    if dil == 1:
        return cache_ref[b, r0:r0 + D_KV, :]
    lane = jax.lax.broadcasted_iota(jnp.int32, (D_KV, LANES), 1)
    is_tap = (lane & (dil - 1)) == 0
    acc = None
    for t in range(dil):
        part = jnp.where(is_tap, cache_ref[b, r0:r0 + D_KV, t * LANES:(t + 1) * LANES], 0.0)
        if t:
            part = pltpu.roll(part, t, 1)
        acc = part if acc is None else acc + part
    return acc


def _decode_attend(qbd_ref, cache_ref, b, new_ref, sink_ref, o_ref, lse_ref):
    qb = qbd_ref[b]
    new = new_ref[b]
    kn = new[:, 0:D_KV]
    vn = new[:, D_KV:2 * D_KV]
    s = jnp.dot(qb.astype(BF16), _tap_columns(cache_ref, b, 0).astype(BF16),
                preferred_element_type=F32)
    sn = jnp.sum(qb * kn, axis=-1, keepdims=True)
    m = jnp.maximum(jnp.max(s, axis=-1, keepdims=True), sn)
    if sink_ref is not None:
        sk = sink_ref[...] * LOG2E
        m = jnp.maximum(m, sk)
    p = jnp.exp2(s - m)
    pn = jnp.exp2(sn - m)
    l = jnp.sum(p, axis=-1, keepdims=True) + pn
    if sink_ref is not None:
        l = l + jnp.exp2(sk - m)
    o = jax.lax.dot_general(p.astype(BF16), _tap_columns(cache_ref, b, D_KV).astype(BF16),
                            (((1,), (1,)), ((), ())), preferred_element_type=F32) + pn * vn
    o_ref[b] = o / l
    lse_ref[b] = jnp.broadcast_to(m * LN2 + jnp.log(l), (N_HEADS, LANES))


def _shift_rows(cache_ref, b, newt_ref, req, cout_ref, r0, r1):
    win = cache_ref.shape[2]
    lane = jax.lax.broadcasted_iota(jnp.int32, (r1 - r0, LANES), 1)
    shifted = pltpu.roll(cache_ref[b, r0:r1, :], win - 1, 1)
    newcol = pltpu.roll(newt_ref[r0:r1, :], LANES - 1 - req, 1)
    cout_ref[b, r0:r1, :] = shifted
    cout_ref[b, r0:r1, win - LANES:win] = jnp.where(lane == LANES - 1, newcol, shifted[:, win - LANES:win])


def _decode_kernel(*refs, has_sink, rb):
    refs = list(refs)
    sink_ref = refs.pop(0) if has_sink else None
    qbd_ref, cache_ref, new_ref, newt_ref, o_ref, lse_ref, cout_ref = refs
    base = pl.program_id(0) * rb

    def body(b, carry):
        _decode_attend(qbd_ref, cache_ref, b, new_ref, sink_ref, o_ref, lse_ref)
        _shift_rows(cache_ref, b, newt_ref, base + b, cout_ref, 0, 2 * D_KV)
        return carry

    jax.lax.fori_loop(0, rb, body, 0, unroll=min(rb, DECODE_UNROLL))


def _decode_attention(qbd, cache, new, newt, sink_rows):
    nreq, feat, win = cache.shape
    rb = max(1, DECODE_BLOCK_BYTES // (feat * win * 4))
    has_sink = sink_rows is not None
    req = lambda *tail: pl.BlockSpec((rb,) + tail, lambda i: (i,) + (0,) * len(tail))
    in_specs = [req(N_HEADS, D_KV), req(feat, win), req(1, 2 * D_KV),
                pl.BlockSpec((feat, nreq), lambda i: (0, 0))]
    args = [qbd, cache, new, newt]
    if has_sink:
        in_specs = [pl.BlockSpec((N_HEADS, 1), lambda i: (0, 0))] + in_specs
        args = [sink_rows] + args
    return pl.pallas_call(
        functools.partial(_decode_kernel, has_sink=has_sink, rb=rb),
        grid=(nreq // rb,), in_specs=in_specs,
        out_specs=[req(N_HEADS, D_KV), req(N_HEADS, LANES), req(feat, win)],
        out_shape=[jax.ShapeDtypeStruct((nreq, N_HEADS, D_KV), F32),
                   jax.ShapeDtypeStruct((nreq, N_HEADS, LANES), F32),
                   jax.ShapeDtypeStruct(cache.shape, F32)],
        compiler_params=_params(("parallel",)),
        name="decode_attn",
    )(*args)


def _oproj_kernel(*refs, n_groups):
    o_refs = refs[:n_groups]
    lse_refs = refs[n_groups:2 * n_groups] if n_groups > 1 else ()
    x_ref, wo_ref, g_ref, out_ref = refs[-4:]
    if n_groups == 1:
        o = o_refs[0][...].astype(BF16)
    else:
        lses = [r[...] for r in lse_refs]
        m = functools.reduce(jnp.maximum, lses)
        es = [jnp.exp(l - m) for l in lses]
        den = functools.reduce(jnp.add, es)
        o = functools.reduce(jnp.add, [(e / den) * r[...].astype(F32)
                                       for e, r in zip(es, o_refs)]).astype(BF16)
    mix = jnp.dot(o, wo_ref[...], preferred_element_type=F32)
    out_ref[...] = x_ref[...] + _rms_scale(mix) * g_ref[...]


def _oproj(os_, lses, x, wo, g):
    t = x.shape[0]
    tm = min(ROW_TILE, t)
    row = pl.BlockSpec((tm, D_MODEL), lambda i: (i, 0))
    n_groups = len(os_)
    return pl.pallas_call(
        functools.partial(_oproj_kernel, n_groups=n_groups),
        grid=(t // tm,),
        in_specs=[row] * (n_groups + len(lses) + 1) + [_resident(wo.shape), _resident((1, D_MODEL))],
        out_specs=row,
        out_shape=jax.ShapeDtypeStruct((t, D_MODEL), F32),
        compiler_params=_params(("parallel",)),
        name="oproj",
    )(*os_, *lses, x, wo, g)


def _merge_groups(o_refs, lse_refs, unperm_refs, expand_ref, tm, dils):
    os_, lses, k = [], [], 0
    for gi, dil in enumerate(dils):
        if dil == 1:
            os_.append(o_refs[gi][0].astype(F32))
            lses.append(lse_refs[gi][0])
            continue
        sel = unperm_refs[k][...]
        k += 1
        n = tm // 2 // dil
        o_halves, lse_halves = [], []
        for hh in range(2):
            rows = slice(hh * n, (hh + 1) * n)
            o = jnp.concatenate([o_refs[gi][r, rows, :] for r in range(dil)], axis=0)
            lse = jnp.concatenate([lse_refs[gi][r, rows, :] for r in range(dil)], axis=0)
            o_halves.append(jnp.dot(sel, o, preferred_element_type=F32))
            lse_halves.append(_onehot_dot(sel, lse, 3))
        os_.append(jnp.concatenate(o_halves, axis=0))
        lses.append(jnp.concatenate(lse_halves, axis=0))
    m = functools.reduce(jnp.maximum, lses)
    es = [jnp.exp(l - m) for l in lses]
    den = functools.reduce(jnp.add, es)
    ws = [jnp.dot((e / den).astype(BF16), expand_ref[...], preferred_element_type=F32) for e in es]
    return functools.reduce(jnp.add, [w * o for w, o in zip(ws, os_)]).astype(BF16)


def _oproj_folded_kernel(*refs, dils):
    ng = len(dils)
    o_refs, lse_refs = refs[:ng], refs[ng:2 * ng]
    unperm_refs = refs[2 * ng:2 * ng + sum(d > 1 for d in dils)]
    expand_ref, x_ref, wo_ref, g_ref, out_ref = refs[-5:]
    o = _merge_groups(o_refs, lse_refs, unperm_refs, expand_ref, x_ref.shape[0], dils)
    mix = jnp.dot(o, wo_ref[...], preferred_element_type=F32)
    out_ref[...] = x_ref[...] + _rms_scale(mix) * g_ref[...]


def _oproj_folded(os_, lses, unperms, expand, x, wo, g):
    nb, seq, _ = x.shape
    tm = ROW_TILE
    dils = tuple(o.shape[1] for o in os_)
    folded = lambda dil, w: pl.BlockSpec((None, dil, tm // dil, w), lambda b, i: (b, 0, i, 0))
    row = pl.BlockSpec((None, tm, D_MODEL), lambda b, i: (b, i, 0))
    in_specs = ([folded(d, D_MODEL) for d in dils] + [folded(d, LANES) for d in dils]
                + [_resident(u.shape) for u in unperms]
                + [_resident(expand.shape), row, _resident(wo.shape), _resident((1, D_MODEL))])
    return pl.pallas_call(
        functools.partial(_oproj_folded_kernel, dils=dils),
        grid=(nb, seq // tm),
        in_specs=in_specs,
        out_specs=row,
        out_shape=jax.ShapeDtypeStruct((nb, seq, D_MODEL), F32),
        compiler_params=_params(("parallel", "parallel")),
        name="oproj_folded",
    )(*os_, *lses, *unperms, expand, x, wo, g)


def _rope_tabs(pos):
    half = ROT_DIM // 2
    pos = np.asarray(pos, np.float32)
    inv = np.float32(ROPE_THETA) ** (-np.arange(0, ROT_DIM, 2, dtype=np.float32) / np.float32(ROT_DIM))
    ang = pos[:, None] * inv[None, :]
    cos, sin = np.cos(ang), np.sin(ang)
    n = pos.shape[0]
    pad = np.zeros((n, HEAD_DIM - ROT_DIM), np.float32)
    zer = np.zeros((n, half), np.float32)
    c = np.concatenate([cos, cos, pad + 1.0], axis=1)
    s1 = np.concatenate([-sin, zer, pad], axis=1)
    s2 = np.concatenate([zer, sin, pad], axis=1)
    return tuple(jnp.asarray(np.tile(t, (1, LANES // HEAD_DIM)), F32) for t in (c, s1, s2))


def _band_bias():
    i = np.arange(QBLK)[:, None]
    j = np.arange(QBLK)[None, :]
    prev_ok = j >= i
    cur_ok = j <= i
    later = np.concatenate([prev_ok, cur_ok], axis=1)
    first = np.concatenate([np.zeros_like(prev_ok), cur_ok], axis=1)
    b = np.where(np.stack([first, later]), 0.0, NEG).astype(np.float32)
    return jnp.asarray(np.tile(b, (1, 2, 1)))


def _fold_perm(dil):
    half = ROW_TILE // 2
    n = half // dil
    dst = np.arange(half)
    src = (dst % n) * dil + dst // n
    return (src[:, None] == np.arange(half)[None, :]).astype(np.float32)


def _head_expand():
    sel = np.arange(LANES)[:, None] == _lse_lane(np.arange(D_MODEL)[None, :] // HEAD_DIM)
    return jnp.asarray(sel.astype(np.float32), BF16)


def _heads_ig(w):
    k = w.shape[0]
    return w.reshape(k, N_KV, N_HEADS // N_KV, HEAD_DIM).transpose(0, 2, 1, 3).reshape(k, D_MODEL)


def _block_diag_q(q):
    r = q.shape[0]
    q4 = q.astype(F32).reshape(r, 4, 1, N_KV, HEAD_DIM)
    eye = jnp.eye(N_KV, dtype=F32)[None, None, :, :, None]
    return (q4 * eye).reshape(r, N_HEADS, D_KV)


def _own_head(o):
    r = o.shape[0]
    o5 = o.reshape(r, 4, N_KV, N_KV, HEAD_DIM)
    d = jnp.einsum('rigkd,gk->rigd', o5, jnp.eye(N_KV, dtype=o.dtype))
    return d.transpose(0, 2, 1, 3).reshape(r, D_MODEL)


def _expand_lse(lse):
    r = lse.shape[0]
    l = lse[:, :, 0].reshape(r, 4, N_KV).transpose(0, 2, 1).reshape(r, N_HEADS)
    return jnp.repeat(l, HEAD_DIM, axis=1)


def _window_minor(cache):
    r, w = cache.shape[:2]
    return cache.transpose(0, 2, 3, 4, 1).reshape(r, 2 * D_KV, w)


def _window_major(cache_t):
    r, _, w = cache_t.shape
    return cache_t.reshape(r, 2, N_KV, HEAD_DIM, w).transpose(0, 4, 1, 2, 3)


def kernel(x_prompt, x_sample, cache_a_kv, cache_b_kv_w128, cache_b_kv_w512, cache_b_kv_w2048,
           norm_g, w_ffn_gu, w_ffn_dn, w_qkv_a, sink_a, w_o_a, g_kv_b, w_kv_b, w_q_b, w_o_b):
    nb, seq, _ = x_prompt.shape
    nreq = x_sample.shape[0]
    tp = nb * seq
    b_caches = (cache_b_kv_w128, cache_b_kv_w512, cache_b_kv_w2048)

    def gain(l, k):
        return norm_g[l, k].reshape(1, D_MODEL)

    wgu = w_ffn_gu.astype(BF16)
    wdn = w_ffn_dn.astype(BF16)

    def ffn(x, l, k):
        return _ffn(x, gain(l, 4 * k), wgu, wdn, gain(l, 4 * k + 1), l, k)

    bias = _band_bias()
    tabs_p = _rope_tabs(np.arange(seq))
    tabs_s = _rope_tabs(np.full((nreq,), PAST_LEN))
    perms = {dil: _fold_perm(dil) for _, dil in B_GROUPS if dil > 1}

    wqkv_a = w_qkv_a.astype(BF16)
    wq_b = w_q_b.astype(BF16)
    wkv_b = w_kv_b.astype(BF16)
    col3 = lambda w, j: pl.BlockSpec((None, D_MODEL, w), lambda *_: (0, 0, j), pipeline_mode=pl.Buffered(1))
    col2 = lambda w, j: pl.BlockSpec((D_MODEL, w), lambda *_: (0, j), pipeline_mode=pl.Buffered(1))
    wq_a = w_qkv_a[0][:, :D_MODEL]
    wkv_a = wqkv_a[0][:, D_MODEL:]
    wo_a = w_o_a[0].astype(BF16)
    wo_b = w_o_b[0].astype(BF16)
    g_kv = g_kv_b.reshape(1, D_MODEL)
    sink = sink_a[0].astype(F32)
    sink_rows = sink.reshape(N_KV, 4).T.reshape(N_HEADS, 1)

    assert all(win == dil * LANES for win, dil in B_GROUPS + ((WIN_A, 1),))
    xs = x_sample.reshape(nreq, D_MODEL)
    xs = ffn(xs, 0, 0)
    q, new, newt = _proj_decode(xs, gain(0, 2), gain(0, 2), _heads_ig(wq_a).astype(BF16), wkv_a, tabs_s)
    o, _, cache_a_new = _decode_attention(_block_diag_q(q), _window_minor(cache_a_kv[0]),
                                          new.reshape(nreq, 1, 2 * D_KV), newt, sink_rows)
    xs = _oproj([_own_head(o)], [], xs, wo_a, gain(0, 3))
    xs = ffn(xs, 0, 1)
    xs = ffn(xs, 1, 0)
    jobs = []
    for gi, (win, dil) in enumerate(B_GROUPS):
        wq = _heads_ig(w_q_b[0][:, gi * D_MODEL:(gi + 1) * D_MODEL]).astype(BF16)
        wkv = wkv_b[:, gi * 2 * D_KV:(gi + 1) * 2 * D_KV]
        q, new, newt = _proj_decode(xs, gain(1, 2), g_kv, wq, wkv, tabs_s)
        jobs.append((_block_diag_q(q), _window_minor(b_caches[gi]), new.reshape(nreq, 1, 2 * D_KV), newt))

    dec_o, dec_lse, dec_caches = [], [], None

    def ffn_with_decode(x, l, k, mixer=None):
        nonlocal dec_caches
        req0 = len(dec_o) * (x.shape[0] // ROW_TILE)
        x, o, lse, dec_caches = _ffn_decode(x, gain(l, 4 * k), wgu, wdn, gain(l, 4 * k + 1), l, k,
                                            jobs, req0, dec_caches, mixer)
        dec_o.append(o)
        dec_lse.append(lse)
        return x

    x = x_prompt.reshape(tp, D_MODEL)
    x = ffn_with_decode(x, 0, 0)
    q, kv_a, k2, v2 = _proj(x.reshape(nb, seq, D_MODEL), gain(0, 2), gain(0, 2),
                            wqkv_a, col3(D_MODEL, 0), wqkv_a, col3(2 * D_KV, 2), tabs_p, None,
                            dil=1, win=WIN_A)
    o, = _band_attention(q, k2, v2, bias, sink, has_lse=False)
    x = ffn_with_decode(x, 0, 1, ([o], [], [], None, wo_a, gain(0, 3)))
    x = ffn_with_decode(x, 1, 0).reshape(nb, seq, D_MODEL)
    os_, lses, kv_b = [], [], []
    for gi, (win, dil) in enumerate(B_GROUPS):
        perm = jnp.asarray(perms[dil], BF16) if dil > 1 else None
        q, kvc, k2, v2 = _proj(x, gain(1, 2), g_kv, wq_b, col3(D_MODEL, gi), wkv_b, col2(2 * D_KV, gi),
                               tabs_p, perm, dil=dil, win=win)
        o, lse = _band_attention(q, k2, v2, bias, None, has_lse=True)
        os_.append(o)
        lses.append(lse)
        kv_b.append(kvc.reshape(nb, win, 2, N_KV, HEAD_DIM))
    unperms = [jnp.asarray(perms[d].T, BF16) for _, d in B_GROUPS if d > 1]
    x = _oproj_folded(os_, lses, unperms, _head_expand(), x, wo_b, gain(1, 3))
    y_p = ffn_with_decode(x.reshape(tp, D_MODEL), 1, 1).reshape(nb, seq, D_MODEL)
    a_p = kv_a.reshape(1, nb, WIN_A, 2, N_KV, HEAD_DIM)

    assert len(dec_o) * (tp // ROW_TILE) == nreq
    os_ = [_own_head(jnp.concatenate([o[gi] for o in dec_o], axis=0)) for gi in range(len(B_GROUPS))]
    lses = [_expand_lse(jnp.concatenate([l[gi] for l in dec_lse], axis=0)) for gi in range(len(B_GROUPS))]
    b_s = [_window_major(c) for c in dec_caches]
    xs = _oproj(os_, lses, xs, wo_b, gain(1, 3))
    y_s = ffn(xs, 1, 1).reshape(nreq, 1, D_MODEL)
    a_s = _window_major(cache_a_new)[None]

    return (y_p, y_s, a_p, kv_b[0], kv_b[1], kv_b[2], a_s, b_s[0], b_s[1], b_s[2])
```

```python
import functools

import jax
import jax.numpy as jnp
import numpy as np
from jax.experimental import pallas as pl
from jax.experimental.pallas import tpu as pltpu

F32 = jnp.float32
BF16 = jnp.bfloat16

D_MODEL = 1024
HEAD_DIM = 64
ROT_DIM = HEAD_DIM // 4
ROPE_THETA = 500000.0
N_HEADS = 16
N_KV = 4
D_KV = N_KV * HEAD_DIM
D_FF = 2816
EPS = 1e-6
PAST_LEN = 8192
WIN_A = 128
B_GROUPS = ((128, 1), (512, 4), (2048, 16))

LANES = 128
FF_CHUNK = 512
ROW_TILE = 512
QBLK = 128
NEG = -1e30
LOG2E = 1.4426950408889634
LN2 = 0.6931471805599453
Q_SCALE = HEAD_DIM ** -0.5 * LOG2E
MAX_QBLKS = 8
VMEM_LIMIT = 56 * 1024 * 1024
VMEM_LIMIT_FUSED = 60 * 1024 * 1024
DECODE_BLOCK_BYTES = 4 * 1024 * 1024
SHIFT_ROWS = 128
DECODE_UNROLL = 4


def _params(sem, vmem_limit=VMEM_LIMIT):
    return pltpu.CompilerParams(dimension_semantics=sem, vmem_limit_bytes=vmem_limit)


def _resident(shape):
    nd = len(shape)
    return pl.BlockSpec(shape, lambda *_: (0,) * nd, pipeline_mode=pl.Buffered(1))


def _rms_scale(x):
    return x * jax.lax.rsqrt(jnp.mean(x * x, axis=-1, keepdims=True) + EPS)


def _split_bf16(x, parts):
    out = []
    for _ in range(parts - 1):
        hi = x.astype(BF16)
        out.append(hi)
        x = x - hi.astype(F32)
    out.append(x.astype(BF16))
    return out


def _onehot_dot(sel, x, parts):
    return functools.reduce(
        jnp.add, [jnp.dot(sel, p, preferred_element_type=F32) for p in _split_bf16(x, parts)])


def _ffn_kernel(x_ref, gpre_ref, wg_ref, wu_ref, wd_ref, gpost_ref, o_ref, side_work=(), mix=None):
    x = x_ref[...]
    if mix is not None:
        x = x + mix()
    xn = (_rms_scale(x) * gpre_ref[...]).astype(BF16)
    acc = None
    bounds = list(range(0, D_FF, FF_CHUNK)) + [D_FF]
    n_chunks = len(bounds) - 1
    for c in range(n_chunks):
        sl = slice(bounds[c], bounds[c + 1])
        gate = jnp.dot(xn, wg_ref[:, sl], preferred_element_type=F32)
        up = jnp.dot(xn, wu_ref[:, sl], preferred_element_type=F32)
        act = (gate * jax.nn.sigmoid(gate) * up).astype(BF16)
        part = jnp.dot(act, wd_ref[sl, :], preferred_element_type=F32)
        acc = part if acc is None else acc + part
        for work in side_work[c::n_chunks]:
            work()
    o_ref[...] = x + 0.5 * (_rms_scale(acc) * gpost_ref[...])


def _ffn_specs(t, wg, wu, wd):
    tm = min(ROW_TILE, t)
    row = pl.BlockSpec((tm, D_MODEL), lambda i: (i, 0))
    in_specs = [row, _resident((1, D_MODEL)), _resident(wg.shape), _resident(wu.shape),
                _resident(wd.shape), _resident((1, D_MODEL))]
    return t // tm, in_specs, row


def _ffn(x, gpre, wg, wu, wd, gpost):
    t = x.shape[0]
    steps, in_specs, row = _ffn_specs(t, wg, wu, wd)
    return pl.pallas_call(
        _ffn_kernel,
        grid=(steps,),
        in_specs=in_specs,
        out_specs=row,
        out_shape=jax.ShapeDtypeStruct((t, D_MODEL), F32),
        compiler_params=_params(("parallel",)),
        name="ffn",
    )(x, gpre, wg, wu, wd, gpost)


CAST_CHUNK = 256


def _ffn_cast_kernel(x_ref, gpre_ref, wg_ref, wu_ref, wd_ref, gpost_ref,
                     y_ref, wg_out, wu_out, wd_out, xn_ref, acc_ref):
    c = pl.program_id(0)

    @pl.when(c == 0)
    def _():
        xn_ref[...] = (_rms_scale(x_ref[...]) * gpre_ref[...]).astype(BF16)
        acc_ref[...] = jnp.zeros_like(acc_ref)

    wg, wu, wd = wg_ref[...].astype(BF16), wu_ref[...].astype(BF16), wd_ref[...].astype(BF16)
    wg_out[...], wu_out[...], wd_out[...] = wg, wu, wd
    xn = xn_ref[...]
    gate = jnp.dot(xn, wg, preferred_element_type=F32)
    up = jnp.dot(xn, wu, preferred_element_type=F32)
    act = (gate * jax.nn.sigmoid(gate) * up).astype(BF16)
    acc_ref[...] += jnp.dot(act, wd, preferred_element_type=F32)

    @pl.when(c == pl.num_programs(0) - 1)
    def _():
        y_ref[...] = x_ref[...] + 0.5 * (_rms_scale(acc_ref[...]) * gpost_ref[...])


def _ffn_cast(x, gpre, w_gu, w_dn, gpost, l, k):
    t = x.shape[0]
    n = D_FF // CAST_CHUNK
    full = pl.BlockSpec((t, D_MODEL), lambda c: (0, 0))
    vec = pl.BlockSpec((1, D_MODEL), lambda c: (0, 0))
    return pl.pallas_call(
        _ffn_cast_kernel,
        grid=(n,),
        in_specs=[full, vec,
                  pl.BlockSpec((None, None, D_MODEL, CAST_CHUNK), lambda c: (l, k, 0, c)),
                  pl.BlockSpec((None, None, D_MODEL, CAST_CHUNK), lambda c: (l, k, 0, n + c)),
                  pl.BlockSpec((None, None, CAST_CHUNK, D_MODEL), lambda c: (l, k, c, 0)), vec],
        out_specs=[full,
                   pl.BlockSpec((D_MODEL, CAST_CHUNK), lambda c: (0, c)),
                   pl.BlockSpec((D_MODEL, CAST_CHUNK), lambda c: (0, c)),
                   pl.BlockSpec((CAST_CHUNK, D_MODEL), lambda c: (c, 0))],
        out_shape=[jax.ShapeDtypeStruct((t, D_MODEL), F32),
                   jax.ShapeDtypeStruct((D_MODEL, D_FF), BF16),
                   jax.ShapeDtypeStruct((D_MODEL, D_FF), BF16),
                   jax.ShapeDtypeStruct((D_FF, D_MODEL), BF16)],
        scratch_shapes=[pltpu.VMEM((t, D_MODEL), BF16), pltpu.VMEM((t, D_MODEL), F32)],
        compiler_params=_params(("arbitrary",)),
        name="ffn_cast",
    )(x, gpre, w_gu, w_gu, w_dn, gpost)


def _ffn_decode_kernel(*refs, n_jobs, req0, has_prev, mix_dils):
    per_job = 5 if has_prev else 4
    ffn_in = refs[:6]
    n_mix = 0
    mix = None
    if mix_dils is not None:
        ng = len(mix_dils)
        n_mix = 2 * ng + sum(d > 1 for d in mix_dils) + 3 if ng > 1 else 3
        mix_refs = refs[6:6 + n_mix]
        wo_ref, g_ref = mix_refs[-2:]

        def mix():
            if ng == 1:
                o = mix_refs[0][...]
            else:
                o = _merge_groups(mix_refs[:ng], mix_refs[ng:2 * ng], mix_refs[2 * ng:-3], mix_refs[-3],
                                  ffn_in[0].shape[0], mix_dils)
            return _rms_scale(jnp.dot(o, wo_ref[...], preferred_element_type=F32)) * g_ref[...]

    first = 6 + n_mix
    job_in = [refs[first + j * per_job:first + (j + 1) * per_job] for j in range(n_jobs)]
    outs = refs[first + n_jobs * per_job:]
    req = req0 + pl.program_id(0)
    side_work = []
    for j in range(n_jobs):
        qbd_ref, cache_ref, new_ref, newt_ref = job_in[j][:4]
        o_ref, lse_ref, cout_ref = outs[1 + 3 * j:4 + 3 * j]
        side_work.append(functools.partial(
            _decode_attend, qbd_ref, cache_ref, 0, new_ref, None, o_ref, lse_ref))
        for r0 in range(0, 2 * D_KV, SHIFT_ROWS):
            side_work.append(functools.partial(
                _shift_rows, cache_ref, 0, newt_ref, req, cout_ref, r0, r0 + SHIFT_ROWS))
    _ffn_kernel(*ffn_in, outs[0], side_work=side_work, mix=mix)


def _ffn_decode(x, gpre, wg, wu, wd, gpost, jobs, req0, prev_caches, mixer=None):
    t = x.shape[0]
    steps, in_specs, row = _ffn_specs(t, wg, wu, wd)
    has_prev = prev_caches is not None
    args = [x, gpre, wg, wu, wd, gpost]
    mix_dils = None
    if mixer is not None:
        os_, lses, unperms, expand, wo, g = mixer
        mix_dils = tuple(o.shape[1] for o in os_)
        tiles = os_[0].shape[2] * mix_dils[0] // ROW_TILE
        folded = lambda dil, w: pl.BlockSpec((None, dil, ROW_TILE // dil, w),
                                             lambda i: (i // tiles, 0, i % tiles, 0))
        if len(os_) == 1:
            in_specs += [pl.BlockSpec((None, None, ROW_TILE, D_MODEL), lambda i: (i // tiles, 0, i % tiles, 0))]
            args += [os_[0]]
        else:
            in_specs += ([folded(d, D_MODEL) for d in mix_dils] + [folded(d, LANES) for d in mix_dils]
                         + [_resident(u.shape) for u in unperms] + [_resident(expand.shape)])
            args += [*os_, *lses, *unperms, expand]
        in_specs += [_resident(wo.shape), _resident((1, D_MODEL))]
        args += [wo, g]
    out_specs, out_shape, aliases = [row], [jax.ShapeDtypeStruct((t, D_MODEL), F32)], {}
    for j, (qbd, cache, new, newt) in enumerate(jobs):
        _, feat, win = cache.shape
        req = lambda *tail: pl.BlockSpec((1,) + tail, lambda i: (req0 + i,) + (0,) * len(tail))
        loc = lambda *tail: pl.BlockSpec((1,) + tail, lambda i: (i,) + (0,) * len(tail))
        in_specs += [req(N_HEADS, D_KV), req(feat, win), req(1, 2 * D_KV), _resident(newt.shape)]
        args += [qbd, cache, new, newt]
        if has_prev:
            aliases[len(args)] = 3 + 3 * j
            in_specs.append(pl.BlockSpec(memory_space=pl.ANY))
            args.append(prev_caches[j])
        out_specs += [loc(N_HEADS, D_KV), loc(N_HEADS, LANES), req(feat, win)]
        out_shape += [jax.ShapeDtypeStruct((steps, N_HEADS, D_KV), F32),
                      jax.ShapeDtypeStruct((steps, N_HEADS, LANES), F32),
                      jax.ShapeDtypeStruct(cache.shape, F32)]
    res = pl.pallas_call(
        functools.partial(_ffn_decode_kernel, n_jobs=len(jobs), req0=req0, has_prev=has_prev,
                          mix_dils=mix_dils),
        grid=(steps,),
        in_specs=in_specs, out_specs=out_specs, out_shape=out_shape,
        input_output_aliases=aliases,
        compiler_params=_params(("arbitrary",), VMEM_LIMIT_FUSED),
        name="ffn_decode",
    )(*args)
    return res[0], list(res[1::3]), list(res[2::3]), list(res[3::3])


def _project(x_ref, gq_ref, gk_ref, wq_ref, wkv_ref, rc_ref, rs1_ref, rs2_ref):
    xh = _rms_scale(x_ref[...])
    hq = (xh * gq_ref[...]).astype(BF16)
    hk = (xh * gk_ref[...]).astype(BF16)
    rc, rs1, rs2 = rc_ref[...], rs1_ref[...], rs2_ref[...]

    def rope(t):
        return t * rc + pltpu.roll(t, LANES - ROT_DIM // 2, 1) * rs1 + pltpu.roll(t, ROT_DIM // 2, 1) * rs2

    q = jnp.dot(hq, wq_ref[...], preferred_element_type=F32)
    qs = [(rope(q[:, c * LANES:(c + 1) * LANES]) * Q_SCALE).astype(BF16)
          for c in range(D_MODEL // LANES)]
    kv = jnp.dot(hk, wkv_ref[...], preferred_element_type=F32)
    kvs = []
    for c in range(2 * D_KV // LANES):
        t = kv[:, c * LANES:(c + 1) * LANES]
        kvs.append(rope(t) if c < D_KV // LANES else t)
    return qs, kvs


def _dup_heads(t):
    lo = jax.lax.broadcasted_iota(jnp.int32, t.shape, 1) < HEAD_DIM
    sw = pltpu.roll(t, HEAD_DIM, 1)
    return [jnp.where(lo, t, sw).astype(BF16), jnp.where(lo, sw, t).astype(BF16)]


def _proj_kernel(*refs, dil, cache_rows):
    ins = refs[:8]
    perm_ref = refs[8] if dil > 1 else None
    q_ref, kv_ref, k2_ref, v2_ref = refs[-4:]
    tm = ins[0].shape[0]
    qs, kvs = _project(*ins)
    for c, t in enumerate(kvs):
        kv_ref[:, c * LANES:(c + 1) * LANES] = t[tm - cache_rows:, :]
    nk = D_KV // LANES
    if dil == 1:
        q_ref[0] = jnp.concatenate(qs, axis=1)
        k2_ref[0] = jnp.concatenate([d for t in kvs[:nk] for d in _dup_heads(t)], axis=1)
        v2_ref[0] = jnp.concatenate([d for t in kvs[nk:] for d in _dup_heads(t)], axis=1)
        return
    cols = jnp.concatenate(qs + [t.astype(BF16) for t in kvs], axis=1)
    half = tm // 2
    n = half // dil
    for hh in range(2):
        f = jnp.dot(perm_ref[...], cols[hh * half:(hh + 1) * half, :], preferred_element_type=F32)
        qf = f[:, :D_MODEL].astype(BF16)
        kf = jnp.concatenate([d for c in range(nk)
                              for d in _dup_heads(f[:, D_MODEL + c * LANES:D_MODEL + (c + 1) * LANES])], axis=1)
        vf = jnp.concatenate([d for c in range(nk, 2 * nk)
                              for d in _dup_heads(f[:, D_MODEL + c * LANES:D_MODEL + (c + 1) * LANES])], axis=1)
        for r in range(dil):
            dst = slice(hh * n, (hh + 1) * n)
            q_ref[r, dst, :] = qf[r * n:(r + 1) * n, :]
            k2_ref[r, dst, :] = kf[r * n:(r + 1) * n, :]
            v2_ref[r, dst, :] = vf[r * n:(r + 1) * n, :]


def _proj(x, gq, gk, wq, wq_spec, wkv, wkv_spec, tabs, perm, *, dil, win):
    nb, seq, _ = x.shape
    tm = ROW_TILE
    nt = seq // tm
    cache_rows = min(win, tm)
    first_cache_tile = nt - win // cache_rows
    n = tm // dil
    tab = pl.BlockSpec((tm, LANES), lambda b, i: (i, 0))
    folded = lambda w: pl.BlockSpec((None, dil, n, w), lambda b, i: (b, 0, i, 0))
    in_specs = [pl.BlockSpec((None, tm, D_MODEL), lambda b, i: (b, i, 0)),
                _resident((1, D_MODEL)), _resident((1, D_MODEL)),
                wq_spec, wkv_spec, tab, tab, tab]
    args = [x, gq, gk, wq, wkv, *tabs]
    if dil > 1:
        in_specs.append(_resident(perm.shape))
        args.append(perm)
    return pl.pallas_call(
        functools.partial(_proj_kernel, dil=dil, cache_rows=cache_rows),
        grid=(nb, nt),
        in_specs=in_specs,
        out_specs=[folded(D_MODEL),
                   pl.BlockSpec((None, cache_rows, 2 * D_KV),
                                lambda b, i: (b, jnp.maximum(i - first_cache_tile, 0), 0)),
                   folded(2 * D_KV), folded(2 * D_KV)],
        out_shape=[jax.ShapeDtypeStruct((nb, dil, seq // dil, D_MODEL), BF16),
                   jax.ShapeDtypeStruct((nb, win, 2 * D_KV), F32),
                   jax.ShapeDtypeStruct((nb, dil, seq // dil, 2 * D_KV), BF16),
                   jax.ShapeDtypeStruct((nb, dil, seq // dil, 2 * D_KV), BF16)],
        compiler_params=_params(("parallel", "arbitrary")),
        name="proj",
    )(*args)


def _proj_decode_kernel(*refs):
    q_ref, new_ref, newt_ref = refs[-3:]
    qs, kvs = _project(*refs[:8])
    for c, t in enumerate(qs):
        q_ref[:, c * LANES:(c + 1) * LANES] = t
    for c, t in enumerate(kvs):
        new_ref[:, c * LANES:(c + 1) * LANES] = t
        newt_ref[c * LANES:(c + 1) * LANES, :] = t.T


def _proj_decode(x, gq, gk, wq, wkv, tabs):
    nreq = x.shape[0]
    full = lambda w: pl.BlockSpec((nreq, w), lambda i: (0, 0))
    return pl.pallas_call(
        _proj_decode_kernel,
        grid=(1,),
        in_specs=[full(D_MODEL), _resident((1, D_MODEL)), _resident((1, D_MODEL)),
                  _resident(wq.shape), _resident(wkv.shape), full(LANES), full(LANES), full(LANES)],
        out_specs=[full(D_MODEL), full(2 * D_KV), pl.BlockSpec((2 * D_KV, nreq), lambda i: (0, 0))],
        out_shape=[jax.ShapeDtypeStruct((nreq, D_MODEL), BF16),
                   jax.ShapeDtypeStruct((nreq, 2 * D_KV), F32),
                   jax.ShapeDtypeStruct((2 * D_KV, nreq), F32)],
        compiler_params=_params(("arbitrary",)),
        name="proj_decode",
    )(x, gq, gk, wq, wkv, *tabs)


def _lse_lane(head):
    return (head % 2) * HEAD_DIM + head - head % 2


def _band_kernel(*refs, has_sink, has_lse, nq):
    refs = list(refs)
    sink_ref = refs.pop(0) if has_sink else None
    q_ref, kp_ref, kc_ref, vp_ref, vc_ref, bias_ref, o_ref = refs[:7]
    lse_ref = refs[7] if has_lse else None

    lo = jax.lax.broadcasted_iota(jnp.int32, (2 * QBLK, LANES), 1) < HEAD_DIM
    lane_q = jax.lax.broadcasted_iota(jnp.int32, (QBLK, LANES), 1)
    top = jax.lax.broadcasted_iota(jnp.int32, (2 * QBLK, 1), 0) < QBLK
    zero = jnp.zeros((), BF16)
    krow_lo = jax.lax.broadcasted_iota(jnp.int32, (4 * QBLK, LANES), 0) < 2 * QBLK
    klane_lo = jax.lax.broadcasted_iota(jnp.int32, (4 * QBLK, LANES), 1) < HEAD_DIM
    ones_sel = jnp.where(krow_lo == klane_lo, 1.0, 0.0).astype(BF16)
    first_step = pl.program_id(2) == 0

    for t in range(nq):
        rows = slice(t * QBLK, (t + 1) * QBLK)
        bias = bias_ref[1] if t else bias_ref[jnp.where(first_step, 0, 1)]
        lse_tile = jnp.zeros((QBLK, LANES), F32)
        for h in range(N_KV):
            ksl = slice(h * LANES, (h + 1) * LANES)
            kprev = kc_ref[(t - 1) * QBLK:t * QBLK, ksl] if t else kp_ref[:, ksl]
            vprev = vc_ref[(t - 1) * QBLK:t * QBLK, ksl] if t else vp_ref[:, ksl]
            k2 = jnp.concatenate([kprev, kc_ref[rows, ksl]], axis=0)
            v2 = jnp.concatenate([vprev, vc_ref[rows, ksl]], axis=0)
            kbd = jnp.concatenate([jnp.where(lo, k2, zero), jnp.where(lo, zero, k2)], axis=0)
            vbd = jnp.concatenate([jnp.where(lo, v2, zero), jnp.where(lo, zero, v2)], axis=0)
            vext = jnp.concatenate([vbd, ones_sel], axis=1)
            qq = jnp.concatenate([q_ref[rows, (2 * h) * LANES:(2 * h + 1) * LANES],
                                  q_ref[rows, (2 * h + 1) * LANES:(2 * h + 2) * LANES]], axis=0)
            s = jax.lax.dot_general(qq, kbd, (((1,), (1,)), ((), ())),
                                    preferred_element_type=F32)
            ps, ms, sks = [], [], []
            for c in range(2):
                sc = s[:, c * 2 * QBLK:(c + 1) * 2 * QBLK] + bias
                m = jnp.max(sc, axis=-1, keepdims=True)
                if has_sink:
                    sk = jnp.where(top, sink_ref[4 * h + c], sink_ref[4 * h + 2 + c]) * LOG2E
                    m = jnp.maximum(m, sk)
                    sks.append(sk)
                ps.append(jnp.exp2(sc - m).astype(BF16))
                ms.append(m)
            ov = jnp.dot(jnp.concatenate(ps, axis=1), vext, preferred_element_type=F32)
            l = ov[:, LANES:]
            if has_sink:
                l = l + jnp.where(lo, jnp.exp2(sks[0] - ms[0]), jnp.exp2(sks[1] - ms[1]))
            o = ov[:, :LANES] / l
            o_ref[rows, (2 * h) * LANES:(2 * h + 1) * LANES] = o[:QBLK].astype(o_ref.dtype)
            o_ref[rows, (2 * h + 1) * LANES:(2 * h + 2) * LANES] = o[QBLK:].astype(o_ref.dtype)
            if has_lse:
                lse = jnp.where(lo, ms[0], ms[1]) * LN2 + jnp.log(l)
                for j in range(2):
                    pair = (lane_q == _lse_lane(4 * h + 2 * j)) | (lane_q == _lse_lane(4 * h + 2 * j + 1))
                    lse_tile = jnp.where(pair, lse[j * QBLK:(j + 1) * QBLK], lse_tile)
        if has_lse:
            lse_ref[rows, :] = lse_tile


def _band_attention(q, k2, v2, bias, sink, *, has_lse):
    nb, dil, length, _ = q.shape
    nq = min(MAX_QBLKS, length // QBLK)
    grid = (nb, dil, length // (nq * QBLK))
    has_sink = sink is not None
    cur = lambda w: pl.BlockSpec((None, None, nq * QBLK, w), lambda b, r, i: (b, r, i, 0))
    prev = lambda w: pl.BlockSpec((None, None, QBLK, w),
                                  lambda b, r, i: (b, r, jnp.maximum(i * nq - 1, 0), 0))
    in_specs = [cur(D_MODEL), prev(2 * D_KV), cur(2 * D_KV), prev(2 * D_KV), cur(2 * D_KV),
                _resident(bias.shape)]
    args = [q, k2, k2, v2, v2, bias]
    if has_sink:
        in_specs = [pl.BlockSpec(memory_space=pltpu.SMEM)] + in_specs
        args = [sink] + args
    out_specs = [cur(D_MODEL)]
    out_shape = [jax.ShapeDtypeStruct((nb, dil, length, D_MODEL), BF16)]
    if has_lse:
        out_specs.append(cur(LANES))
        out_shape.append(jax.ShapeDtypeStruct((nb, dil, length, LANES), F32))
    return pl.pallas_call(
        functools.partial(_band_kernel, has_sink=has_sink, has_lse=has_lse, nq=nq),
        grid=grid, in_specs=in_specs, out_specs=out_specs, out_shape=out_shape,
        compiler_params=_params(("parallel", "parallel", "parallel")),
        name="band_attn",
    )(*args)


def _tap_columns(cache_ref, b, r0):
    dil = cache_ref.shape[2] // LANES
    if dil == 1:
        return cache_ref[b, r0:r0 + D_KV, :]
    lane = jax.lax.broadcasted_iota(jnp.int32, (D_KV, LANES), 1)
    is_tap = (lane & (dil - 1)) == 0
    acc = None
    for t in range(dil):
        part = jnp.where(is_tap, cache_ref[b, r0:r0 + D_KV, t * LANES:(t + 1) * LANES], 0.0)
        if t:
            part = pltpu.roll(part, t, 1)
        acc = part if acc is None else acc + part
    return acc


def _decode_attend(qbd_ref, cache_ref, b, new_ref, sink_ref, o_ref, lse_ref):
    qb = qbd_ref[b]
    new = new_ref[b]
    kn = new[:, 0:D_KV]
    vn = new[:, D_KV:2 * D_KV]
    s = jnp.dot(qb.astype(BF16), _tap_columns(cache_ref, b, 0).astype(BF16),
                preferred_element_type=F32)
    sn = jnp.sum(qb * kn, axis=-1, keepdims=True)
    m = jnp.maximum(jnp.max(s, axis=-1, keepdims=True), sn)
    if sink_ref is not None:
        sk = sink_ref[...] * LOG2E
        m = jnp.maximum(m, sk)
    p = jnp.exp2(s - m)
    pn = jnp.exp2(sn - m)
    l = jnp.sum(p, axis=-1, keepdims=True) + pn
    if sink_ref is not None:
        l = l + jnp.exp2(sk - m)
    o = jax.lax.dot_general(p.astype(BF16), _tap_columns(cache_ref, b, D_KV).astype(BF16),
                            (((1,), (1,)), ((), ())), preferred_element_type=F32) + pn * vn
    o_ref[b] = o / l
    lse_ref[b] = jnp.broadcast_to(m * LN2 + jnp.log(l), (N_HEADS, LANES))


def _shift_rows(cache_ref, b, newt_ref, req, cout_ref, r0, r1):
    win = cache_ref.shape[2]
    lane = jax.lax.broadcasted_iota(jnp.int32, (r1 - r0, LANES), 1)
    shifted = pltpu.roll(cache_ref[b, r0:r1, :], win - 1, 1)
    newcol = pltpu.roll(newt_ref[r0:r1, :], LANES - 1 - req, 1)
    cout_ref[b, r0:r1, :] = shifted
    cout_ref[b, r0:r1, win - LANES:win] = jnp.where(lane == LANES - 1, newcol, shifted[:, win - LANES:win])


def _decode_kernel(*refs, has_sink, rb):
    refs = list(refs)
    sink_ref = refs.pop(0) if has_sink else None
    qbd_ref, cache_ref, new_ref, newt_ref, o_ref, lse_ref, cout_ref = refs
    base = pl.program_id(0) * rb

    def body(b, carry):
        _decode_attend(qbd_ref, cache_ref, b, new_ref, sink_ref, o_ref, lse_ref)
        _shift_rows(cache_ref, b, newt_ref, base + b, cout_ref, 0, 2 * D_KV)
        return carry

    jax.lax.fori_loop(0, rb, body, 0, unroll=min(rb, DECODE_UNROLL))


def _decode_attention(qbd, cache, new, newt, sink_rows):
    nreq, feat, win = cache.shape
    rb = max(1, DECODE_BLOCK_BYTES // (feat * win * 4))
    has_sink = sink_rows is not None
    req = lambda *tail: pl.BlockSpec((rb,) + tail, lambda i: (i,) + (0,) * len(tail))
    in_specs = [req(N_HEADS, D_KV), req(feat, win), req(1, 2 * D_KV),
                pl.BlockSpec((feat, nreq), lambda i: (0, 0))]
    args = [qbd, cache, new, newt]
    if has_sink:
        in_specs = [pl.BlockSpec((N_HEADS, 1), lambda i: (0, 0))] + in_specs
        args = [sink_rows] + args
    return pl.pallas_call(
        functools.partial(_decode_kernel, has_sink=has_sink, rb=rb),
        grid=(nreq // rb,), in_specs=in_specs,
        out_specs=[req(N_HEADS, D_KV), req(N_HEADS, LANES), req(feat, win)],
        out_shape=[jax.ShapeDtypeStruct((nreq, N_HEADS, D_KV), F32),
                   jax.ShapeDtypeStruct((nreq, N_HEADS, LANES), F32),
                   jax.ShapeDtypeStruct(cache.shape, F32)],
        compiler_params=_params(("parallel",)),
        name="decode_attn",
    )(*args)


def _oproj_kernel(*refs, n_groups):
    o_refs = refs[:n_groups]
    lse_refs = refs[n_groups:2 * n_groups] if n_groups > 1 else ()
    x_ref, wo_ref, g_ref, out_ref = refs[-4:]
    if n_groups == 1:
        o = o_refs[0][...].astype(BF16)
    else:
        lses = [r[...] for r in lse_refs]
        m = functools.reduce(jnp.maximum, lses)
        es = [jnp.exp(l - m) for l in lses]
        den = functools.reduce(jnp.add, es)
        o = functools.reduce(jnp.add, [(e / den) * r[...].astype(F32)
                                       for e, r in zip(es, o_refs)]).astype(BF16)
    mix = jnp.dot(o, wo_ref[...], preferred_element_type=F32)
    out_ref[...] = x_ref[...] + _rms_scale(mix) * g_ref[...]


def _oproj(os_, lses, x, wo, g):
    t = x.shape[0]
    tm = min(ROW_TILE, t)
    row = pl.BlockSpec((tm, D_MODEL), lambda i: (i, 0))
    n_groups = len(os_)
    return pl.pallas_call(
        functools.partial(_oproj_kernel, n_groups=n_groups),
        grid=(t // tm,),
        in_specs=[row] * (n_groups + len(lses) + 1) + [_resident(wo.shape), _resident((1, D_MODEL))],
        out_specs=row,
        out_shape=jax.ShapeDtypeStruct((t, D_MODEL), F32),
        compiler_params=_params(("parallel",)),
        name="oproj",
    )(*os_, *lses, x, wo, g)


def _merge_groups(o_refs, lse_refs, unperm_refs, expand_ref, tm, dils):
    os_, lses, k = [], [], 0
    for gi, dil in enumerate(dils):
        if dil == 1:
            os_.append(o_refs[gi][0].astype(F32))
            lses.append(lse_refs[gi][0])
            continue
        sel = unperm_refs[k][...]
        k += 1
        n = tm // 2 // dil
        o_halves, lse_halves = [], []
        for hh in range(2):
            rows = slice(hh * n, (hh + 1) * n)
            o = jnp.concatenate([o_refs[gi][r, rows, :] for r in range(dil)], axis=0)
            lse = jnp.concatenate([lse_refs[gi][r, rows, :] for r in range(dil)], axis=0)
            o_halves.append(jnp.dot(sel, o, preferred_element_type=F32))
            lse_halves.append(_onehot_dot(sel, lse, 3))
        os_.append(jnp.concatenate(o_halves, axis=0))
        lses.append(jnp.concatenate(lse_halves, axis=0))
    m = functools.reduce(jnp.maximum, lses)
    es = [jnp.exp(l - m) for l in lses]
    den = functools.reduce(jnp.add, es)
    ws = [jnp.dot((e / den).astype(BF16), expand_ref[...], preferred_element_type=F32) for e in es]
    return functools.reduce(jnp.add, [w * o for w, o in zip(ws, os_)]).astype(BF16)


def _oproj_folded_kernel(*refs, dils):
    ng = len(dils)
    o_refs, lse_refs = refs[:ng], refs[ng:2 * ng]
    unperm_refs = refs[2 * ng:2 * ng + sum(d > 1 for d in dils)]
    expand_ref, x_ref, wo_ref, g_ref, out_ref = refs[-5:]
    o = _merge_groups(o_refs, lse_refs, unperm_refs, expand_ref, x_ref.shape[0], dils)
    mix = jnp.dot(o, wo_ref[...], preferred_element_type=F32)
    out_ref[...] = x_ref[...] + _rms_scale(mix) * g_ref[...]


def _oproj_folded(os_, lses, unperms, expand, x, wo, g):
    nb, seq, _ = x.shape
    tm = ROW_TILE
    dils = tuple(o.shape[1] for o in os_)
    folded = lambda dil, w: pl.BlockSpec((None, dil, tm // dil, w), lambda b, i: (b, 0, i, 0))
    row = pl.BlockSpec((None, tm, D_MODEL), lambda b, i: (b, i, 0))
    in_specs = ([folded(d, D_MODEL) for d in dils] + [folded(d, LANES) for d in dils]
                + [_resident(u.shape) for u in unperms]
                + [_resident(expand.shape), row, _resident(wo.shape), _resident((1, D_MODEL))])
    return pl.pallas_call(
        functools.partial(_oproj_folded_kernel, dils=dils),
        grid=(nb, seq // tm),
        in_specs=in_specs,
        out_specs=row,
        out_shape=jax.ShapeDtypeStruct((nb, seq, D_MODEL), F32),
        compiler_params=_params(("parallel", "parallel")),
        name="oproj_folded",
    )(*os_, *lses, *unperms, expand, x, wo, g)


def _rope_tabs(pos):
    half = ROT_DIM // 2
    pos = np.asarray(pos, np.float32)
    inv = np.float32(ROPE_THETA) ** (-np.arange(0, ROT_DIM, 2, dtype=np.float32) / np.float32(ROT_DIM))
    ang = pos[:, None] * inv[None, :]
    cos, sin = np.cos(ang), np.sin(ang)
    n = pos.shape[0]
    pad = np.zeros((n, HEAD_DIM - ROT_DIM), np.float32)
    zer = np.zeros((n, half), np.float32)
    c = np.concatenate([cos, cos, pad + 1.0], axis=1)
    s1 = np.concatenate([-sin, zer, pad], axis=1)
    s2 = np.concatenate([zer, sin, pad], axis=1)
    return tuple(jnp.asarray(np.tile(t, (1, LANES // HEAD_DIM)), F32) for t in (c, s1, s2))


def _band_bias():
    i = np.arange(QBLK)[:, None]
    j = np.arange(QBLK)[None, :]
    prev_ok = j >= i
    cur_ok = j <= i
    later = np.concatenate([prev_ok, cur_ok], axis=1)
    first = np.concatenate([np.zeros_like(prev_ok), cur_ok], axis=1)
    b = np.where(np.stack([first, later]), 0.0, NEG).astype(np.float32)
    return jnp.asarray(np.tile(b, (1, 2, 1)))


def _fold_perm(dil):
    half = ROW_TILE // 2
    n = half // dil
    dst = np.arange(half)
    src = (dst % n) * dil + dst // n
    return (src[:, None] == np.arange(half)[None, :]).astype(np.float32)


def _head_expand():
    sel = np.arange(LANES)[:, None] == _lse_lane(np.arange(D_MODEL)[None, :] // HEAD_DIM)
    return jnp.asarray(sel.astype(np.float32), BF16)


def _heads_ig(w):
    k = w.shape[0]
    return w.reshape(k, N_KV, N_HEADS // N_KV, HEAD_DIM).transpose(0, 2, 1, 3).reshape(k, D_MODEL)


def _block_diag_q(q):
    r = q.shape[0]
    q4 = q.astype(F32).reshape(r, 4, 1, N_KV, HEAD_DIM)
    eye = jnp.eye(N_KV, dtype=F32)[None, None, :, :, None]
    return (q4 * eye).reshape(r, N_HEADS, D_KV)


def _own_head(o):
    r = o.shape[0]
    o5 = o.reshape(r, 4, N_KV, N_KV, HEAD_DIM)
    d = jnp.einsum('rigkd,gk->rigd', o5, jnp.eye(N_KV, dtype=o.dtype))
    return d.transpose(0, 2, 1, 3).reshape(r, D_MODEL)


def _expand_lse(lse):
    r = lse.shape[0]
    l = lse[:, :, 0].reshape(r, 4, N_KV).transpose(0, 2, 1).reshape(r, N_HEADS)
    return jnp.repeat(l, HEAD_DIM, axis=1)


def _window_minor(cache):
    r, w = cache.shape[:2]
    return cache.transpose(0, 2, 3, 4, 1).reshape(r, 2 * D_KV, w)


def _window_major(cache_t):
    r, _, w = cache_t.shape
    return cache_t.reshape(r, 2, N_KV, HEAD_DIM, w).transpose(0, 4, 1, 2, 3)


def kernel(x_prompt, x_sample, cache_a_kv, cache_b_kv_w128, cache_b_kv_w512, cache_b_kv_w2048,
           norm_g, w_ffn_gu, w_ffn_dn, w_qkv_a, sink_a, w_o_a, g_kv_b, w_kv_b, w_q_b, w_o_b):
    nb, seq, _ = x_prompt.shape
    nreq = x_sample.shape[0]
    tp = nb * seq
    b_caches = (cache_b_kv_w128, cache_b_kv_w512, cache_b_kv_w2048)

    def gain(l, k):
        return norm_g[l, k].reshape(1, D_MODEL)

    ffn_w = {(1, 1): (w_ffn_gu[1, 1, :, :D_FF].astype(BF16), w_ffn_gu[1, 1, :, D_FF:].astype(BF16),
                      w_ffn_dn[1, 1].astype(BF16))}

    def ffn(x, l, k):
        if (l, k) in ffn_w:
            return _ffn(x, gain(l, 4 * k), *ffn_w[l, k], gain(l, 4 * k + 1))
        y, *ffn_w[l, k] = _ffn_cast(x, gain(l, 4 * k), w_ffn_gu, w_ffn_dn, gain(l, 4 * k + 1), l, k)
        return y

    bias = _band_bias()
    tabs_p = _rope_tabs(np.arange(seq))
    tabs_s = _rope_tabs(np.full((nreq,), PAST_LEN))
    perms = {dil: _fold_perm(dil) for _, dil in B_GROUPS if dil > 1}

    wqkv_a = w_qkv_a.astype(BF16)
    wq_b = w_q_b.astype(BF16)
    wkv_b = w_kv_b.astype(BF16)
    col3 = lambda w, j: pl.BlockSpec((None, D_MODEL, w), lambda *_: (0, 0, j), pipeline_mode=pl.Buffered(1))
    col2 = lambda w, j: pl.BlockSpec((D_MODEL, w), lambda *_: (0, j), pipeline_mode=pl.Buffered(1))
    wq_a = w_qkv_a[0][:, :D_MODEL]
    wkv_a = wqkv_a[0][:, D_MODEL:]
    wo_a = w_o_a[0].astype(BF16)
    wo_b = w_o_b[0].astype(BF16)
    g_kv = g_kv_b.reshape(1, D_MODEL)
    sink = sink_a[0].astype(F32)
    sink_rows = sink.reshape(N_KV, 4).T.reshape(N_HEADS, 1)

    assert all(win == dil * LANES for win, dil in B_GROUPS + ((WIN_A, 1),))
    xs = x_sample.reshape(nreq, D_MODEL)
    xs = ffn(xs, 0, 0)
    q, new, newt = _proj_decode(xs, gain(0, 2), gain(0, 2), _heads_ig(wq_a).astype(BF16), wkv_a, tabs_s)
    o, _, cache_a_new = _decode_attention(_block_diag_q(q), _window_minor(cache_a_kv[0]),
                                          new.reshape(nreq, 1, 2 * D_KV), newt, sink_rows)
    xs = _oproj([_own_head(o)], [], xs, wo_a, gain(0, 3))
    xs = ffn(xs, 0, 1)
    xs = ffn(xs, 1, 0)
    jobs = []
    for gi, (win, dil) in enumerate(B_GROUPS):
        wq = _heads_ig(w_q_b[0][:, gi * D_MODEL:(gi + 1) * D_MODEL]).astype(BF16)
        wkv = wkv_b[:, gi * 2 * D_KV:(gi + 1) * 2 * D_KV]
        q, new, newt = _proj_decode(xs, gain(1, 2), g_kv, wq, wkv, tabs_s)
        jobs.append((_block_diag_q(q), _window_minor(b_caches[gi]), new.reshape(nreq, 1, 2 * D_KV), newt))

    dec_o, dec_lse, dec_caches = [], [], None

    def ffn_with_decode(x, l, k, mixer=None):
        nonlocal dec_caches
        req0 = len(dec_o) * (x.shape[0] // ROW_TILE)
        x, o, lse, dec_caches = _ffn_decode(x, gain(l, 4 * k), *ffn_w[l, k], gain(l, 4 * k + 1),
                                            jobs, req0, dec_caches, mixer)
        dec_o.append(o)
        dec_lse.append(lse)
        return x

    x = x_prompt.reshape(tp, D_MODEL)
    x = ffn_with_decode(x, 0, 0)
    q, kv_a, k2, v2 = _proj(x.reshape(nb, seq, D_MODEL), gain(0, 2), gain(0, 2),
                            wqkv_a, col3(D_MODEL, 0), wqkv_a, col3(2 * D_KV, 2), tabs_p, None,
                            dil=1, win=WIN_A)
    o, = _band_attention(q, k2, v2, bias, sink, has_lse=False)
    x = ffn_with_decode(x, 0, 1, ([o], [], [], None, wo_a, gain(0, 3)))
    x = ffn_with_decode(x, 1, 0).reshape(nb, seq, D_MODEL)
    os_, lses, kv_b = [], [], []
    for gi, (win, dil) in enumerate(B_GROUPS):
        perm = jnp.asarray(perms[dil], BF16) if dil > 1 else None
        q, kvc, k2, v2 = _proj(x, gain(1, 2), g_kv, wq_b, col3(D_MODEL, gi), wkv_b, col2(2 * D_KV, gi),
                               tabs_p, perm, dil=dil, win=win)
        o, lse = _band_attention(q, k2, v2, bias, None, has_lse=True)
        os_.append(o)
        lses.append(lse)
        kv_b.append(kvc.reshape(nb, win, 2, N_KV, HEAD_DIM))
    unperms = [jnp.asarray(perms[d].T, BF16) for _, d in B_GROUPS if d > 1]
    x = _oproj_folded(os_, lses, unperms, _head_expand(), x, wo_b, gain(1, 3))
    y_p = ffn_with_decode(x.reshape(tp, D_MODEL), 1, 1).reshape(nb, seq, D_MODEL)
    a_p = kv_a.reshape(1, nb, WIN_A, 2, N_KV, HEAD_DIM)

    assert len(dec_o) * (tp // ROW_TILE) == nreq
    os_ = [_own_head(jnp.concatenate([o[gi] for o in dec_o], axis=0)) for gi in range(len(B_GROUPS))]
    lses = [_expand_lse(jnp.concatenate([l[gi] for l in dec_lse], axis=0)) for gi in range(len(B_GROUPS))]
    b_s = [_window_major(c) for c in dec_caches]
    xs = _oproj(os_, lses, xs, wo_b, gain(1, 3))
    y_s = ffn(xs, 1, 1).reshape(nreq, 1, D_MODEL)
    a_s = _window_major(cache_a_new)[None]

    return (y_p, y_s, a_p, kv_b[0], kv_b[1], kv_b[2], a_s, b_s[0], b_s[1], b_s[2])
```

```python
import functools

import jax
import jax.numpy as jnp
import numpy as np
from jax.experimental import pallas as pl
from jax.experimental.pallas import tpu as pltpu

F32 = jnp.float32
BF16 = jnp.bfloat16

D_MODEL = 1024
HEAD_DIM = 64
ROT_DIM = HEAD_DIM // 4
ROPE_THETA = 500000.0
N_HEADS = 16
N_KV = 4
D_KV = N_KV * HEAD_DIM
D_FF = 2816
EPS = 1e-6
PAST_LEN = 8192
WIN_A = 128
B_GROUPS = ((128, 1), (512, 4), (2048, 16))

LANES = 128
FF_CHUNK = 512
ROW_TILE = 512
QBLK = 128
NEG = -1e30
LOG2E = 1.4426950408889634
LN2 = 0.6931471805599453
Q_SCALE = HEAD_DIM ** -0.5 * LOG2E
MAX_QBLKS = 8
VMEM_LIMIT = 56 * 1024 * 1024
VMEM_LIMIT_FUSED = 60 * 1024 * 1024
DECODE_BLOCK_BYTES = 4 * 1024 * 1024
SHIFT_ROWS = 128
DECODE_UNROLL = 4


def _params(sem, vmem_limit=VMEM_LIMIT):
    return pltpu.CompilerParams(dimension_semantics=sem, vmem_limit_bytes=vmem_limit)


def _resident(shape):
    nd = len(shape)
    return pl.BlockSpec(shape, lambda *_: (0,) * nd, pipeline_mode=pl.Buffered(1))


def _rms_scale(x):
    return x * jax.lax.rsqrt(jnp.mean(x * x, axis=-1, keepdims=True) + EPS)


def _split_bf16(x, parts):
    out = []
    for _ in range(parts - 1):
        hi = x.astype(BF16)
        out.append(hi)
        x = x - hi.astype(F32)
    out.append(x.astype(BF16))
    return out


def _onehot_dot(sel, x, parts):
    return functools.reduce(
        jnp.add, [jnp.dot(sel, p, preferred_element_type=F32) for p in _split_bf16(x, parts)])


def _ffn_kernel(x_ref, gpre_ref, wg_ref, wu_ref, wd_ref, gpost_ref, o_ref, side_work=(), mix=None):
    x = x_ref[...]
    if mix is not None:
        x = x + mix()
    xn = (_rms_scale(x) * gpre_ref[...]).astype(BF16)
    acc = None
    bounds = list(range(0, D_FF, FF_CHUNK)) + [D_FF]
    n_chunks = len(bounds) - 1
    for c in range(n_chunks):
        sl = slice(bounds[c], bounds[c + 1])
        gate = jnp.dot(xn, wg_ref[:, sl], preferred_element_type=F32)
        up = jnp.dot(xn, wu_ref[:, sl], preferred_element_type=F32)
        act = (gate * jax.nn.sigmoid(gate) * up).astype(BF16)
        part = jnp.dot(act, wd_ref[sl, :], preferred_element_type=F32)
        acc = part if acc is None else acc + part
        for work in side_work[c::n_chunks]:
            work()
    o_ref[...] = x + 0.5 * (_rms_scale(acc) * gpost_ref[...])


def _ffn_specs(t, wg, wu, wd):
    tm = min(ROW_TILE, t)
    row = pl.BlockSpec((tm, D_MODEL), lambda i: (i, 0))
    in_specs = [row, _resident((1, D_MODEL)), _resident(wg.shape), _resident(wu.shape),
                _resident(wd.shape), _resident((1, D_MODEL))]
    return t // tm, in_specs, row


def _ffn(x, gpre, wg, wu, wd, gpost):
    t = x.shape[0]
    steps, in_specs, row = _ffn_specs(t, wg, wu, wd)
    return pl.pallas_call(
        _ffn_kernel,
        grid=(steps,),
        in_specs=in_specs,
        out_specs=row,
        out_shape=jax.ShapeDtypeStruct((t, D_MODEL), F32),
        compiler_params=_params(("parallel",)),
        name="ffn",
    )(x, gpre, wg, wu, wd, gpost)


CAST_CHUNK = 256


def _ffn_cast_kernel(x_ref, gpre_ref, wg_ref, wu_ref, wd_ref, gpost_ref,
                     y_ref, wg_out, wu_out, wd_out, xn_ref, acc_ref):
    c = pl.program_id(0)

    @pl.when(c == 0)
    def _():
        xn_ref[...] = (_rms_scale(x_ref[...]) * gpre_ref[...]).astype(BF16)
        acc_ref[...] = jnp.zeros_like(acc_ref)

    wg, wu, wd = wg_ref[...].astype(BF16), wu_ref[...].astype(BF16), wd_ref[...].astype(BF16)
    wg_out[...], wu_out[...], wd_out[...] = wg, wu, wd
    xn = xn_ref[...]
    gate = jnp.dot(xn, wg, preferred_element_type=F32)
    up = jnp.dot(xn, wu, preferred_element_type=F32)
    act = (gate * jax.nn.sigmoid(gate) * up).astype(BF16)
    acc_ref[...] += jnp.dot(act, wd, preferred_element_type=F32)

    @pl.when(c == pl.num_programs(0) - 1)
    def _():
        y_ref[...] = x_ref[...] + 0.5 * (_rms_scale(acc_ref[...]) * gpost_ref[...])


def _cast_kernel(wg_ref, wu_ref, wd_ref, wg_out, wu_out, wd_out):
    wg_out[...] = wg_ref[...].astype(BF16)
    wu_out[...] = wu_ref[...].astype(BF16)
    wd_out[...] = wd_ref[...].astype(BF16)


def _cast_specs(l, k):
    n = D_FF // CAST_CHUNK
    in_specs = [pl.BlockSpec((None, None, D_MODEL, CAST_CHUNK), lambda c: (l, k, 0, c)),
                pl.BlockSpec((None, None, D_MODEL, CAST_CHUNK), lambda c: (l, k, 0, n + c)),
                pl.BlockSpec((None, None, CAST_CHUNK, D_MODEL), lambda c: (l, k, c, 0))]
    out_specs = [pl.BlockSpec((D_MODEL, CAST_CHUNK), lambda c: (0, c)),
                 pl.BlockSpec((D_MODEL, CAST_CHUNK), lambda c: (0, c)),
                 pl.BlockSpec((CAST_CHUNK, D_MODEL), lambda c: (c, 0))]
    out_shape = [jax.ShapeDtypeStruct((D_MODEL, D_FF), BF16),
                 jax.ShapeDtypeStruct((D_MODEL, D_FF), BF16),
                 jax.ShapeDtypeStruct((D_FF, D_MODEL), BF16)]
    return n, in_specs, out_specs, out_shape


def _cast_weights(w_gu, w_dn, l, k):
    n, in_specs, out_specs, out_shape = _cast_specs(l, k)
    return pl.pallas_call(
        _cast_kernel, grid=(n,), in_specs=in_specs, out_specs=out_specs, out_shape=out_shape,
        compiler_params=_params(("parallel",)), name="cast_weights",
    )(w_gu, w_gu, w_dn)


def _ffn_cast(x, gpre, w_gu, w_dn, gpost, l, k):
    t = x.shape[0]
    n, w_in, w_out, w_shape = _cast_specs(l, k)
    full = pl.BlockSpec((t, D_MODEL), lambda c: (0, 0))
    vec = pl.BlockSpec((1, D_MODEL), lambda c: (0, 0))
    return pl.pallas_call(
        _ffn_cast_kernel,
        grid=(n,),
        in_specs=[full, vec, *w_in, vec],
        out_specs=[full, *w_out],
        out_shape=[jax.ShapeDtypeStruct((t, D_MODEL), F32), *w_shape],
        scratch_shapes=[pltpu.VMEM((t, D_MODEL), BF16), pltpu.VMEM((t, D_MODEL), F32)],
        compiler_params=_params(("arbitrary",)),
        name="ffn_cast",
    )(x, gpre, w_gu, w_gu, w_dn, gpost)


def _ffn_decode_kernel(*refs, n_jobs, req0, has_prev, mix_dils):
    per_job = 5 if has_prev else 4
    ffn_in = refs[:6]
    n_mix = 0
    mix = None
    if mix_dils is not None:
        ng = len(mix_dils)
        n_mix = 2 * ng + sum(d > 1 for d in mix_dils) + 3 if ng > 1 else 3
        mix_refs = refs[6:6 + n_mix]
        wo_ref, g_ref = mix_refs[-2:]

        def mix():
            if ng == 1:
                o = mix_refs[0][...]
            else:
                o = _merge_groups(mix_refs[:ng], mix_refs[ng:2 * ng], mix_refs[2 * ng:-3], mix_refs[-3],
                                  ffn_in[0].shape[0], mix_dils)
            return _rms_scale(jnp.dot(o, wo_ref[...], preferred_element_type=F32)) * g_ref[...]

    first = 6 + n_mix
    job_in = [refs[first + j * per_job:first + (j + 1) * per_job] for j in range(n_jobs)]
    outs = refs[first + n_jobs * per_job:]
    req = req0 + pl.program_id(0)
    side_work = []
    for j in range(n_jobs):
        qbd_ref, cache_ref, new_ref, newt_ref = job_in[j][:4]
        o_ref, lse_ref, cout_ref = outs[1 + 3 * j:4 + 3 * j]
        side_work.append(functools.partial(
            _decode_attend, qbd_ref, cache_ref, 0, new_ref, None, o_ref, lse_ref))
        for r0 in range(0, 2 * D_KV, SHIFT_ROWS):
            side_work.append(functools.partial(
                _shift_rows, cache_ref, 0, newt_ref, req, cout_ref, r0, r0 + SHIFT_ROWS))
    _ffn_kernel(*ffn_in, outs[0], side_work=side_work, mix=mix)


def _ffn_decode(x, gpre, wg, wu, wd, gpost, jobs, req0, prev_caches, mixer=None):
    t = x.shape[0]
    steps, in_specs, row = _ffn_specs(t, wg, wu, wd)
    has_prev = prev_caches is not None
    args = [x, gpre, wg, wu, wd, gpost]
    mix_dils = None
    if mixer is not None:
        os_, lses, unperms, expand, wo, g = mixer
        mix_dils = tuple(o.shape[1] for o in os_)
        tiles = os_[0].shape[2] * mix_dils[0] // ROW_TILE
        folded = lambda dil, w: pl.BlockSpec((None, dil, ROW_TILE // dil, w),
                                             lambda i: (i // tiles, 0, i % tiles, 0))
        if len(os_) == 1:
            in_specs += [pl.BlockSpec((None, None, ROW_TILE, D_MODEL), lambda i: (i // tiles, 0, i % tiles, 0))]
            args += [os_[0]]
        else:
            in_specs += ([folded(d, D_MODEL) for d in mix_dils] + [folded(d, LANES) for d in mix_dils]
                         + [_resident(u.shape) for u in unperms] + [_resident(expand.shape)])
            args += [*os_, *lses, *unperms, expand]
        in_specs += [_resident(wo.shape), _resident((1, D_MODEL))]
        args += [wo, g]
    out_specs, out_shape, aliases = [row], [jax.ShapeDtypeStruct((t, D_MODEL), F32)], {}
    for j, (qbd, cache, new, newt) in enumerate(jobs):
        _, feat, win = cache.shape
        req = lambda *tail: pl.BlockSpec((1,) + tail, lambda i: (req0 + i,) + (0,) * len(tail))
        loc = lambda *tail: pl.BlockSpec((1,) + tail, lambda i: (i,) + (0,) * len(tail))
        in_specs += [req(N_HEADS, D_KV), req(feat, win), req(1, 2 * D_KV), _resident(newt.shape)]
        args += [qbd, cache, new, newt]
        if has_prev:
            aliases[len(args)] = 3 + 3 * j
            in_specs.append(pl.BlockSpec(memory_space=pl.ANY))
            args.append(prev_caches[j])
        out_specs += [loc(N_HEADS, D_KV), loc(N_HEADS, LANES), req(feat, win)]
        out_shape += [jax.ShapeDtypeStruct((steps, N_HEADS, D_KV), F32),
                      jax.ShapeDtypeStruct((steps, N_HEADS, LANES), F32),
                      jax.ShapeDtypeStruct(cache.shape, F32)]
    res = pl.pallas_call(
        functools.partial(_ffn_decode_kernel, n_jobs=len(jobs), req0=req0, has_prev=has_prev,
                          mix_dils=mix_dils),
        grid=(steps,),
        in_specs=in_specs, out_specs=out_specs, out_shape=out_shape,
        input_output_aliases=aliases,
        compiler_params=_params(("arbitrary",), VMEM_LIMIT_FUSED),
        name="ffn_decode",
    )(*args)
    return res[0], list(res[1::3]), list(res[2::3]), list(res[3::3])


def _project(x_ref, gq_ref, gk_ref, wq_ref, wkv_ref, rc_ref, rs1_ref, rs2_ref):
    xh = _rms_scale(x_ref[...])
    hq = (xh * gq_ref[...]).astype(BF16)
    hk = (xh * gk_ref[...]).astype(BF16)
    rc, rs1, rs2 = rc_ref[...], rs1_ref[...], rs2_ref[...]

    def rope(t):
        return t * rc + pltpu.roll(t, LANES - ROT_DIM // 2, 1) * rs1 + pltpu.roll(t, ROT_DIM // 2, 1) * rs2

    q = jnp.dot(hq, wq_ref[...], preferred_element_type=F32)
    qs = [(rope(q[:, c * LANES:(c + 1) * LANES]) * Q_SCALE).astype(BF16)
          for c in range(D_MODEL // LANES)]
    kv = jnp.dot(hk, wkv_ref[...], preferred_element_type=F32)
    kvs = []
    for c in range(2 * D_KV // LANES):
        t = kv[:, c * LANES:(c + 1) * LANES]
        kvs.append(rope(t) if c < D_KV // LANES else t)
    return qs, kvs


def _dup_heads(t):
    lo = jax.lax.broadcasted_iota(jnp.int32, t.shape, 1) < HEAD_DIM
    sw = pltpu.roll(t, HEAD_DIM, 1)
    return [jnp.where(lo, t, sw).astype(BF16), jnp.where(lo, sw, t).astype(BF16)]


def _proj_kernel(*refs, dil, cache_rows):
    ins = refs[:8]
    perm_ref = refs[8] if dil > 1 else None
    q_ref, kv_ref, k2_ref, v2_ref = refs[-4:]
    tm = ins[0].shape[0]
    qs, kvs = _project(*ins)
    for c, t in enumerate(kvs):
        kv_ref[:, c * LANES:(c + 1) * LANES] = t[tm - cache_rows:, :]
    nk = D_KV // LANES
    if dil == 1:
        q_ref[0] = jnp.concatenate(qs, axis=1)
        k2_ref[0] = jnp.concatenate([d for t in kvs[:nk] for d in _dup_heads(t)], axis=1)
        v2_ref[0] = jnp.concatenate([d for t in kvs[nk:] for d in _dup_heads(t)], axis=1)
        return
    cols = jnp.concatenate(qs + [t.astype(BF16) for t in kvs], axis=1)
    half = tm // 2
    n = half // dil
    for hh in range(2):
        f = jnp.dot(perm_ref[...], cols[hh * half:(hh + 1) * half, :], preferred_element_type=F32)
        qf = f[:, :D_MODEL].astype(BF16)
        kf = jnp.concatenate([d for c in range(nk)
                              for d in _dup_heads(f[:, D_MODEL + c * LANES:D_MODEL + (c + 1) * LANES])], axis=1)
        vf = jnp.concatenate([d for c in range(nk, 2 * nk)
                              for d in _dup_heads(f[:, D_MODEL + c * LANES:D_MODEL + (c + 1) * LANES])], axis=1)
        for r in range(dil):
            dst = slice(hh * n, (hh + 1) * n)
            q_ref[r, dst, :] = qf[r * n:(r + 1) * n, :]
            k2_ref[r, dst, :] = kf[r * n:(r + 1) * n, :]
            v2_ref[r, dst, :] = vf[r * n:(r + 1) * n, :]


def _proj(x, gq, gk, wq, wq_spec, wkv, wkv_spec, tabs, perm, *, dil, win):
    nb, seq, _ = x.shape
    tm = ROW_TILE
    nt = seq // tm
    cache_rows = min(win, tm)
    first_cache_tile = nt - win // cache_rows
    n = tm // dil
    tab = pl.BlockSpec((tm, LANES), lambda b, i: (i, 0))
    folded = lambda w: pl.BlockSpec((None, dil, n, w), lambda b, i: (b, 0, i, 0))
    in_specs = [pl.BlockSpec((None, tm, D_MODEL), lambda b, i: (b, i, 0)),
                _resident((1, D_MODEL)), _resident((1, D_MODEL)),
                wq_spec, wkv_spec, tab, tab, tab]
    args = [x, gq, gk, wq, wkv, *tabs]
    if dil > 1:
        in_specs.append(_resident(perm.shape))
        args.append(perm)
    return pl.pallas_call(
        functools.partial(_proj_kernel, dil=dil, cache_rows=cache_rows),
        grid=(nb, nt),
        in_specs=in_specs,
        out_specs=[folded(D_MODEL),
                   pl.BlockSpec((None, cache_rows, 2 * D_KV),
                                lambda b, i: (b, jnp.maximum(i - first_cache_tile, 0), 0)),
                   folded(2 * D_KV), folded(2 * D_KV)],
        out_shape=[jax.ShapeDtypeStruct((nb, dil, seq // dil, D_MODEL), BF16),
                   jax.ShapeDtypeStruct((nb, win, 2 * D_KV), F32),
                   jax.ShapeDtypeStruct((nb, dil, seq // dil, 2 * D_KV), BF16),
                   jax.ShapeDtypeStruct((nb, dil, seq // dil, 2 * D_KV), BF16)],
        compiler_params=_params(("parallel", "arbitrary")),
        name="proj",
    )(*args)


def _proj_decode_kernel(*refs):
    q_ref, new_ref, newt_ref = refs[-3:]
    qs, kvs = _project(*refs[:8])
    for c, t in enumerate(qs):
        q_ref[:, c * LANES:(c + 1) * LANES] = t
    for c, t in enumerate(kvs):
        new_ref[:, c * LANES:(c + 1) * LANES] = t
        newt_ref[c * LANES:(c + 1) * LANES, :] = t.T


def _proj_decode(x, gq, gk, wq, wkv, tabs):
    nreq = x.shape[0]
    full = lambda w: pl.BlockSpec((nreq, w), lambda i: (0, 0))
    return pl.pallas_call(
        _proj_decode_kernel,
        grid=(1,),
        in_specs=[full(D_MODEL), _resident((1, D_MODEL)), _resident((1, D_MODEL)),
                  _resident(wq.shape), _resident(wkv.shape), full(LANES), full(LANES), full(LANES)],
        out_specs=[full(D_MODEL), full(2 * D_KV), pl.BlockSpec((2 * D_KV, nreq), lambda i: (0, 0))],
        out_shape=[jax.ShapeDtypeStruct((nreq, D_MODEL), BF16),
                   jax.ShapeDtypeStruct((nreq, 2 * D_KV), F32),
                   jax.ShapeDtypeStruct((2 * D_KV, nreq), F32)],
        compiler_params=_params(("arbitrary",)),
        name="proj_decode",
    )(x, gq, gk, wq, wkv, *tabs)


def _lse_lane(head):
    return (head % 2) * HEAD_DIM + head - head % 2


def _band_kernel(*refs, has_sink, has_lse, nq):
    refs = list(refs)
    sink_ref = refs.pop(0) if has_sink else None
    q_ref, kp_ref, kc_ref, vp_ref, vc_ref, bias_ref, o_ref = refs[:7]
    lse_ref = refs[7] if has_lse else None

    lo = jax.lax.broadcasted_iota(jnp.int32, (2 * QBLK, LANES), 1) < HEAD_DIM
    lane_q = jax.lax.broadcasted_iota(jnp.int32, (QBLK, LANES), 1)
    top = jax.lax.broadcasted_iota(jnp.int32, (2 * QBLK, 1), 0) < QBLK
    zero = jnp.zeros((), BF16)
    krow_lo = jax.lax.broadcasted_iota(jnp.int32, (4 * QBLK, LANES), 0) < 2 * QBLK
    klane_lo = jax.lax.broadcasted_iota(jnp.int32, (4 * QBLK, LANES), 1) < HEAD_DIM
    ones_sel = jnp.where(krow_lo == klane_lo, 1.0, 0.0).astype(BF16)
    first_step = pl.program_id(2) == 0

    for t in range(nq):
        rows = slice(t * QBLK, (t + 1) * QBLK)
        bias = bias_ref[1] if t else bias_ref[jnp.where(first_step, 0, 1)]
        lse_tile = jnp.zeros((QBLK, LANES), F32)
        for h in range(N_KV):
            ksl = slice(h * LANES, (h + 1) * LANES)
            kprev = kc_ref[(t - 1) * QBLK:t * QBLK, ksl] if t else kp_ref[:, ksl]
            vprev = vc_ref[(t - 1) * QBLK:t * QBLK, ksl] if t else vp_ref[:, ksl]
            k2 = jnp.concatenate([kprev, kc_ref[rows, ksl]], axis=0)
            v2 = jnp.concatenate([vprev, vc_ref[rows, ksl]], axis=0)
            kbd = jnp.concatenate([jnp.where(lo, k2, zero), jnp.where(lo, zero, k2)], axis=0)
            vbd = jnp.concatenate([jnp.where(lo, v2, zero), jnp.where(lo, zero, v2)], axis=0)
            vext = jnp.concatenate([vbd, ones_sel], axis=1)
            qq = jnp.concatenate([q_ref[rows, (2 * h) * LANES:(2 * h + 1) * LANES],
                                  q_ref[rows, (2 * h + 1) * LANES:(2 * h + 2) * LANES]], axis=0)
            s = jax.lax.dot_general(qq, kbd, (((1,), (1,)), ((), ())),
                                    preferred_element_type=F32)
            ps, ms, sks = [], [], []
            for c in range(2):
                sc = s[:, c * 2 * QBLK:(c + 1) * 2 * QBLK] + bias
                m = jnp.max(sc, axis=-1, keepdims=True)
                if has_sink:
                    sk = jnp.where(top, sink_ref[4 * h + c], sink_ref[4 * h + 2 + c]) * LOG2E
                    m = jnp.maximum(m, sk)
                    sks.append(sk)
                ps.append(jnp.exp2(sc - m).astype(BF16))
                ms.append(m)
            ov = jnp.dot(jnp.concatenate(ps, axis=1), vext, preferred_element_type=F32)
            l = ov[:, LANES:]
            if has_sink:
                l = l + jnp.where(lo, jnp.exp2(sks[0] - ms[0]), jnp.exp2(sks[1] - ms[1]))
            o = ov[:, :LANES] / l
            o_ref[rows, (2 * h) * LANES:(2 * h + 1) * LANES] = o[:QBLK].astype(o_ref.dtype)
            o_ref[rows, (2 * h + 1) * LANES:(2 * h + 2) * LANES] = o[QBLK:].astype(o_ref.dtype)
            if has_lse:
                lse = jnp.where(lo, ms[0], ms[1]) * LN2 + jnp.log(l)
                for j in range(2):
                    pair = (lane_q == _lse_lane(4 * h + 2 * j)) | (lane_q == _lse_lane(4 * h + 2 * j + 1))
                    lse_tile = jnp.where(pair, lse[j * QBLK:(j + 1) * QBLK], lse_tile)
        if has_lse:
            lse_ref[rows, :] = lse_tile


def _band_attention(q, k2, v2, bias, sink, *, has_lse):
    nb, dil, length, _ = q.shape
    nq = min(MAX_QBLKS, length // QBLK)
    grid = (nb, dil, length // (nq * QBLK))
    has_sink = sink is not None
    cur = lambda w: pl.BlockSpec((None, None, nq * QBLK, w), lambda b, r, i: (b, r, i, 0))
    prev = lambda w: pl.BlockSpec((None, None, QBLK, w),
                                  lambda b, r, i: (b, r, jnp.maximum(i * nq - 1, 0), 0))
    in_specs = [cur(D_MODEL), prev(2 * D_KV), cur(2 * D_KV), prev(2 * D_KV), cur(2 * D_KV),
                _resident(bias.shape)]
    args = [q, k2, k2, v2, v2, bias]
    if has_sink:
        in_specs = [pl.BlockSpec(memory_space=pltpu.SMEM)] + in_specs
        args = [sink] + args
    out_specs = [cur(D_MODEL)]
    out_shape = [jax.ShapeDtypeStruct((nb, dil, length, D_MODEL), BF16)]
    if has_lse:
        out_specs.append(cur(LANES))
        out_shape.append(jax.ShapeDtypeStruct((nb, dil, length, LANES), F32))
    return pl.pallas_call(
        functools.partial(_band_kernel, has_sink=has_sink, has_lse=has_lse, nq=nq),
        grid=grid, in_specs=in_specs, out_specs=out_specs, out_shape=out_shape,
        compiler_params=_params(("parallel", "parallel", "parallel")),
        name="band_attn",
    )(*args)


def _tap_columns(cache_ref, b, r0):
    dil = cache_ref.shape[2] // LANES
    if dil == 1:
        return cache_ref[b, r0:r0 + D_KV, :]
    lane = jax.lax.broadcasted_iota(jnp.int32, (D_KV, LANES), 1)
    is_tap = (lane & (dil - 1)) == 0
    acc = None
    for t in range(dil):
        part = jnp.where(is_tap, cache_ref[b, r0:r0 + D_KV, t * LANES:(t + 1) * LANES], 0.0)
        if t:
            part = pltpu.roll(part, t, 1)
        acc = part if acc is None else acc + part
    return acc


def _decode_attend(qbd_ref, cache_ref, b, new_ref, sink_ref, o_ref, lse_ref):
    qb = qbd_ref[b]
    new = new_ref[b]
    kn = new[:, 0:D_KV]
    vn = new[:, D_KV:2 * D_KV]
    s = jnp.dot(qb.astype(BF16), _tap_columns(cache_ref, b, 0).astype(BF16),
                preferred_element_type=F32)
    sn = jnp.sum(qb * kn, axis=-1, keepdims=True)
    m = jnp.maximum(jnp.max(s, axis=-1, keepdims=True), sn)
    if sink_ref is not None:
        sk = sink_ref[...] * LOG2E
        m = jnp.maximum(m, sk)
    p = jnp.exp2(s - m)
    pn = jnp.exp2(sn - m)
    l = jnp.sum(p, axis=-1, keepdims=True) + pn
    if sink_ref is not None:
        l = l + jnp.exp2(sk - m)
    o = jax.lax.dot_general(p.astype(BF16), _tap_columns(cache_ref, b, D_KV).astype(BF16),
                            (((1,), (1,)), ((), ())), preferred_element_type=F32) + pn * vn
    o_ref[b] = o / l
    lse_ref[b] = jnp.broadcast_to(m * LN2 + jnp.log(l), (N_HEADS, LANES))


def _shift_rows(cache_ref, b, newt_ref, req, cout_ref, r0, r1):
    win = cache_ref.shape[2]
    lane = jax.lax.broadcasted_iota(jnp.int32, (r1 - r0, LANES), 1)
    shifted = pltpu.roll(cache_ref[b, r0:r1, :], win - 1, 1)
    newcol = pltpu.roll(newt_ref[r0:r1, :], LANES - 1 - req, 1)
    cout_ref[b, r0:r1, :] = shifted
    cout_ref[b, r0:r1, win - LANES:win] = jnp.where(lane == LANES - 1, newcol, shifted[:, win - LANES:win])


def _decode_kernel(*refs, has_sink, rb):
    refs = list(refs)
    sink_ref = refs.pop(0) if has_sink else None
    qbd_ref, cache_ref, new_ref, newt_ref, o_ref, lse_ref, cout_ref = refs
    base = pl.program_id(0) * rb

    def body(b, carry):
        _decode_attend(qbd_ref, cache_ref, b, new_ref, sink_ref, o_ref, lse_ref)
        _shift_rows(cache_ref, b, newt_ref, base + b, cout_ref, 0, 2 * D_KV)
        return carry

    jax.lax.fori_loop(0, rb, body, 0, unroll=min(rb, DECODE_UNROLL))


def _decode_attention(qbd, cache, new, newt, sink_rows):
    nreq, feat, win = cache.shape
    rb = max(1, DECODE_BLOCK_BYTES // (feat * win * 4))
    has_sink = sink_rows is not None
    req = lambda *tail: pl.BlockSpec((rb,) + tail, lambda i: (i,) + (0,) * len(tail))
    in_specs = [req(N_HEADS, D_KV), req(feat, win), req(1, 2 * D_KV),
                pl.BlockSpec((feat, nreq), lambda i: (0, 0))]
    args = [qbd, cache, new, newt]
    if has_sink:
        in_specs = [pl.BlockSpec((N_HEADS, 1), lambda i: (0, 0))] + in_specs
        args = [sink_rows] + args
    return pl.pallas_call(
        functools.partial(_decode_kernel, has_sink=has_sink, rb=rb),
        grid=(nreq // rb,), in_specs=in_specs,
        out_specs=[req(N_HEADS, D_KV), req(N_HEADS, LANES), req(feat, win)],
        out_shape=[jax.ShapeDtypeStruct((nreq, N_HEADS, D_KV), F32),
                   jax.ShapeDtypeStruct((nreq, N_HEADS, LANES), F32),
                   jax.ShapeDtypeStruct(cache.shape, F32)],
        compiler_params=_params(("parallel",)),
        name="decode_attn",
    )(*args)


def _oproj_kernel(*refs, n_groups):
    o_refs = refs[:n_groups]
    lse_refs = refs[n_groups:2 * n_groups] if n_groups > 1 else ()
    x_ref, wo_ref, g_ref, out_ref = refs[-4:]
    if n_groups == 1:
        o = o_refs[0][...].astype(BF16)
    else:
        lses = [r[...] for r in lse_refs]
        m = functools.reduce(jnp.maximum, lses)
        es = [jnp.exp(l - m) for l in lses]
        den = functools.reduce(jnp.add, es)
        o = functools.reduce(jnp.add, [(e / den) * r[...].astype(F32)
                                       for e, r in zip(es, o_refs)]).astype(BF16)
    mix = jnp.dot(o, wo_ref[...], preferred_element_type=F32)
    out_ref[...] = x_ref[...] + _rms_scale(mix) * g_ref[...]


def _oproj(os_, lses, x, wo, g):
    t = x.shape[0]
    tm = min(ROW_TILE, t)
    row = pl.BlockSpec((tm, D_MODEL), lambda i: (i, 0))
    n_groups = len(os_)
    return pl.pallas_call(
        functools.partial(_oproj_kernel, n_groups=n_groups),
        grid=(t // tm,),
        in_specs=[row] * (n_groups + len(lses) + 1) + [_resident(wo.shape), _resident((1, D_MODEL))],
        out_specs=row,
        out_shape=jax.ShapeDtypeStruct((t, D_MODEL), F32),
        compiler_params=_params(("parallel",)),
        name="oproj",
    )(*os_, *lses, x, wo, g)


def _merge_groups(o_refs, lse_refs, unperm_refs, expand_ref, tm, dils):
    os_, lses, k = [], [], 0
    for gi, dil in enumerate(dils):
        if dil == 1:
            os_.append(o_refs[gi][0].astype(F32))
            lses.append(lse_refs[gi][0])
            continue
        sel = unperm_refs[k][...]
        k += 1
        n = tm // 2 // dil
        o_halves, lse_halves = [], []
        for hh in range(2):
            rows = slice(hh * n, (hh + 1) * n)
            o = jnp.concatenate([o_refs[gi][r, rows, :] for r in range(dil)], axis=0)
            lse = jnp.concatenate([lse_refs[gi][r, rows, :] for r in range(dil)], axis=0)
            o_halves.append(jnp.dot(sel, o, preferred_element_type=F32))
            lse_halves.append(_onehot_dot(sel, lse, 3))
        os_.append(jnp.concatenate(o_halves, axis=0))
        lses.append(jnp.concatenate(lse_halves, axis=0))
    m = functools.reduce(jnp.maximum, lses)
    es = [jnp.exp(l - m) for l in lses]
    den = functools.reduce(jnp.add, es)
    ws = [jnp.dot((e / den).astype(BF16), expand_ref[...], preferred_element_type=F32) for e in es]
    return functools.reduce(jnp.add, [w * o for w, o in zip(ws, os_)]).astype(BF16)


def _oproj_folded_kernel(*refs, dils):
    ng = len(dils)
    o_refs, lse_refs = refs[:ng], refs[ng:2 * ng]
    unperm_refs = refs[2 * ng:2 * ng + sum(d > 1 for d in dils)]
    expand_ref, x_ref, wo_ref, g_ref, out_ref = refs[-5:]
    o = _merge_groups(o_refs, lse_refs, unperm_refs, expand_ref, x_ref.shape[0], dils)
    mix = jnp.dot(o, wo_ref[...], preferred_element_type=F32)
    out_ref[...] = x_ref[...] + _rms_scale(mix) * g_ref[...]


def _oproj_folded(os_, lses, unperms, expand, x, wo, g):
    nb, seq, _ = x.shape
    tm = ROW_TILE
    dils = tuple(o.shape[1] for o in os_)
    folded = lambda dil, w: pl.BlockSpec((None, dil, tm // dil, w), lambda b, i: (b, 0, i, 0))
    row = pl.BlockSpec((None, tm, D_MODEL), lambda b, i: (b, i, 0))
    in_specs = ([folded(d, D_MODEL) for d in dils] + [folded(d, LANES) for d in dils]
                + [_resident(u.shape) for u in unperms]
                + [_resident(expand.shape), row, _resident(wo.shape), _resident((1, D_MODEL))])
    return pl.pallas_call(
        functools.partial(_oproj_folded_kernel, dils=dils),
        grid=(nb, seq // tm),
        in_specs=in_specs,
        out_specs=row,
        out_shape=jax.ShapeDtypeStruct((nb, seq, D_MODEL), F32),
        compiler_params=_params(("parallel", "parallel")),
        name="oproj_folded",
    )(*os_, *lses, *unperms, expand, x, wo, g)


def _rope_tabs(pos):
    half = ROT_DIM // 2
    pos = np.asarray(pos, np.float32)
    inv = np.float32(ROPE_THETA) ** (-np.arange(0, ROT_DIM, 2, dtype=np.float32) / np.float32(ROT_DIM))
    ang = pos[:, None] * inv[None, :]
    cos, sin = np.cos(ang), np.sin(ang)
    n = pos.shape[0]
    pad = np.zeros((n, HEAD_DIM - ROT_DIM), np.float32)
    zer = np.zeros((n, half), np.float32)
    c = np.concatenate([cos, cos, pad + 1.0], axis=1)
    s1 = np.concatenate([-sin, zer, pad], axis=1)
    s2 = np.concatenate([zer, sin, pad], axis=1)
    return tuple(jnp.asarray(np.tile(t, (1, LANES // HEAD_DIM)), F32) for t in (c, s1, s2))


def _band_bias():
    i = np.arange(QBLK)[:, None]
    j = np.arange(QBLK)[None, :]
    prev_ok = j >= i
    cur_ok = j <= i
    later = np.concatenate([prev_ok, cur_ok], axis=1)
    first = np.concatenate([np.zeros_like(prev_ok), cur_ok], axis=1)
    b = np.where(np.stack([first, later]), 0.0, NEG).astype(np.float32)
    return jnp.asarray(np.tile(b, (1, 2, 1)))


def _fold_perm(dil):
    half = ROW_TILE // 2
    n = half // dil
    dst = np.arange(half)
    src = (dst % n) * dil + dst // n
    return (src[:, None] == np.arange(half)[None, :]).astype(np.float32)


def _head_expand():
    sel = np.arange(LANES)[:, None] == _lse_lane(np.arange(D_MODEL)[None, :] // HEAD_DIM)
    return jnp.asarray(sel.astype(np.float32), BF16)


def _heads_ig(w):
    k = w.shape[0]
    return w.reshape(k, N_KV, N_HEADS // N_KV, HEAD_DIM).transpose(0, 2, 1, 3).reshape(k, D_MODEL)


def _block_diag_q(q):
    r = q.shape[0]
    q4 = q.astype(F32).reshape(r, 4, 1, N_KV, HEAD_DIM)
    eye = jnp.eye(N_KV, dtype=F32)[None, None, :, :, None]
    return (q4 * eye).reshape(r, N_HEADS, D_KV)


def _own_head(o):
    r = o.shape[0]
    o5 = o.reshape(r, 4, N_KV, N_KV, HEAD_DIM)
    d = jnp.einsum('rigkd,gk->rigd', o5, jnp.eye(N_KV, dtype=o.dtype))
    return d.transpose(0, 2, 1, 3).reshape(r, D_MODEL)


def _expand_lse(lse):
    r = lse.shape[0]
    l = lse[:, :, 0].reshape(r, 4, N_KV).transpose(0, 2, 1).reshape(r, N_HEADS)
    return jnp.repeat(l, HEAD_DIM, axis=1)


def _window_minor(cache):
    r, w = cache.shape[:2]
    return cache.transpose(0, 2, 3, 4, 1).reshape(r, 2 * D_KV, w)


def _window_major(cache_t):
    r, _, w = cache_t.shape
    return cache_t.reshape(r, 2, N_KV, HEAD_DIM, w).transpose(0, 4, 1, 2, 3)


def kernel(x_prompt, x_sample, cache_a_kv, cache_b_kv_w128, cache_b_kv_w512, cache_b_kv_w2048,
           norm_g, w_ffn_gu, w_ffn_dn, w_qkv_a, sink_a, w_o_a, g_kv_b, w_kv_b, w_q_b, w_o_b):
    nb, seq, _ = x_prompt.shape
    nreq = x_sample.shape[0]
    tp = nb * seq
    b_caches = (cache_b_kv_w128, cache_b_kv_w512, cache_b_kv_w2048)

    def gain(l, k):
        return norm_g[l, k].reshape(1, D_MODEL)

    ffn_w = {(1, 1): _cast_weights(w_ffn_gu, w_ffn_dn, 1, 1)}

    def ffn(x, l, k):
        if (l, k) in ffn_w:
            return _ffn(x, gain(l, 4 * k), *ffn_w[l, k], gain(l, 4 * k + 1))
        y, *ffn_w[l, k] = _ffn_cast(x, gain(l, 4 * k), w_ffn_gu, w_ffn_dn, gain(l, 4 * k + 1), l, k)
        return y

    bias = _band_bias()
    tabs_p = _rope_tabs(np.arange(seq))
    tabs_s = _rope_tabs(np.full((nreq,), PAST_LEN))
    perms = {dil: _fold_perm(dil) for _, dil in B_GROUPS if dil > 1}

    wqkv_a = w_qkv_a.astype(BF16)
    wq_b = w_q_b.astype(BF16)
    wkv_b = w_kv_b.astype(BF16)
    col3 = lambda w, j: pl.BlockSpec((None, D_MODEL, w), lambda *_: (0, 0, j), pipeline_mode=pl.Buffered(1))
    col2 = lambda w, j: pl.BlockSpec((D_MODEL, w), lambda *_: (0, j), pipeline_mode=pl.Buffered(1))
    wq_a = w_qkv_a[0][:, :D_MODEL]
    wkv_a = wqkv_a[0][:, D_MODEL:]
    wo_a = w_o_a[0].astype(BF16)
    wo_b = w_o_b[0].astype(BF16)
    g_kv = g_kv_b.reshape(1, D_MODEL)
    sink = sink_a[0].astype(F32)
    sink_rows = sink.reshape(N_KV, 4).T.reshape(N_HEADS, 1)

    assert all(win == dil * LANES for win, dil in B_GROUPS + ((WIN_A, 1),))
    xs = x_sample.reshape(nreq, D_MODEL)
    xs = ffn(xs, 0, 0)
    q, new, newt = _proj_decode(xs, gain(0, 2), gain(0, 2), _heads_ig(wq_a).astype(BF16), wkv_a, tabs_s)
    o, _, cache_a_new = _decode_attention(_block_diag_q(q), _window_minor(cache_a_kv[0]),
                                          new.reshape(nreq, 1, 2 * D_KV), newt, sink_rows)
    xs = _oproj([_own_head(o)], [], xs, wo_a, gain(0, 3))
    xs = ffn(xs, 0, 1)
    xs = ffn(xs, 1, 0)
    jobs = []
    for gi, (win, dil) in enumerate(B_GROUPS):
        wq = _heads_ig(w_q_b[0][:, gi * D_MODEL:(gi + 1) * D_MODEL]).astype(BF16)
        wkv = wkv_b[:, gi * 2 * D_KV:(gi + 1) * 2 * D_KV]
        q, new, newt = _proj_decode(xs, gain(1, 2), g_kv, wq, wkv, tabs_s)
        jobs.append((_block_diag_q(q), _window_minor(b_caches[gi]), new.reshape(nreq, 1, 2 * D_KV), newt))

    dec_o, dec_lse, dec_caches = [], [], None

    def ffn_with_decode(x, l, k, mixer=None):
        nonlocal dec_caches
        req0 = len(dec_o) * (x.shape[0] // ROW_TILE)
        x, o, lse, dec_caches = _ffn_decode(x, gain(l, 4 * k), *ffn_w[l, k], gain(l, 4 * k + 1),
                                            jobs, req0, dec_caches, mixer)
        dec_o.append(o)
        dec_lse.append(lse)
        return x

    x = x_prompt.reshape(tp, D_MODEL)
    x = ffn_with_decode(x, 0, 0)
    q, kv_a, k2, v2 = _proj(x.reshape(nb, seq, D_MODEL), gain(0, 2), gain(0, 2),
                            wqkv_a, col3(D_MODEL, 0), wqkv_a, col3(2 * D_KV, 2), tabs_p, None,
                            dil=1, win=WIN_A)
    o, = _band_attention(q, k2, v2, bias, sink, has_lse=False)
    x = ffn_with_decode(x, 0, 1, ([o], [], [], None, wo_a, gain(0, 3)))
    x = ffn_with_decode(x, 1, 0).reshape(nb, seq, D_MODEL)
    os_, lses, kv_b = [], [], []
    for gi, (win, dil) in enumerate(B_GROUPS):
        perm = jnp.asarray(perms[dil], BF16) if dil > 1 else None
        q, kvc, k2, v2 = _proj(x, gain(1, 2), g_kv, wq_b, col3(D_MODEL, gi), wkv_b, col2(2 * D_KV, gi),
                               tabs_p, perm, dil=dil, win=win)
        o, lse = _band_attention(q, k2, v2, bias, None, has_lse=True)
        os_.append(o)
        lses.append(lse)
        kv_b.append(kvc.reshape(nb, win, 2, N_KV, HEAD_DIM))
    unperms = [jnp.asarray(perms[d].T, BF16) for _, d in B_GROUPS if d > 1]
    x = _oproj_folded(os_, lses, unperms, _head_expand(), x, wo_b, gain(1, 3))
    y_p = ffn_with_decode(x.reshape(tp, D_MODEL), 1, 1).reshape(nb, seq, D_MODEL)
    a_p = kv_a.reshape(1, nb, WIN_A, 2, N_KV, HEAD_DIM)

    assert len(dec_o) * (tp // ROW_TILE) == nreq
    os_ = [_own_head(jnp.concatenate([o[gi] for o in dec_o], axis=0)) for gi in range(len(B_GROUPS))]
    lses = [_expand_lse(jnp.concatenate([l[gi] for l in dec_lse], axis=0)) for gi in range(len(B_GROUPS))]
    b_s = [_window_major(c) for c in dec_caches]
    xs = _oproj(os_, lses, xs, wo_b, gain(1, 3))
    y_s = ffn(xs, 1, 1).reshape(nreq, 1, D_MODEL)
    a_s = _window_major(cache_a_new)[None]

    return (y_p, y_s, a_p, kv_b[0], kv_b[1], kv_b[2], a_s, b_s[0], b_s[1], b_s[2])
```

```python
import functools

import jax
import jax.numpy as jnp
import numpy as np
from jax.experimental import pallas as pl
from jax.experimental.pallas import tpu as pltpu

F32 = jnp.float32
BF16 = jnp.bfloat16

D_MODEL = 1024
HEAD_DIM = 64
ROT_DIM = HEAD_DIM // 4
ROPE_THETA = 500000.0
N_HEADS = 16
N_KV = 4
D_KV = N_KV * HEAD_DIM
D_FF = 2816
EPS = 1e-6
PAST_LEN = 8192
WIN_A = 128
B_GROUPS = ((128, 1), (512, 4), (2048, 16))

LANES = 128
FF_CHUNK = 512
ROW_TILE = 512
QBLK = 128
NEG = -1e30
LOG2E = 1.4426950408889634
LN2 = 0.6931471805599453
Q_SCALE = HEAD_DIM ** -0.5 * LOG2E
MAX_QBLKS = 8
VMEM_LIMIT = 56 * 1024 * 1024
VMEM_LIMIT_FUSED = 60 * 1024 * 1024
DECODE_BLOCK_BYTES = 4 * 1024 * 1024
SHIFT_ROWS = 128
DECODE_UNROLL = 4


def _params(sem, vmem_limit=VMEM_LIMIT):
    return pltpu.CompilerParams(dimension_semantics=sem, vmem_limit_bytes=vmem_limit)


def _resident(shape):
    nd = len(shape)
    return pl.BlockSpec(shape, lambda *_: (0,) * nd, pipeline_mode=pl.Buffered(1))


def _rms_scale(x):
    return x * jax.lax.rsqrt(jnp.mean(x * x, axis=-1, keepdims=True) + EPS)


def _split_bf16(x, parts):
    out = []
    for _ in range(parts - 1):
        hi = x.astype(BF16)
        out.append(hi)
        x = x - hi.astype(F32)
    out.append(x.astype(BF16))
    return out


def _onehot_dot(sel, x, parts):
    return functools.reduce(
        jnp.add, [jnp.dot(sel, p, preferred_element_type=F32) for p in _split_bf16(x, parts)])


def _ffn_kernel(x_ref, gpre_ref, wg_ref, wu_ref, wd_ref, gpost_ref, o_ref, side_work=(), mix=None):
    x = x_ref[...]
    if mix is not None:
        x = x + mix()
    xn = (_rms_scale(x) * gpre_ref[...]).astype(BF16)
    acc = None
    bounds = list(range(0, D_FF, FF_CHUNK)) + [D_FF]
    n_chunks = len(bounds) - 1
    for c in range(n_chunks):
        sl = slice(bounds[c], bounds[c + 1])
        gate = jnp.dot(xn, wg_ref[:, sl], preferred_element_type=F32)
        up = jnp.dot(xn, wu_ref[:, sl], preferred_element_type=F32)
        act = (gate * jax.nn.sigmoid(gate) * up).astype(BF16)
        part = jnp.dot(act, wd_ref[sl, :], preferred_element_type=F32)
        acc = part if acc is None else acc + part
        for work in side_work[c::n_chunks]:
            work()
    o_ref[...] = x + 0.5 * (_rms_scale(acc) * gpost_ref[...])


def _ffn_specs(t, wg, wu, wd):
    tm = min(ROW_TILE, t)
    row = pl.BlockSpec((tm, D_MODEL), lambda i: (i, 0))
    in_specs = [row, _resident((1, D_MODEL)), _resident(wg.shape), _resident(wu.shape),
                _resident(wd.shape), _resident((1, D_MODEL))]
    return t // tm, in_specs, row


def _ffn(x, gpre, wg, wu, wd, gpost):
    t = x.shape[0]
    steps, in_specs, row = _ffn_specs(t, wg, wu, wd)
    return pl.pallas_call(
        _ffn_kernel,
        grid=(steps,),
        in_specs=in_specs,
        out_specs=row,
        out_shape=jax.ShapeDtypeStruct((t, D_MODEL), F32),
        compiler_params=_params(("parallel",)),
        name="ffn",
    )(x, gpre, wg, wu, wd, gpost)


CAST_CHUNK = 256


def _ffn_cast_kernel(x_ref, gpre_ref, wg_ref, wu_ref, wd_ref, gpost_ref,
                     y_ref, wg_out, wu_out, wd_out, xn_ref, acc_ref):
    c = pl.program_id(0)

    @pl.when(c == 0)
    def _():
        xn_ref[...] = (_rms_scale(x_ref[...]) * gpre_ref[...]).astype(BF16)
        acc_ref[...] = jnp.zeros_like(acc_ref)

    wg, wu, wd = wg_ref[...].astype(BF16), wu_ref[...].astype(BF16), wd_ref[...].astype(BF16)
    wg_out[...], wu_out[...], wd_out[...] = wg, wu, wd
    xn = xn_ref[...]
    gate = jnp.dot(xn, wg, preferred_element_type=F32)
    up = jnp.dot(xn, wu, preferred_element_type=F32)
    act = (gate * jax.nn.sigmoid(gate) * up).astype(BF16)
    acc_ref[...] += jnp.dot(act, wd, preferred_element_type=F32)

    @pl.when(c == pl.num_programs(0) - 1)
    def _():
        y_ref[...] = x_ref[...] + 0.5 * (_rms_scale(acc_ref[...]) * gpost_ref[...])


def _cast_kernel(wg_ref, wu_ref, wd_ref, wg_out, wu_out, wd_out):
    wg_out[...] = wg_ref[...].astype(BF16)
    wu_out[...] = wu_ref[...].astype(BF16)
    wd_out[...] = wd_ref[...].astype(BF16)


def _cast_specs(l, k):
    n = D_FF // CAST_CHUNK
    in_specs = [pl.BlockSpec((None, None, D_MODEL, CAST_CHUNK), lambda c: (l, k, 0, c)),
                pl.BlockSpec((None, None, D_MODEL, CAST_CHUNK), lambda c: (l, k, 0, n + c)),
                pl.BlockSpec((None, None, CAST_CHUNK, D_MODEL), lambda c: (l, k, c, 0))]
    out_specs = [pl.BlockSpec((D_MODEL, CAST_CHUNK), lambda c: (0, c)),
                 pl.BlockSpec((D_MODEL, CAST_CHUNK), lambda c: (0, c)),
                 pl.BlockSpec((CAST_CHUNK, D_MODEL), lambda c: (c, 0))]
    out_shape = [jax.ShapeDtypeStruct((D_MODEL, D_FF), BF16),
                 jax.ShapeDtypeStruct((D_MODEL, D_FF), BF16),
                 jax.ShapeDtypeStruct((D_FF, D_MODEL), BF16)]
    return n, in_specs, out_specs, out_shape


def _cast_weights(w_gu, w_dn, l, k):
    n, in_specs, out_specs, out_shape = _cast_specs(l, k)
    return pl.pallas_call(
        _cast_kernel, grid=(n,), in_specs=in_specs, out_specs=out_specs, out_shape=out_shape,
        compiler_params=_params(("parallel",)), name="cast_weights",
    )(w_gu, w_gu, w_dn)


def _ffn_cast(x, gpre, w_gu, w_dn, gpost, l, k):
    t = x.shape[0]
    n, w_in, w_out, w_shape = _cast_specs(l, k)
    full = pl.BlockSpec((t, D_MODEL), lambda c: (0, 0))
    vec = pl.BlockSpec((1, D_MODEL), lambda c: (0, 0))
    return pl.pallas_call(
        _ffn_cast_kernel,
        grid=(n,),
        in_specs=[full, vec, *w_in, vec],
        out_specs=[full, *w_out],
        out_shape=[jax.ShapeDtypeStruct((t, D_MODEL), F32), *w_shape],
        scratch_shapes=[pltpu.VMEM((t, D_MODEL), BF16), pltpu.VMEM((t, D_MODEL), F32)],
        compiler_params=_params(("arbitrary",)),
        name="ffn_cast",
    )(x, gpre, w_gu, w_gu, w_dn, gpost)


def _ffn_decode_kernel(*refs, n_jobs, req0, has_prev, mix_dils):
    per_job = 5 if has_prev else 4
    ffn_in = refs[:6]
    n_mix = 0
    mix = None
    if mix_dils is not None:
        ng = len(mix_dils)
        n_mix = 2 * ng + sum(d > 1 for d in mix_dils) + 3 if ng > 1 else 3
        mix_refs = refs[6:6 + n_mix]
        wo_ref, g_ref = mix_refs[-2:]

        def mix():
            if ng == 1:
                o = mix_refs[0][...]
            else:
                o = _merge_groups(mix_refs[:ng], mix_refs[ng:2 * ng], mix_refs[2 * ng:-3], mix_refs[-3],
                                  ffn_in[0].shape[0], mix_dils)
            return _rms_scale(jnp.dot(o, wo_ref[...], preferred_element_type=F32)) * g_ref[...]

    first = 6 + n_mix
    job_in = [refs[first + j * per_job:first + (j + 1) * per_job] for j in range(n_jobs)]
    outs = refs[first + n_jobs * per_job:]
    req = req0 + pl.program_id(0)
    side_work = []
    for j in range(n_jobs):
        qbd_ref, cache_ref, new_ref, newt_ref = job_in[j][:4]
        o_ref, lse_ref, cout_ref = outs[1 + 3 * j:4 + 3 * j]
        side_work.append(functools.partial(
            _decode_attend, qbd_ref, cache_ref, 0, new_ref, None, o_ref, lse_ref))
        for r0 in range(0, 2 * D_KV, SHIFT_ROWS):
            side_work.append(functools.partial(
                _shift_rows, cache_ref, 0, newt_ref, req, cout_ref, r0, r0 + SHIFT_ROWS))
    _ffn_kernel(*ffn_in, outs[0], side_work=side_work, mix=mix)


def _ffn_decode(x, gpre, wg, wu, wd, gpost, jobs, req0, prev_caches, mixer=None):
    t = x.shape[0]
    steps, in_specs, row = _ffn_specs(t, wg, wu, wd)
    has_prev = prev_caches is not None
    args = [x, gpre, wg, wu, wd, gpost]
    mix_dils = None
    if mixer is not None:
        os_, lses, unperms, expand, wo, g = mixer
        mix_dils = tuple(o.shape[1] for o in os_)
        tiles = os_[0].shape[2] * mix_dils[0] // ROW_TILE
        folded = lambda dil, w: pl.BlockSpec((None, dil, ROW_TILE // dil, w),
                                             lambda i: (i // tiles, 0, i % tiles, 0))
        if len(os_) == 1:
            in_specs += [pl.BlockSpec((None, None, ROW_TILE, D_MODEL), lambda i: (i // tiles, 0, i % tiles, 0))]
            args += [os_[0]]
        else:
            in_specs += ([folded(d, D_MODEL) for d in mix_dils] + [folded(d, LANES) for d in mix_dils]
                         + [_resident(u.shape) for u in unperms] + [_resident(expand.shape)])
            args += [*os_, *lses, *unperms, expand]
        in_specs += [_resident(wo.shape), _resident((1, D_MODEL))]
        args += [wo, g]
    out_specs, out_shape, aliases = [row], [jax.ShapeDtypeStruct((t, D_MODEL), F32)], {}
    for j, (qbd, cache, new, newt) in enumerate(jobs):
        _, feat, win = cache.shape
        req = lambda *tail: pl.BlockSpec((1,) + tail, lambda i: (req0 + i,) + (0,) * len(tail))
        loc = lambda *tail: pl.BlockSpec((1,) + tail, lambda i: (i,) + (0,) * len(tail))
        in_specs += [req(N_HEADS, D_KV), req(feat, win), req(1, 2 * D_KV), _resident(newt.shape)]
        args += [qbd, cache, new, newt]
        if has_prev:
            aliases[len(args)] = 3 + 3 * j
            in_specs.append(pl.BlockSpec(memory_space=pl.ANY))
            args.append(prev_caches[j])
        out_specs += [loc(N_HEADS, D_KV), loc(N_HEADS, LANES), req(feat, win)]
        out_shape += [jax.ShapeDtypeStruct((steps, N_HEADS, D_KV), F32),
                      jax.ShapeDtypeStruct((steps, N_HEADS, LANES), F32),
                      jax.ShapeDtypeStruct(cache.shape, F32)]
    res = pl.pallas_call(
        functools.partial(_ffn_decode_kernel, n_jobs=len(jobs), req0=req0, has_prev=has_prev,
                          mix_dils=mix_dils),
        grid=(steps,),
        in_specs=in_specs, out_specs=out_specs, out_shape=out_shape,
        input_output_aliases=aliases,
        compiler_params=_params(("arbitrary",), VMEM_LIMIT_FUSED),
        name="ffn_decode",
    )(*args)
    return res[0], list(res[1::3]), list(res[2::3]), list(res[3::3])


def _project(x_ref, gq_ref, gk_ref, wq_ref, wkv_ref, rc_ref, rs1_ref, rs2_ref):
    xh = _rms_scale(x_ref[...])
    hq = (xh * gq_ref[...]).astype(BF16)
    hk = (xh * gk_ref[...]).astype(BF16)
    rc, rs1, rs2 = rc_ref[...], rs1_ref[...], rs2_ref[...]

    def rope(t):
        return t * rc + pltpu.roll(t, LANES - ROT_DIM // 2, 1) * rs1 + pltpu.roll(t, ROT_DIM // 2, 1) * rs2

    q = jnp.dot(hq, wq_ref[...], preferred_element_type=F32)
    qs = [(rope(q[:, c * LANES:(c + 1) * LANES]) * Q_SCALE).astype(BF16)
          for c in range(D_MODEL // LANES)]
    kv = jnp.dot(hk, wkv_ref[...], preferred_element_type=F32)
    kvs = []
    for c in range(2 * D_KV // LANES):
        t = kv[:, c * LANES:(c + 1) * LANES]
        kvs.append(rope(t) if c < D_KV // LANES else t)
    return qs, kvs


def _dup_heads(t):
    lo = jax.lax.broadcasted_iota(jnp.int32, t.shape, 1) < HEAD_DIM
    sw = pltpu.roll(t, HEAD_DIM, 1)
    return [jnp.where(lo, t, sw).astype(BF16), jnp.where(lo, sw, t).astype(BF16)]


def _proj_kernel(*refs, dil, cache_rows):
    ins = refs[:8]
    perm_ref = refs[8] if dil > 1 else None
    q_ref, kv_ref, k2_ref, v2_ref = refs[-4:]
    tm = ins[0].shape[0]
    qs, kvs = _project(*ins)
    for c, t in enumerate(kvs):
        kv_ref[:, c * LANES:(c + 1) * LANES] = t[tm - cache_rows:, :]
    nk = D_KV // LANES
    if dil == 1:
        q_ref[0] = jnp.concatenate(qs, axis=1)
        k2_ref[0] = jnp.concatenate([d for t in kvs[:nk] for d in _dup_heads(t)], axis=1)
        v2_ref[0] = jnp.concatenate([d for t in kvs[nk:] for d in _dup_heads(t)], axis=1)
        return
    cols = jnp.concatenate(qs + [t.astype(BF16) for t in kvs], axis=1)
    half = tm // 2
    n = half // dil
    for hh in range(2):
        f = jnp.dot(perm_ref[...], cols[hh * half:(hh + 1) * half, :], preferred_element_type=F32)
        qf = f[:, :D_MODEL].astype(BF16)
        kf = jnp.concatenate([d for c in range(nk)
                              for d in _dup_heads(f[:, D_MODEL + c * LANES:D_MODEL + (c + 1) * LANES])], axis=1)
        vf = jnp.concatenate([d for c in range(nk, 2 * nk)
                              for d in _dup_heads(f[:, D_MODEL + c * LANES:D_MODEL + (c + 1) * LANES])], axis=1)
        for r in range(dil):
            dst = slice(hh * n, (hh + 1) * n)
            q_ref[r, dst, :] = qf[r * n:(r + 1) * n, :]
            k2_ref[r, dst, :] = kf[r * n:(r + 1) * n, :]
            v2_ref[r, dst, :] = vf[r * n:(r + 1) * n, :]


def _proj(x, gq, gk, wq, wq_spec, wkv, wkv_spec, tabs, perm, *, dil, win):
    nb, seq, _ = x.shape
    tm = ROW_TILE
    nt = seq // tm
    cache_rows = min(win, tm)
    first_cache_tile = nt - win // cache_rows
    n = tm // dil
    tab = pl.BlockSpec((tm, LANES), lambda b, i: (i, 0))
    folded = lambda w: pl.BlockSpec((None, dil, n, w), lambda b, i: (b, 0, i, 0))
    in_specs = [pl.BlockSpec((None, tm, D_MODEL), lambda b, i: (b, i, 0)),
                _resident((1, D_MODEL)), _resident((1, D_MODEL)),
                wq_spec, wkv_spec, tab, tab, tab]
    args = [x, gq, gk, wq, wkv, *tabs]
    if dil > 1:
        in_specs.append(_resident(perm.shape))
        args.append(perm)
    return pl.pallas_call(
        functools.partial(_proj_kernel, dil=dil, cache_rows=cache_rows),
        grid=(nb, nt),
        in_specs=in_specs,
        out_specs=[folded(D_MODEL),
                   pl.BlockSpec((None, cache_rows, 2 * D_KV),
                                lambda b, i: (b, jnp.maximum(i - first_cache_tile, 0), 0)),
                   folded(2 * D_KV), folded(2 * D_KV)],
        out_shape=[jax.ShapeDtypeStruct((nb, dil, seq // dil, D_MODEL), BF16),
                   jax.ShapeDtypeStruct((nb, win, 2 * D_KV), F32),
                   jax.ShapeDtypeStruct((nb, dil, seq // dil, 2 * D_KV), BF16),
                   jax.ShapeDtypeStruct((nb, dil, seq // dil, 2 * D_KV), BF16)],
        compiler_params=_params(("parallel", "arbitrary")),
        name="proj",
    )(*args)


def _proj_decode_kernel(*refs):
    q_ref, new_ref, newt_ref = refs[-3:]
    qs, kvs = _project(*refs[:8])
    for c, t in enumerate(qs):
        q_ref[:, c * LANES:(c + 1) * LANES] = t
    for c, t in enumerate(kvs):
        new_ref[:, c * LANES:(c + 1) * LANES] = t
        newt_ref[c * LANES:(c + 1) * LANES, :] = t.T


def _proj_decode(x, gq, gk, wq, wq_spec, wkv, wkv_spec, tabs):
    nreq = x.shape[0]
    full = lambda w: pl.BlockSpec((nreq, w), lambda i: (0, 0))
    return pl.pallas_call(
        _proj_decode_kernel,
        grid=(1,),
        in_specs=[full(D_MODEL), _resident((1, D_MODEL)), _resident((1, D_MODEL)),
                  wq_spec, wkv_spec, full(LANES), full(LANES), full(LANES)],
        out_specs=[full(D_MODEL), full(2 * D_KV), pl.BlockSpec((2 * D_KV, nreq), lambda i: (0, 0))],
        out_shape=[jax.ShapeDtypeStruct((nreq, D_MODEL), BF16),
                   jax.ShapeDtypeStruct((nreq, 2 * D_KV), F32),
                   jax.ShapeDtypeStruct((2 * D_KV, nreq), F32)],
        compiler_params=_params(("arbitrary",)),
        name="proj_decode",
    )(x, gq, gk, wq, wkv, *tabs)


def _lse_lane(head):
    return (head % 2) * HEAD_DIM + head - head % 2


def _band_kernel(*refs, has_sink, has_lse, nq):
    refs = list(refs)
    sink_ref = refs.pop(0) if has_sink else None
    q_ref, kp_ref, kc_ref, vp_ref, vc_ref, bias_ref, o_ref = refs[:7]
    lse_ref = refs[7] if has_lse else None

    lo = jax.lax.broadcasted_iota(jnp.int32, (2 * QBLK, LANES), 1) < HEAD_DIM
    lane_q = jax.lax.broadcasted_iota(jnp.int32, (QBLK, LANES), 1)
    top = jax.lax.broadcasted_iota(jnp.int32, (2 * QBLK, 1), 0) < QBLK
    zero = jnp.zeros((), BF16)
    krow_lo = jax.lax.broadcasted_iota(jnp.int32, (4 * QBLK, LANES), 0) < 2 * QBLK
    klane_lo = jax.lax.broadcasted_iota(jnp.int32, (4 * QBLK, LANES), 1) < HEAD_DIM
    ones_sel = jnp.where(krow_lo == klane_lo, 1.0, 0.0).astype(BF16)
    first_step = pl.program_id(2) == 0

    for t in range(nq):
        rows = slice(t * QBLK, (t + 1) * QBLK)
        bias = bias_ref[1] if t else bias_ref[jnp.where(first_step, 0, 1)]
        lse_tile = jnp.zeros((QBLK, LANES), F32)
        for h in range(N_KV):
            ksl = slice(h * LANES, (h + 1) * LANES)
            kprev = kc_ref[(t - 1) * QBLK:t * QBLK, ksl] if t else kp_ref[:, ksl]
            vprev = vc_ref[(t - 1) * QBLK:t * QBLK, ksl] if t else vp_ref[:, ksl]
            k2 = jnp.concatenate([kprev, kc_ref[rows, ksl]], axis=0)
            v2 = jnp.concatenate([vprev, vc_ref[rows, ksl]], axis=0)
            kbd = jnp.concatenate([jnp.where(lo, k2, zero), jnp.where(lo, zero, k2)], axis=0)
            vbd = jnp.concatenate([jnp.where(lo, v2, zero), jnp.where(lo, zero, v2)], axis=0)
            vext = jnp.concatenate([vbd, ones_sel], axis=1)
            qq = jnp.concatenate([q_ref[rows, (2 * h) * LANES:(2 * h + 1) * LANES],
                                  q_ref[rows, (2 * h + 1) * LANES:(2 * h + 2) * LANES]], axis=0)
            s = jax.lax.dot_general(qq, kbd, (((1,), (1,)), ((), ())),
                                    preferred_element_type=F32)
            ps, ms, sks = [], [], []
            for c in range(2):
                sc = s[:, c * 2 * QBLK:(c + 1) * 2 * QBLK] + bias
                m = jnp.max(sc, axis=-1, keepdims=True)
                if has_sink:
                    sk = jnp.where(top, sink_ref[4 * h + c], sink_ref[4 * h + 2 + c]) * LOG2E
                    m = jnp.maximum(m, sk)
                    sks.append(sk)
                ps.append(jnp.exp2(sc - m).astype(BF16))
                ms.append(m)
            ov = jnp.dot(jnp.concatenate(ps, axis=1), vext, preferred_element_type=F32)
            l = ov[:, LANES:]
            if has_sink:
                l = l + jnp.where(lo, jnp.exp2(sks[0] - ms[0]), jnp.exp2(sks[1] - ms[1]))
            o = ov[:, :LANES] / l
            o_ref[rows, (2 * h) * LANES:(2 * h + 1) * LANES] = o[:QBLK].astype(o_ref.dtype)
            o_ref[rows, (2 * h + 1) * LANES:(2 * h + 2) * LANES] = o[QBLK:].astype(o_ref.dtype)
            if has_lse:
                lse = jnp.where(lo, ms[0], ms[1]) * LN2 + jnp.log(l)
                for j in range(2):
                    pair = (lane_q == _lse_lane(4 * h + 2 * j)) | (lane_q == _lse_lane(4 * h + 2 * j + 1))
                    lse_tile = jnp.where(pair, lse[j * QBLK:(j + 1) * QBLK], lse_tile)
        if has_lse:
            lse_ref[rows, :] = lse_tile


def _band_attention(q, k2, v2, bias, sink, *, has_lse):
    nb, dil, length, _ = q.shape
    nq = min(MAX_QBLKS, length // QBLK)
    grid = (nb, dil, length // (nq * QBLK))
    has_sink = sink is not None
    cur = lambda w: pl.BlockSpec((None, None, nq * QBLK, w), lambda b, r, i: (b, r, i, 0))
    prev = lambda w: pl.BlockSpec((None, None, QBLK, w),
                                  lambda b, r, i: (b, r, jnp.maximum(i * nq - 1, 0), 0))
    in_specs = [cur(D_MODEL), prev(2 * D_KV), cur(2 * D_KV), prev(2 * D_KV), cur(2 * D_KV),
                _resident(bias.shape)]
    args = [q, k2, k2, v2, v2, bias]
    if has_sink:
        in_specs = [pl.BlockSpec(memory_space=pltpu.SMEM)] + in_specs
        args = [sink] + args
    out_specs = [cur(D_MODEL)]
    out_shape = [jax.ShapeDtypeStruct((nb, dil, length, D_MODEL), BF16)]
    if has_lse:
        out_specs.append(cur(LANES))
        out_shape.append(jax.ShapeDtypeStruct((nb, dil, length, LANES), F32))
    return pl.pallas_call(
        functools.partial(_band_kernel, has_sink=has_sink, has_lse=has_lse, nq=nq),
        grid=grid, in_specs=in_specs, out_specs=out_specs, out_shape=out_shape,
        compiler_params=_params(("parallel", "parallel", "parallel")),
        name="band_attn",
    )(*args)


def _tap_columns(cache_ref, b, r0):
    dil = cache_ref.shape[2] // LANES
    if dil == 1:
        return cache_ref[b, r0:r0 + D_KV, :]
    lane = jax.lax.broadcasted_iota(jnp.int32, (D_KV, LANES), 1)
    is_tap = (lane & (dil - 1)) == 0
    acc = None
    for t in range(dil):
        part = jnp.where(is_tap, cache_ref[b, r0:r0 + D_KV, t * LANES:(t + 1) * LANES], 0.0)
        if t:
            part = pltpu.roll(part, t, 1)
        acc = part if acc is None else acc + part
    return acc


def _decode_attend(qbd_ref, cache_ref, b, new_ref, sink_ref, o_ref, lse_ref):
    qb = qbd_ref[b]
    new = new_ref[b]
    kn = new[:, 0:D_KV]
    vn = new[:, D_KV:2 * D_KV]
    s = jnp.dot(qb.astype(BF16), _tap_columns(cache_ref, b, 0).astype(BF16),
                preferred_element_type=F32)
    sn = jnp.sum(qb * kn, axis=-1, keepdims=True)
    m = jnp.maximum(jnp.max(s, axis=-1, keepdims=True), sn)
    if sink_ref is not None:
        sk = sink_ref[...] * LOG2E
        m = jnp.maximum(m, sk)
    p = jnp.exp2(s - m)
    pn = jnp.exp2(sn - m)
    l = jnp.sum(p, axis=-1, keepdims=True) + pn
    if sink_ref is not None:
        l = l + jnp.exp2(sk - m)
    o = jax.lax.dot_general(p.astype(BF16), _tap_columns(cache_ref, b, D_KV).astype(BF16),
                            (((1,), (1,)), ((), ())), preferred_element_type=F32) + pn * vn
    o_ref[b] = o / l
    lse_ref[b] = jnp.broadcast_to(m * LN2 + jnp.log(l), (N_HEADS, LANES))


def _shift_rows(cache_ref, b, newt_ref, req, cout_ref, r0, r1):
    win = cache_ref.shape[2]
    lane = jax.lax.broadcasted_iota(jnp.int32, (r1 - r0, LANES), 1)
    shifted = pltpu.roll(cache_ref[b, r0:r1, :], win - 1, 1)
    newcol = pltpu.roll(newt_ref[r0:r1, :], LANES - 1 - req, 1)
    cout_ref[b, r0:r1, :] = shifted
    cout_ref[b, r0:r1, win - LANES:win] = jnp.where(lane == LANES - 1, newcol, shifted[:, win - LANES:win])


def _decode_kernel(*refs, has_sink, rb):
    refs = list(refs)
    sink_ref = refs.pop(0) if has_sink else None
    qbd_ref, cache_ref, new_ref, newt_ref, o_ref, lse_ref, cout_ref = refs
    base = pl.program_id(0) * rb

    def body(b, carry):
        _decode_attend(qbd_ref, cache_ref, b, new_ref, sink_ref, o_ref, lse_ref)
        _shift_rows(cache_ref, b, newt_ref, base + b, cout_ref, 0, 2 * D_KV)
        return carry

    jax.lax.fori_loop(0, rb, body, 0, unroll=min(rb, DECODE_UNROLL))


def _decode_attention(qbd, cache, new, newt, sink_rows):
    nreq, feat, win = cache.shape
    rb = max(1, DECODE_BLOCK_BYTES // (feat * win * 4))
    has_sink = sink_rows is not None
    req = lambda *tail: pl.BlockSpec((rb,) + tail, lambda i: (i,) + (0,) * len(tail))
    in_specs = [req(N_HEADS, D_KV), req(feat, win), req(1, 2 * D_KV),
                pl.BlockSpec((feat, nreq), lambda i: (0, 0))]
    args = [qbd, cache, new, newt]
    if has_sink:
        in_specs = [pl.BlockSpec((N_HEADS, 1), lambda i: (0, 0))] + in_specs
        args = [sink_rows] + args
    return pl.pallas_call(
        functools.partial(_decode_kernel, has_sink=has_sink, rb=rb),
        grid=(nreq // rb,), in_specs=in_specs,
        out_specs=[req(N_HEADS, D_KV), req(N_HEADS, LANES), req(feat, win)],
        out_shape=[jax.ShapeDtypeStruct((nreq, N_HEADS, D_KV), F32),
                   jax.ShapeDtypeStruct((nreq, N_HEADS, LANES), F32),
                   jax.ShapeDtypeStruct(cache.shape, F32)],
        compiler_params=_params(("parallel",)),
        name="decode_attn",
    )(*args)


def _oproj_kernel(*refs, n_groups):
    o_refs = refs[:n_groups]
    lse_refs = refs[n_groups:2 * n_groups] if n_groups > 1 else ()
    x_ref, wo_ref, g_ref, out_ref = refs[-4:]
    if n_groups == 1:
        o = o_refs[0][...].astype(BF16)
    else:
        lses = [r[...] for r in lse_refs]
        m = functools.reduce(jnp.maximum, lses)
        es = [jnp.exp(l - m) for l in lses]
        den = functools.reduce(jnp.add, es)
        o = functools.reduce(jnp.add, [(e / den) * r[...].astype(F32)
                                       for e, r in zip(es, o_refs)]).astype(BF16)
    mix = jnp.dot(o, wo_ref[...], preferred_element_type=F32)
    out_ref[...] = x_ref[...] + _rms_scale(mix) * g_ref[...]


def _oproj(os_, lses, x, wo, g):
    t = x.shape[0]
    tm = min(ROW_TILE, t)
    row = pl.BlockSpec((tm, D_MODEL), lambda i: (i, 0))
    n_groups = len(os_)
    return pl.pallas_call(
        functools.partial(_oproj_kernel, n_groups=n_groups),
        grid=(t // tm,),
        in_specs=[row] * (n_groups + len(lses) + 1) + [_resident(wo.shape), _resident((1, D_MODEL))],
        out_specs=row,
        out_shape=jax.ShapeDtypeStruct((t, D_MODEL), F32),
        compiler_params=_params(("parallel",)),
        name="oproj",
    )(*os_, *lses, x, wo, g)


def _merge_groups(o_refs, lse_refs, unperm_refs, expand_ref, tm, dils):
    os_, lses, k = [], [], 0
    for gi, dil in enumerate(dils):
        if dil == 1:
            os_.append(o_refs[gi][0].astype(F32))
            lses.append(lse_refs[gi][0])
            continue
        sel = unperm_refs[k][...]
        k += 1
        n = tm // 2 // dil
        o_halves, lse_halves = [], []
        for hh in range(2):
            rows = slice(hh * n, (hh + 1) * n)
            o = jnp.concatenate([o_refs[gi][r, rows, :] for r in range(dil)], axis=0)
            lse = jnp.concatenate([lse_refs[gi][r, rows, :] for r in range(dil)], axis=0)
            o_halves.append(jnp.dot(sel, o, preferred_element_type=F32))
            lse_halves.append(_onehot_dot(sel, lse, 3))
        os_.append(jnp.concatenate(o_halves, axis=0))
        lses.append(jnp.concatenate(lse_halves, axis=0))
    m = functools.reduce(jnp.maximum, lses)
    es = [jnp.exp(l - m) for l in lses]
    den = functools.reduce(jnp.add, es)
    ws = [jnp.dot((e / den).astype(BF16), expand_ref[...], preferred_element_type=F32) for e in es]
    return functools.reduce(jnp.add, [w * o for w, o in zip(ws, os_)]).astype(BF16)


def _oproj_folded_kernel(*refs, dils):
    ng = len(dils)
    o_refs, lse_refs = refs[:ng], refs[ng:2 * ng]
    unperm_refs = refs[2 * ng:2 * ng + sum(d > 1 for d in dils)]
    expand_ref, x_ref, wo_ref, g_ref, out_ref = refs[-5:]
    o = _merge_groups(o_refs, lse_refs, unperm_refs, expand_ref, x_ref.shape[0], dils)
    mix = jnp.dot(o, wo_ref[...], preferred_element_type=F32)
    out_ref[...] = x_ref[...] + _rms_scale(mix) * g_ref[...]


def _oproj_folded(os_, lses, unperms, expand, x, wo, g):
    nb, seq, _ = x.shape
    tm = ROW_TILE
    dils = tuple(o.shape[1] for o in os_)
    folded = lambda dil, w: pl.BlockSpec((None, dil, tm // dil, w), lambda b, i: (b, 0, i, 0))
    row = pl.BlockSpec((None, tm, D_MODEL), lambda b, i: (b, i, 0))
    in_specs = ([folded(d, D_MODEL) for d in dils] + [folded(d, LANES) for d in dils]
                + [_resident(u.shape) for u in unperms]
                + [_resident(expand.shape), row, _resident(wo.shape), _resident((1, D_MODEL))])
    return pl.pallas_call(
        functools.partial(_oproj_folded_kernel, dils=dils),
        grid=(nb, seq // tm),
        in_specs=in_specs,
        out_specs=row,
        out_shape=jax.ShapeDtypeStruct((nb, seq, D_MODEL), F32),
        compiler_params=_params(("parallel", "parallel")),
        name="oproj_folded",
    )(*os_, *lses, *unperms, expand, x, wo, g)


def _rope_tabs(pos):
    half = ROT_DIM // 2
    pos = np.asarray(pos, np.float32)
    inv = np.float32(ROPE_THETA) ** (-np.arange(0, ROT_DIM, 2, dtype=np.float32) / np.float32(ROT_DIM))
    ang = pos[:, None] * inv[None, :]
    cos, sin = np.cos(ang), np.sin(ang)
    n = pos.shape[0]
    pad = np.zeros((n, HEAD_DIM - ROT_DIM), np.float32)
    zer = np.zeros((n, half), np.float32)
    c = np.concatenate([cos, cos, pad + 1.0], axis=1)
    s1 = np.concatenate([-sin, zer, pad], axis=1)
    s2 = np.concatenate([zer, sin, pad], axis=1)
    return tuple(jnp.asarray(np.tile(t, (1, LANES // HEAD_DIM)), F32) for t in (c, s1, s2))


def _band_bias():
    i = np.arange(QBLK)[:, None]
    j = np.arange(QBLK)[None, :]
    prev_ok = j >= i
    cur_ok = j <= i
    later = np.concatenate([prev_ok, cur_ok], axis=1)
    first = np.concatenate([np.zeros_like(prev_ok), cur_ok], axis=1)
    b = np.where(np.stack([first, later]), 0.0, NEG).astype(np.float32)
    return jnp.asarray(np.tile(b, (1, 2, 1)))


def _fold_perm(dil):
    half = ROW_TILE // 2
    n = half // dil
    dst = np.arange(half)
    src = (dst % n) * dil + dst // n
    return (src[:, None] == np.arange(half)[None, :]).astype(np.float32)


def _head_expand():
    sel = np.arange(LANES)[:, None] == _lse_lane(np.arange(D_MODEL)[None, :] // HEAD_DIM)
    return jnp.asarray(sel.astype(np.float32), BF16)


def _block_diag_q(q):
    r = q.shape[0]
    q4 = q.astype(F32).reshape(r, N_KV, 4, HEAD_DIM).transpose(0, 2, 1, 3)[:, :, None]
    eye = jnp.eye(N_KV, dtype=F32)[None, None, :, :, None]
    return (q4 * eye).reshape(r, N_HEADS, D_KV)


def _own_head(o):
    r = o.shape[0]
    o5 = o.reshape(r, 4, N_KV, N_KV, HEAD_DIM)
    d = jnp.einsum('rigkd,gk->rigd', o5, jnp.eye(N_KV, dtype=o.dtype))
    return d.transpose(0, 2, 1, 3).reshape(r, D_MODEL)


def _expand_lse(lse):
    r = lse.shape[0]
    l = lse[:, :, 0].reshape(r, 4, N_KV).transpose(0, 2, 1).reshape(r, N_HEADS)
    return jnp.repeat(l, HEAD_DIM, axis=1)


def _window_minor(cache):
    r, w = cache.shape[:2]
    return cache.transpose(0, 2, 3, 4, 1).reshape(r, 2 * D_KV, w)


def _window_major(cache_t):
    r, _, w = cache_t.shape
    return cache_t.reshape(r, 2, N_KV, HEAD_DIM, w).transpose(0, 4, 1, 2, 3)


def kernel(x_prompt, x_sample, cache_a_kv, cache_b_kv_w128, cache_b_kv_w512, cache_b_kv_w2048,
           norm_g, w_ffn_gu, w_ffn_dn, w_qkv_a, sink_a, w_o_a, g_kv_b, w_kv_b, w_q_b, w_o_b):
    nb, seq, _ = x_prompt.shape
    nreq = x_sample.shape[0]
    tp = nb * seq
    b_caches = (cache_b_kv_w128, cache_b_kv_w512, cache_b_kv_w2048)

    def gain(l, k):
        return norm_g[l, k].reshape(1, D_MODEL)

    ffn_w = {(1, 1): _cast_weights(w_ffn_gu, w_ffn_dn, 1, 1)}

    def ffn(x, l, k):
        if (l, k) in ffn_w:
            return _ffn(x, gain(l, 4 * k), *ffn_w[l, k], gain(l, 4 * k + 1))
        y, *ffn_w[l, k] = _ffn_cast(x, gain(l, 4 * k), w_ffn_gu, w_ffn_dn, gain(l, 4 * k + 1), l, k)
        return y

    bias = _band_bias()
    tabs_p = _rope_tabs(np.arange(seq))
    tabs_s = _rope_tabs(np.full((nreq,), PAST_LEN))
    perms = {dil: _fold_perm(dil) for _, dil in B_GROUPS if dil > 1}

    wqkv_a = w_qkv_a.astype(BF16)
    wq_b = w_q_b.astype(BF16)
    wkv_b = w_kv_b.astype(BF16)
    col3 = lambda w, j: pl.BlockSpec((None, D_MODEL, w), lambda *_: (0, 0, j), pipeline_mode=pl.Buffered(1))
    col2 = lambda w, j: pl.BlockSpec((D_MODEL, w), lambda *_: (0, j), pipeline_mode=pl.Buffered(1))
    wo_a = w_o_a[0].astype(BF16)
    wo_b = w_o_b[0].astype(BF16)
    g_kv = g_kv_b.reshape(1, D_MODEL)
    sink = sink_a[0].astype(F32)
    sink_rows = sink.reshape(N_KV, 4).T.reshape(N_HEADS, 1)

    assert all(win == dil * LANES for win, dil in B_GROUPS + ((WIN_A, 1),))
    xs = x_sample.reshape(nreq, D_MODEL)
    xs = ffn(xs, 0, 0)
    q, new, newt = _proj_decode(xs, gain(0, 2), gain(0, 2), wqkv_a, col3(D_MODEL, 0),
                                wqkv_a, col3(2 * D_KV, 2), tabs_s)
    o, _, cache_a_new = _decode_attention(_block_diag_q(q), _window_minor(cache_a_kv[0]),
                                          new.reshape(nreq, 1, 2 * D_KV), newt, sink_rows)
    xs = _oproj([_own_head(o)], [], xs, wo_a, gain(0, 3))
    xs = ffn(xs, 0, 1)
    xs = ffn(xs, 1, 0)
    jobs = []
    for gi, (win, dil) in enumerate(B_GROUPS):
        q, new, newt = _proj_decode(xs, gain(1, 2), g_kv, wq_b, col3(D_MODEL, gi),
                                    wkv_b, col2(2 * D_KV, gi), tabs_s)
        jobs.append((_block_diag_q(q), _window_minor(b_caches[gi]), new.reshape(nreq, 1, 2 * D_KV), newt))

    dec_o, dec_lse, dec_caches = [], [], None

    def ffn_with_decode(x, l, k, mixer=None):
        nonlocal dec_caches
        req0 = len(dec_o) * (x.shape[0] // ROW_TILE)
        x, o, lse, dec_caches = _ffn_decode(x, gain(l, 4 * k), *ffn_w[l, k], gain(l, 4 * k + 1),
                                            jobs, req0, dec_caches, mixer)
        dec_o.append(o)
        dec_lse.append(lse)
        return x

    x = x_prompt.reshape(tp, D_MODEL)
    x = ffn_with_decode(x, 0, 0)
    q, kv_a, k2, v2 = _proj(x.reshape(nb, seq, D_MODEL), gain(0, 2), gain(0, 2),
                            wqkv_a, col3(D_MODEL, 0), wqkv_a, col3(2 * D_KV, 2), tabs_p, None,
                            dil=1, win=WIN_A)
    o, = _band_attention(q, k2, v2, bias, sink, has_lse=False)
    x = ffn_with_decode(x, 0, 1, ([o], [], [], None, wo_a, gain(0, 3)))
    x = ffn_with_decode(x, 1, 0).reshape(nb, seq, D_MODEL)
    os_, lses, kv_b = [], [], []
    for gi, (win, dil) in enumerate(B_GROUPS):
        perm = jnp.asarray(perms[dil], BF16) if dil > 1 else None
        q, kvc, k2, v2 = _proj(x, gain(1, 2), g_kv, wq_b, col3(D_MODEL, gi), wkv_b, col2(2 * D_KV, gi),
                               tabs_p, perm, dil=dil, win=win)
        o, lse = _band_attention(q, k2, v2, bias, None, has_lse=True)
        os_.append(o)
        lses.append(lse)
        kv_b.append(kvc.reshape(nb, win, 2, N_KV, HEAD_DIM))
    unperms = [jnp.asarray(perms[d].T, BF16) for _, d in B_GROUPS if d > 1]
    x = _oproj_folded(os_, lses, unperms, _head_expand(), x, wo_b, gain(1, 3))
    y_p = ffn_with_decode(x.reshape(tp, D_MODEL), 1, 1).reshape(nb, seq, D_MODEL)
    a_p = kv_a.reshape(1, nb, WIN_A, 2, N_KV, HEAD_DIM)

    assert len(dec_o) * (tp // ROW_TILE) == nreq
    os_ = [_own_head(jnp.concatenate([o[gi] for o in dec_o], axis=0)) for gi in range(len(B_GROUPS))]
    lses = [_expand_lse(jnp.concatenate([l[gi] for l in dec_lse], axis=0)) for gi in range(len(B_GROUPS))]
    b_s = [_window_major(c) for c in dec_caches]
    xs = _oproj(os_, lses, xs, wo_b, gain(1, 3))
    y_s = ffn(xs, 1, 1).reshape(nreq, 1, D_MODEL)
    a_s = _window_major(cache_a_new)[None]

    return (y_p, y_s, a_p, kv_b[0], kv_b[1], kv_b[2], a_s, b_s[0], b_s[1], b_s[2])
```

```python
import functools

import jax
import jax.numpy as jnp
import numpy as np
from jax.experimental import pallas as pl
from jax.experimental.pallas import tpu as pltpu

F32 = jnp.float32
BF16 = jnp.bfloat16

D_MODEL = 1024
HEAD_DIM = 64
ROT_DIM = HEAD_DIM // 4
ROPE_THETA = 500000.0
N_HEADS = 16
N_KV = 4
D_KV = N_KV * HEAD_DIM
D_FF = 2816
EPS = 1e-6
PAST_LEN = 8192
WIN_A = 128
B_GROUPS = ((128, 1), (512, 4), (2048, 16))

LANES = 128
FF_CHUNK = 512
ROW_TILE = 512
QBLK = 128
NEG = -1e30
LOG2E = 1.4426950408889634
LN2 = 0.6931471805599453
Q_SCALE = HEAD_DIM ** -0.5 * LOG2E
MAX_QBLKS = 8
VMEM_LIMIT = 56 * 1024 * 1024
VMEM_LIMIT_FUSED = 60 * 1024 * 1024
DECODE_BLOCK_BYTES = 4 * 1024 * 1024
SHIFT_ROWS = 128
DECODE_UNROLL = 4


def _params(sem, vmem_limit=VMEM_LIMIT):
    return pltpu.CompilerParams(dimension_semantics=sem, vmem_limit_bytes=vmem_limit)


def _resident(shape):
    nd = len(shape)
    return pl.BlockSpec(shape, lambda *_: (0,) * nd, pipeline_mode=pl.Buffered(1))


def _rms_scale(x):
    return x * jax.lax.rsqrt(jnp.mean(x * x, axis=-1, keepdims=True) + EPS)


def _split_bf16(x, parts):
    out = []
    for _ in range(parts - 1):
        hi = x.astype(BF16)
        out.append(hi)
        x = x - hi.astype(F32)
    out.append(x.astype(BF16))
    return out


def _onehot_dot(sel, x, parts):
    return functools.reduce(
        jnp.add, [jnp.dot(sel, p, preferred_element_type=F32) for p in _split_bf16(x, parts)])


def _ffn_kernel(x_ref, gpre_ref, wg_ref, wu_ref, wd_ref, gpost_ref, o_ref, side_work=(), mix=None):
    x = x_ref[...]
    if mix is not None:
        x = x + mix()
    xn = (_rms_scale(x) * gpre_ref[...]).astype(BF16)
    acc = None
    bounds = list(range(0, D_FF, FF_CHUNK)) + [D_FF]
    n_chunks = len(bounds) - 1
    for c in range(n_chunks):
        sl = slice(bounds[c], bounds[c + 1])
        gate = jnp.dot(xn, wg_ref[:, sl], preferred_element_type=F32)
        up = jnp.dot(xn, wu_ref[:, sl], preferred_element_type=F32)
        act = (gate * jax.nn.sigmoid(gate) * up).astype(BF16)
        part = jnp.dot(act, wd_ref[sl, :], preferred_element_type=F32)
        acc = part if acc is None else acc + part
        for work in side_work[c::n_chunks]:
            work()
    o_ref[...] = x + 0.5 * (_rms_scale(acc) * gpost_ref[...])


def _ffn_specs(t, wg, wu, wd):
    tm = min(ROW_TILE, t)
    row = pl.BlockSpec((tm, D_MODEL), lambda i: (i, 0))
    in_specs = [row, _resident((1, D_MODEL)), _resident(wg.shape), _resident(wu.shape),
                _resident(wd.shape), _resident((1, D_MODEL))]
    return t // tm, in_specs, row


def _ffn(x, gpre, wg, wu, wd, gpost):
    t = x.shape[0]
    steps, in_specs, row = _ffn_specs(t, wg, wu, wd)
    return pl.pallas_call(
        _ffn_kernel,
        grid=(steps,),
        in_specs=in_specs,
        out_specs=row,
        out_shape=jax.ShapeDtypeStruct((t, D_MODEL), F32),
        compiler_params=_params(("parallel",)),
        name="ffn",
    )(x, gpre, wg, wu, wd, gpost)


CAST_CHUNK = 256


def _ffn_cast_kernel(x_ref, gpre_ref, wg_ref, wu_ref, wd_ref, gpost_ref,
                     y_ref, wg_out, wu_out, wd_out, xn_ref, acc_ref):
    c = pl.program_id(0)

    @pl.when(c == 0)
    def _():
        xn_ref[...] = (_rms_scale(x_ref[...]) * gpre_ref[...]).astype(BF16)
        acc_ref[...] = jnp.zeros_like(acc_ref)

    wg, wu, wd = wg_ref[...].astype(BF16), wu_ref[...].astype(BF16), wd_ref[...].astype(BF16)
    wg_out[...], wu_out[...], wd_out[...] = wg, wu, wd
    xn = xn_ref[...]
    gate = jnp.dot(xn, wg, preferred_element_type=F32)
    up = jnp.dot(xn, wu, preferred_element_type=F32)
    act = (gate * jax.nn.sigmoid(gate) * up).astype(BF16)
    acc_ref[...] += jnp.dot(act, wd, preferred_element_type=F32)

    @pl.when(c == pl.num_programs(0) - 1)
    def _():
        y_ref[...] = x_ref[...] + 0.5 * (_rms_scale(acc_ref[...]) * gpost_ref[...])


def _cast_kernel(wg_ref, wu_ref, wd_ref, wg_out, wu_out, wd_out):
    wg_out[...] = wg_ref[...].astype(BF16)
    wu_out[...] = wu_ref[...].astype(BF16)
    wd_out[...] = wd_ref[...].astype(BF16)


def _cast_specs(l, k):
    n = D_FF // CAST_CHUNK
    in_specs = [pl.BlockSpec((None, None, D_MODEL, CAST_CHUNK), lambda c: (l, k, 0, c)),
                pl.BlockSpec((None, None, D_MODEL, CAST_CHUNK), lambda c: (l, k, 0, n + c)),
                pl.BlockSpec((None, None, CAST_CHUNK, D_MODEL), lambda c: (l, k, c, 0))]
    out_specs = [pl.BlockSpec((D_MODEL, CAST_CHUNK), lambda c: (0, c)),
                 pl.BlockSpec((D_MODEL, CAST_CHUNK), lambda c: (0, c)),
                 pl.BlockSpec((CAST_CHUNK, D_MODEL), lambda c: (c, 0))]
    out_shape = [jax.ShapeDtypeStruct((D_MODEL, D_FF), BF16),
                 jax.ShapeDtypeStruct((D_MODEL, D_FF), BF16),
                 jax.ShapeDtypeStruct((D_FF, D_MODEL), BF16)]
    return n, in_specs, out_specs, out_shape


def _cast_weights(w_gu, w_dn, l, k):
    n, in_specs, out_specs, out_shape = _cast_specs(l, k)
    return pl.pallas_call(
        _cast_kernel, grid=(n,), in_specs=in_specs, out_specs=out_specs, out_shape=out_shape,
        compiler_params=_params(("parallel",)), name="cast_weights",
    )(w_gu, w_gu, w_dn)


def _ffn_cast(x, gpre, w_gu, w_dn, gpost, l, k):
    t = x.shape[0]
    n, w_in, w_out, w_shape = _cast_specs(l, k)
    full = pl.BlockSpec((t, D_MODEL), lambda c: (0, 0))
    vec = pl.BlockSpec((1, D_MODEL), lambda c: (0, 0))
    return pl.pallas_call(
        _ffn_cast_kernel,
        grid=(n,),
        in_specs=[full, vec, *w_in, vec],
        out_specs=[full, *w_out],
        out_shape=[jax.ShapeDtypeStruct((t, D_MODEL), F32), *w_shape],
        scratch_shapes=[pltpu.VMEM((t, D_MODEL), BF16), pltpu.VMEM((t, D_MODEL), F32)],
        compiler_params=_params(("arbitrary",)),
        name="ffn_cast",
    )(x, gpre, w_gu, w_gu, w_dn, gpost)


def _ffn_decode_kernel(*refs, n_jobs, req0, has_prev, mix_dils):
    per_job = 5 if has_prev else 4
    ffn_in = refs[:6]
    n_mix = 0
    mix = None
    if mix_dils is not None:
        ng = len(mix_dils)
        n_mix = 2 * ng + sum(d > 1 for d in mix_dils) + 3 if ng > 1 else 3
        mix_refs = refs[6:6 + n_mix]
        wo_ref, g_ref = mix_refs[-2:]

        def mix():
            if ng == 1:
                o = mix_refs[0][...]
            else:
                o = _merge_groups(mix_refs[:ng], mix_refs[ng:2 * ng], mix_refs[2 * ng:-3], mix_refs[-3],
                                  ffn_in[0].shape[0], mix_dils)
            return _rms_scale(jnp.dot(o, wo_ref[...], preferred_element_type=F32)) * g_ref[...]

    first = 6 + n_mix
    job_in = [refs[first + j * per_job:first + (j + 1) * per_job] for j in range(n_jobs)]
    outs = refs[first + n_jobs * per_job:]
    req = req0 + pl.program_id(0)
    side_work = []
    for j in range(n_jobs):
        qbd_ref, cache_ref, new_ref, newt_ref = job_in[j][:4]
        o_ref, lse_ref, cout_ref = outs[1 + 3 * j:4 + 3 * j]
        side_work.append(functools.partial(
            _decode_attend, qbd_ref, cache_ref, 0, new_ref, None, o_ref, lse_ref))
        for r0 in range(0, 2 * D_KV, SHIFT_ROWS):
            side_work.append(functools.partial(
                _shift_rows, cache_ref, 0, newt_ref, req, cout_ref, r0, r0 + SHIFT_ROWS))
    _ffn_kernel(*ffn_in, outs[0], side_work=side_work, mix=mix)


def _ffn_decode(x, gpre, wg, wu, wd, gpost, jobs, req0, prev_caches, mixer=None):
    t = x.shape[0]
    steps, in_specs, row = _ffn_specs(t, wg, wu, wd)
    has_prev = prev_caches is not None
    args = [x, gpre, wg, wu, wd, gpost]
    mix_dils = None
    if mixer is not None:
        os_, lses, unperms, expand, wo, g = mixer
        mix_dils = tuple(o.shape[1] for o in os_)
        tiles = os_[0].shape[2] * mix_dils[0] // ROW_TILE
        folded = lambda dil, w: pl.BlockSpec((None, dil, ROW_TILE // dil, w),
                                             lambda i: (i // tiles, 0, i % tiles, 0))
        if len(os_) == 1:
            in_specs += [pl.BlockSpec((None, None, ROW_TILE, D_MODEL), lambda i: (i // tiles, 0, i % tiles, 0))]
            args += [os_[0]]
        else:
            in_specs += ([folded(d, D_MODEL) for d in mix_dils] + [folded(d, LANES) for d in mix_dils]
                         + [_resident(u.shape) for u in unperms] + [_resident(expand.shape)])
            args += [*os_, *lses, *unperms, expand]
        in_specs += [_resident(wo.shape), _resident((1, D_MODEL))]
        args += [wo, g]
    out_specs, out_shape, aliases = [row], [jax.ShapeDtypeStruct((t, D_MODEL), F32)], {}
    for j, (qbd, cache, new, newt) in enumerate(jobs):
        _, feat, win = cache.shape
        req = lambda *tail: pl.BlockSpec((1,) + tail, lambda i: (req0 + i,) + (0,) * len(tail))
        loc = lambda *tail: pl.BlockSpec((1,) + tail, lambda i: (i,) + (0,) * len(tail))
        in_specs += [req(N_HEADS, D_KV), req(feat, win), req(1, 2 * D_KV), _resident(newt.shape)]
        args += [qbd, cache, new, newt]
        if has_prev:
            aliases[len(args)] = 3 + 3 * j
            in_specs.append(pl.BlockSpec(memory_space=pl.ANY))
            args.append(prev_caches[j])
        out_specs += [loc(N_HEADS, D_KV), loc(N_HEADS, LANES), req(feat, win)]
        out_shape += [jax.ShapeDtypeStruct((steps, N_HEADS, D_KV), F32),
                      jax.ShapeDtypeStruct((steps, N_HEADS, LANES), F32),
                      jax.ShapeDtypeStruct(cache.shape, F32)]
    res = pl.pallas_call(
        functools.partial(_ffn_decode_kernel, n_jobs=len(jobs), req0=req0, has_prev=has_prev,
                          mix_dils=mix_dils),
        grid=(steps,),
        in_specs=in_specs, out_specs=out_specs, out_shape=out_shape,
        input_output_aliases=aliases,
        compiler_params=_params(("arbitrary",), VMEM_LIMIT_FUSED),
        name="ffn_decode",
    )(*args)
    return res[0], list(res[1::3]), list(res[2::3]), list(res[3::3])


def _project(x_ref, gq_ref, gk_ref, wq_ref, wkv_ref, rc_ref, rs1_ref, rs2_ref):
    xh = _rms_scale(x_ref[...])
    hq = (xh * gq_ref[...]).astype(BF16)
    hk = (xh * gk_ref[...]).astype(BF16)
    rc, rs1, rs2 = rc_ref[...], rs1_ref[...], rs2_ref[...]

    def rope(t):
        return t * rc + pltpu.roll(t, LANES - ROT_DIM // 2, 1) * rs1 + pltpu.roll(t, ROT_DIM // 2, 1) * rs2

    q = jnp.dot(hq, wq_ref[...], preferred_element_type=F32)
    qs = [(rope(q[:, c * LANES:(c + 1) * LANES]) * Q_SCALE).astype(BF16)
          for c in range(D_MODEL // LANES)]
    kv = jnp.dot(hk, wkv_ref[...], preferred_element_type=F32)
    kvs = []
    for c in range(2 * D_KV // LANES):
        t = kv[:, c * LANES:(c + 1) * LANES]
        kvs.append(rope(t) if c < D_KV // LANES else t)
    return qs, kvs


def _dup_heads(t):
    lo = jax.lax.broadcasted_iota(jnp.int32, t.shape, 1) < HEAD_DIM
    sw = pltpu.roll(t, HEAD_DIM, 1)
    return [jnp.where(lo, t, sw).astype(BF16), jnp.where(lo, sw, t).astype(BF16)]


def _proj_kernel(*refs, dil, cache_rows):
    ins = refs[:8]
    perm_ref = refs[8] if dil > 1 else None
    q_ref, kv_ref, k2_ref, v2_ref = refs[-4:]
    tm = ins[0].shape[0]
    qs, kvs = _project(*ins)
    for c, t in enumerate(kvs):
        kv_ref[:, c * LANES:(c + 1) * LANES] = t[tm - cache_rows:, :]
    nk = D_KV // LANES
    if dil == 1:
        q_ref[0] = jnp.concatenate(qs, axis=1)
        k2_ref[0] = jnp.concatenate([d for t in kvs[:nk] for d in _dup_heads(t)], axis=1)
        v2_ref[0] = jnp.concatenate([d for t in kvs[nk:] for d in _dup_heads(t)], axis=1)
        return
    cols = jnp.concatenate(qs + [t.astype(BF16) for t in kvs], axis=1)
    half = tm // 2
    n = half // dil
    for hh in range(2):
        f = jnp.dot(perm_ref[...], cols[hh * half:(hh + 1) * half, :], preferred_element_type=F32)
        qf = f[:, :D_MODEL].astype(BF16)
        kf = jnp.concatenate([d for c in range(nk)
                              for d in _dup_heads(f[:, D_MODEL + c * LANES:D_MODEL + (c + 1) * LANES])], axis=1)
        vf = jnp.concatenate([d for c in range(nk, 2 * nk)
                              for d in _dup_heads(f[:, D_MODEL + c * LANES:D_MODEL + (c + 1) * LANES])], axis=1)
        for r in range(dil):
            dst = slice(hh * n, (hh + 1) * n)
            q_ref[r, dst, :] = qf[r * n:(r + 1) * n, :]
            k2_ref[r, dst, :] = kf[r * n:(r + 1) * n, :]
            v2_ref[r, dst, :] = vf[r * n:(r + 1) * n, :]


def _proj(x, gq, gk, wq, wq_spec, wkv, wkv_spec, tabs, perm, *, dil, win):
    nb, seq, _ = x.shape
    tm = ROW_TILE if dil > 1 else 2 * ROW_TILE
    nt = seq // tm
    cache_rows = min(win, tm)
    first_cache_tile = nt - win // cache_rows
    n = tm // dil
    tab = pl.BlockSpec((tm, LANES), lambda b, i: (i, 0))
    folded = lambda w: pl.BlockSpec((None, dil, n, w), lambda b, i: (b, 0, i, 0))
    in_specs = [pl.BlockSpec((None, tm, D_MODEL), lambda b, i: (b, i, 0)),
                _resident((1, D_MODEL)), _resident((1, D_MODEL)),
                wq_spec, wkv_spec, tab, tab, tab]
    args = [x, gq, gk, wq, wkv, *tabs]
    if dil > 1:
        in_specs.append(_resident(perm.shape))
        args.append(perm)
    return pl.pallas_call(
        functools.partial(_proj_kernel, dil=dil, cache_rows=cache_rows),
        grid=(nb, nt),
        in_specs=in_specs,
        out_specs=[folded(D_MODEL),
                   pl.BlockSpec((None, cache_rows, 2 * D_KV),
                                lambda b, i: (b, jnp.maximum(i - first_cache_tile, 0), 0)),
                   folded(2 * D_KV), folded(2 * D_KV)],
        out_shape=[jax.ShapeDtypeStruct((nb, dil, seq // dil, D_MODEL), BF16),
                   jax.ShapeDtypeStruct((nb, win, 2 * D_KV), F32),
                   jax.ShapeDtypeStruct((nb, dil, seq // dil, 2 * D_KV), BF16),
                   jax.ShapeDtypeStruct((nb, dil, seq // dil, 2 * D_KV), BF16)],
        compiler_params=_params(("parallel", "arbitrary")),
        name="proj",
    )(*args)


def _proj_decode_kernel(*refs):
    q_ref, new_ref, newt_ref = refs[-3:]
    qs, kvs = _project(*refs[:8])
    for c, t in enumerate(qs):
        q_ref[:, c * LANES:(c + 1) * LANES] = t
    for c, t in enumerate(kvs):
        new_ref[:, c * LANES:(c + 1) * LANES] = t
        newt_ref[c * LANES:(c + 1) * LANES, :] = t.T


def _proj_decode(x, gq, gk, wq, wq_spec, wkv, wkv_spec, tabs):
    nreq = x.shape[0]
    full = lambda w: pl.BlockSpec((nreq, w), lambda i: (0, 0))
    return pl.pallas_call(
        _proj_decode_kernel,
        grid=(1,),
        in_specs=[full(D_MODEL), _resident((1, D_MODEL)), _resident((1, D_MODEL)),
                  wq_spec, wkv_spec, full(LANES), full(LANES), full(LANES)],
        out_specs=[full(D_MODEL), full(2 * D_KV), pl.BlockSpec((2 * D_KV, nreq), lambda i: (0, 0))],
        out_shape=[jax.ShapeDtypeStruct((nreq, D_MODEL), BF16),
                   jax.ShapeDtypeStruct((nreq, 2 * D_KV), F32),
                   jax.ShapeDtypeStruct((2 * D_KV, nreq), F32)],
        compiler_params=_params(("arbitrary",)),
        name="proj_decode",
    )(x, gq, gk, wq, wkv, *tabs)


def _lse_lane(head):
    return (head % 2) * HEAD_DIM + head - head % 2


def _band_kernel(*refs, has_sink, has_lse, nq, nr):
    refs = list(refs)
    sink_ref = refs.pop(0) if has_sink else None
    q_ref, kp_ref, kc_ref, vp_ref, vc_ref, bias_ref, o_ref = refs[:7]
    lse_ref = refs[7] if has_lse else None

    lo = jax.lax.broadcasted_iota(jnp.int32, (2 * QBLK, LANES), 1) < HEAD_DIM
    lane_q = jax.lax.broadcasted_iota(jnp.int32, (QBLK, LANES), 1)
    top = jax.lax.broadcasted_iota(jnp.int32, (2 * QBLK, 1), 0) < QBLK
    zero = jnp.zeros((), BF16)
    krow_lo = jax.lax.broadcasted_iota(jnp.int32, (4 * QBLK, LANES), 0) < 2 * QBLK
    klane_lo = jax.lax.broadcasted_iota(jnp.int32, (4 * QBLK, LANES), 1) < HEAD_DIM
    ones_sel = jnp.where(krow_lo == klane_lo, 1.0, 0.0).astype(BF16)
    first_step = pl.program_id(2) == 0

    for rr in range(nr):
        q_r, kp_r, kc_r, vp_r, vc_r, o_r = (ref.at[rr] for ref in (q_ref, kp_ref, kc_ref, vp_ref, vc_ref, o_ref))
        lse_r = lse_ref.at[rr] if has_lse else None
        for t in range(nq):
            rows = slice(t * QBLK, (t + 1) * QBLK)
            bias = bias_ref[1] if t else bias_ref[jnp.where(first_step, 0, 1)]
            lse_tile = jnp.zeros((QBLK, LANES), F32)
            for h in range(N_KV):
                ksl = slice(h * LANES, (h + 1) * LANES)
                kprev = kc_r[(t - 1) * QBLK:t * QBLK, ksl] if t else kp_r[:, ksl]
                vprev = vc_r[(t - 1) * QBLK:t * QBLK, ksl] if t else vp_r[:, ksl]
                k2 = jnp.concatenate([kprev, kc_r[rows, ksl]], axis=0)
                v2 = jnp.concatenate([vprev, vc_r[rows, ksl]], axis=0)
                kbd = jnp.concatenate([jnp.where(lo, k2, zero), jnp.where(lo, zero, k2)], axis=0)
                vbd = jnp.concatenate([jnp.where(lo, v2, zero), jnp.where(lo, zero, v2)], axis=0)
                vext = jnp.concatenate([vbd, ones_sel], axis=1)
                qq = jnp.concatenate([q_r[rows, (2 * h) * LANES:(2 * h + 1) * LANES],
                                      q_r[rows, (2 * h + 1) * LANES:(2 * h + 2) * LANES]], axis=0)
                s = jax.lax.dot_general(qq, kbd, (((1,), (1,)), ((), ())),
                                        preferred_element_type=F32)
                ps, ms, sks = [], [], []
                for c in range(2):
                    sc = s[:, c * 2 * QBLK:(c + 1) * 2 * QBLK] + bias
                    m = jnp.max(sc, axis=-1, keepdims=True)
                    if has_sink:
                        sk = jnp.where(top, sink_ref[4 * h + c], sink_ref[4 * h + 2 + c]) * LOG2E
                        m = jnp.maximum(m, sk)
                        sks.append(sk)
                    ps.append(jnp.exp2(sc - m).astype(BF16))
                    ms.append(m)
                ov = jnp.dot(jnp.concatenate(ps, axis=1), vext, preferred_element_type=F32)
                l = ov[:, LANES:]
                if has_sink:
                    l = l + jnp.where(lo, jnp.exp2(sks[0] - ms[0]), jnp.exp2(sks[1] - ms[1]))
                o = ov[:, :LANES] / l
                o_r[rows, (2 * h) * LANES:(2 * h + 1) * LANES] = o[:QBLK].astype(o_ref.dtype)
                o_r[rows, (2 * h + 1) * LANES:(2 * h + 2) * LANES] = o[QBLK:].astype(o_ref.dtype)
                if has_lse:
                    lse = jnp.where(lo, ms[0], ms[1]) * LN2 + jnp.log(l)
                    for j in range(2):
                        pair = (lane_q == _lse_lane(4 * h + 2 * j)) | (lane_q == _lse_lane(4 * h + 2 * j + 1))
                        lse_tile = jnp.where(pair, lse[j * QBLK:(j + 1) * QBLK], lse_tile)
            if has_lse:
                lse_r[rows, :] = lse_tile


def _band_attention(q, k2, v2, bias, sink, *, has_lse):
    nb, dil, length, _ = q.shape
    nq = min(MAX_QBLKS, length // QBLK)
    nr = min(dil, MAX_QBLKS // nq)
    grid = (nb, dil // nr, length // (nq * QBLK))
    has_sink = sink is not None
    cur = lambda w: pl.BlockSpec((None, nr, nq * QBLK, w), lambda b, r, i: (b, r, i, 0))
    prev = lambda w: pl.BlockSpec((None, nr, QBLK, w),
                                  lambda b, r, i: (b, r, jnp.maximum(i * nq - 1, 0), 0))
    in_specs = [cur(D_MODEL), prev(2 * D_KV), cur(2 * D_KV), prev(2 * D_KV), cur(2 * D_KV),
                _resident(bias.shape)]
    args = [q, k2, k2, v2, v2, bias]
    if has_sink:
        in_specs = [pl.BlockSpec(memory_space=pltpu.SMEM)] + in_specs
        args = [sink] + args
    out_specs = [cur(D_MODEL)]
    out_shape = [jax.ShapeDtypeStruct((nb, dil, length, D_MODEL), BF16)]
    if has_lse:
        out_specs.append(cur(LANES))
        out_shape.append(jax.ShapeDtypeStruct((nb, dil, length, LANES), F32))
    return pl.pallas_call(
        functools.partial(_band_kernel, has_sink=has_sink, has_lse=has_lse, nq=nq, nr=nr),
        grid=grid, in_specs=in_specs, out_specs=out_specs, out_shape=out_shape,
        compiler_params=_params(("parallel", "parallel", "parallel")),
        name="band_attn",
    )(*args)


def _tap_columns(cache_ref, b, r0):
    dil = cache_ref.shape[2] // LANES
    if dil == 1:
        return cache_ref[b, r0:r0 + D_KV, :]
    lane = jax.lax.broadcasted_iota(jnp.int32, (D_KV, LANES), 1)
    is_tap = (lane & (dil - 1)) == 0
    acc = None
    for t in range(dil):
        part = jnp.where(is_tap, cache_ref[b, r0:r0 + D_KV, t * LANES:(t + 1) * LANES], 0.0)
        if t:
            part = pltpu.roll(part, t, 1)
        acc = part if acc is None else acc + part
    return acc


def _decode_attend(qbd_ref, cache_ref, b, new_ref, sink_ref, o_ref, lse_ref):
    qb = qbd_ref[b]
    new = new_ref[b]
    kn = new[:, 0:D_KV]
    vn = new[:, D_KV:2 * D_KV]
    s = jnp.dot(qb.astype(BF16), _tap_columns(cache_ref, b, 0).astype(BF16),
                preferred_element_type=F32)
    sn = jnp.sum(qb * kn, axis=-1, keepdims=True)
    m = jnp.maximum(jnp.max(s, axis=-1, keepdims=True), sn)
    if sink_ref is not None:
        sk = sink_ref[...] * LOG2E
        m = jnp.maximum(m, sk)
    p = jnp.exp2(s - m)
    pn = jnp.exp2(sn - m)
    l = jnp.sum(p, axis=-1, keepdims=True) + pn
    if sink_ref is not None:
        l = l + jnp.exp2(sk - m)
    o = jax.lax.dot_general(p.astype(BF16), _tap_columns(cache_ref, b, D_KV).astype(BF16),
                            (((1,), (1,)), ((), ())), preferred_element_type=F32) + pn * vn
    o_ref[b] = o / l
    lse_ref[b] = jnp.broadcast_to(m * LN2 + jnp.log(l), (N_HEADS, LANES))


def _shift_rows(cache_ref, b, newt_ref, req, cout_ref, r0, r1):
    win = cache_ref.shape[2]
    lane = jax.lax.broadcasted_iota(jnp.int32, (r1 - r0, LANES), 1)
    shifted = pltpu.roll(cache_ref[b, r0:r1, :], win - 1, 1)
    newcol = pltpu.roll(newt_ref[r0:r1, :], LANES - 1 - req, 1)
    cout_ref[b, r0:r1, :] = shifted
    cout_ref[b, r0:r1, win - LANES:win] = jnp.where(lane == LANES - 1, newcol, shifted[:, win - LANES:win])


def _decode_kernel(*refs, has_sink, rb):
    refs = list(refs)
    sink_ref = refs.pop(0) if has_sink else None
    qbd_ref, cache_ref, new_ref, newt_ref, o_ref, lse_ref, cout_ref = refs
    base = pl.program_id(0) * rb

    def body(b, carry):
        _decode_attend(qbd_ref, cache_ref, b, new_ref, sink_ref, o_ref, lse_ref)
        _shift_rows(cache_ref, b, newt_ref, base + b, cout_ref, 0, 2 * D_KV)
        return carry

    jax.lax.fori_loop(0, rb, body, 0, unroll=min(rb, DECODE_UNROLL))


def _decode_attention(qbd, cache, new, newt, sink_rows):
    nreq, feat, win = cache.shape
    rb = max(1, DECODE_BLOCK_BYTES // (feat * win * 4))
    has_sink = sink_rows is not None
    req = lambda *tail: pl.BlockSpec((rb,) + tail, lambda i: (i,) + (0,) * len(tail))
    in_specs = [req(N_HEADS, D_KV), req(feat, win), req(1, 2 * D_KV),
                pl.BlockSpec((feat, nreq), lambda i: (0, 0))]
    args = [qbd, cache, new, newt]
    if has_sink:
        in_specs = [pl.BlockSpec((N_HEADS, 1), lambda i: (0, 0))] + in_specs
        args = [sink_rows] + args
    return pl.pallas_call(
        functools.partial(_decode_kernel, has_sink=has_sink, rb=rb),
        grid=(nreq // rb,), in_specs=in_specs,
        out_specs=[req(N_HEADS, D_KV), req(N_HEADS, LANES), req(feat, win)],
        out_shape=[jax.ShapeDtypeStruct((nreq, N_HEADS, D_KV), F32),
                   jax.ShapeDtypeStruct((nreq, N_HEADS, LANES), F32),
                   jax.ShapeDtypeStruct(cache.shape, F32)],
        compiler_params=_params(("parallel",)),
        name="decode_attn",
    )(*args)


def _oproj_kernel(*refs, n_groups):
    o_refs = refs[:n_groups]
    lse_refs = refs[n_groups:2 * n_groups] if n_groups > 1 else ()
    x_ref, wo_ref, g_ref, out_ref = refs[-4:]
    if n_groups == 1:
        o = o_refs[0][...].astype(BF16)
    else:
        lses = [r[...] for r in lse_refs]
        m = functools.reduce(jnp.maximum, lses)
        es = [jnp.exp(l - m) for l in lses]
        den = functools.reduce(jnp.add, es)
        o = functools.reduce(jnp.add, [(e / den) * r[...].astype(F32)
                                       for e, r in zip(es, o_refs)]).astype(BF16)
    mix = jnp.dot(o, wo_ref[...], preferred_element_type=F32)
    out_ref[...] = x_ref[...] + _rms_scale(mix) * g_ref[...]


def _oproj(os_, lses, x, wo, g):
    t = x.shape[0]
    tm = min(ROW_TILE, t)
    row = pl.BlockSpec((tm, D_MODEL), lambda i: (i, 0))
    n_groups = len(os_)
    return pl.pallas_call(
        functools.partial(_oproj_kernel, n_groups=n_groups),
        grid=(t // tm,),
        in_specs=[row] * (n_groups + len(lses) + 1) + [_resident(wo.shape), _resident((1, D_MODEL))],
        out_specs=row,
        out_shape=jax.ShapeDtypeStruct((t, D_MODEL), F32),
        compiler_params=_params(("parallel",)),
        name="oproj",
    )(*os_, *lses, x, wo, g)


def _merge_groups(o_refs, lse_refs, unperm_refs, expand_ref, tm, dils):
    os_, lses, k = [], [], 0
    for gi, dil in enumerate(dils):
        if dil == 1:
            os_.append(o_refs[gi][0].astype(F32))
            lses.append(lse_refs[gi][0])
            continue
        sel = unperm_refs[k][...]
        k += 1
        n = tm // 2 // dil
        o_halves, lse_halves = [], []
        for hh in range(2):
            rows = slice(hh * n, (hh + 1) * n)
            o = jnp.concatenate([o_refs[gi][r, rows, :] for r in range(dil)], axis=0)
            lse = jnp.concatenate([lse_refs[gi][r, rows, :] for r in range(dil)], axis=0)
            o_halves.append(jnp.dot(sel, o, preferred_element_type=F32))
            lse_halves.append(_onehot_dot(sel, lse, 3))
        os_.append(jnp.concatenate(o_halves, axis=0))
        lses.append(jnp.concatenate(lse_halves, axis=0))
    m = functools.reduce(jnp.maximum, lses)
    es = [jnp.exp(l - m) for l in lses]
    den = functools.reduce(jnp.add, es)
    ws = [jnp.dot((e / den).astype(BF16), expand_ref[...], preferred_element_type=F32) for e in es]
    return functools.reduce(jnp.add, [w * o for w, o in zip(ws, os_)]).astype(BF16)


def _oproj_folded_kernel(*refs, dils):
    ng = len(dils)
    o_refs, lse_refs = refs[:ng], refs[ng:2 * ng]
    unperm_refs = refs[2 * ng:2 * ng + sum(d > 1 for d in dils)]
    expand_ref, x_ref, wo_ref, g_ref, out_ref = refs[-5:]
    o = _merge_groups(o_refs, lse_refs, unperm_refs, expand_ref, x_ref.shape[0], dils)
    mix = jnp.dot(o, wo_ref[...], preferred_element_type=F32)
    out_ref[...] = x_ref[...] + _rms_scale(mix) * g_ref[...]


def _oproj_folded(os_, lses, unperms, expand, x, wo, g):
    nb, seq, _ = x.shape
    tm = ROW_TILE
    dils = tuple(o.shape[1] for o in os_)
    folded = lambda dil, w: pl.BlockSpec((None, dil, tm // dil, w), lambda b, i: (b, 0, i, 0))
    row = pl.BlockSpec((None, tm, D_MODEL), lambda b, i: (b, i, 0))
    in_specs = ([folded(d, D_MODEL) for d in dils] + [folded(d, LANES) for d in dils]
                + [_resident(u.shape) for u in unperms]
                + [_resident(expand.shape), row, _resident(wo.shape), _resident((1, D_MODEL))])
    return pl.pallas_call(
        functools.partial(_oproj_folded_kernel, dils=dils),
        grid=(nb, seq // tm),
        in_specs=in_specs,
        out_specs=row,
        out_shape=jax.ShapeDtypeStruct((nb, seq, D_MODEL), F32),
        compiler_params=_params(("parallel", "parallel")),
        name="oproj_folded",
    )(*os_, *lses, *unperms, expand, x, wo, g)


def _rope_tabs(pos):
    half = ROT_DIM // 2
    pos = np.asarray(pos, np.float32)
    inv = np.float32(ROPE_THETA) ** (-np.arange(0, ROT_DIM, 2, dtype=np.float32) / np.float32(ROT_DIM))
    ang = pos[:, None] * inv[None, :]
    cos, sin = np.cos(ang), np.sin(ang)
    n = pos.shape[0]
    pad = np.zeros((n, HEAD_DIM - ROT_DIM), np.float32)
    zer = np.zeros((n, half), np.float32)
    c = np.concatenate([cos, cos, pad + 1.0], axis=1)
    s1 = np.concatenate([-sin, zer, pad], axis=1)
    s2 = np.concatenate([zer, sin, pad], axis=1)
    return tuple(jnp.asarray(np.tile(t, (1, LANES // HEAD_DIM)), F32) for t in (c, s1, s2))


def _band_bias():
    i = np.arange(QBLK)[:, None]
    j = np.arange(QBLK)[None, :]
    prev_ok = j >= i
    cur_ok = j <= i
    later = np.concatenate([prev_ok, cur_ok], axis=1)
    first = np.concatenate([np.zeros_like(prev_ok), cur_ok], axis=1)
    b = np.where(np.stack([first, later]), 0.0, NEG).astype(np.float32)
    return jnp.asarray(np.tile(b, (1, 2, 1)))


def _fold_perm(dil):
    half = ROW_TILE // 2
    n = half // dil
    dst = np.arange(half)
    src = (dst % n) * dil + dst // n
    return (src[:, None] == np.arange(half)[None, :]).astype(np.float32)


def _head_expand():
    sel = np.arange(LANES)[:, None] == _lse_lane(np.arange(D_MODEL)[None, :] // HEAD_DIM)
    return jnp.asarray(sel.astype(np.float32), BF16)


def _block_diag_q(q):
    r = q.shape[0]
    q4 = q.astype(F32).reshape(r, N_KV, 4, HEAD_DIM).transpose(0, 2, 1, 3)[:, :, None]
    eye = jnp.eye(N_KV, dtype=F32)[None, None, :, :, None]
    return (q4 * eye).reshape(r, N_HEADS, D_KV)


def _own_head(o):
    r = o.shape[0]
    o5 = o.reshape(r, 4, N_KV, N_KV, HEAD_DIM)
    d = jnp.einsum('rigkd,gk->rigd', o5, jnp.eye(N_KV, dtype=o.dtype))
    return d.transpose(0, 2, 1, 3).reshape(r, D_MODEL)


def _expand_lse(lse):
    r = lse.shape[0]
    l = lse[:, :, 0].reshape(r, 4, N_KV).transpose(0, 2, 1).reshape(r, N_HEADS)
    return jnp.repeat(l, HEAD_DIM, axis=1)


def _window_minor(cache):
    r, w = cache.shape[:2]
    return cache.transpose(0, 2, 3, 4, 1).reshape(r, 2 * D_KV, w)


def _window_major(cache_t):
    r, _, w = cache_t.shape
    return cache_t.reshape(r, 2, N_KV, HEAD_DIM, w).transpose(0, 4, 1, 2, 3)


def kernel(x_prompt, x_sample, cache_a_kv, cache_b_kv_w128, cache_b_kv_w512, cache_b_kv_w2048,
           norm_g, w_ffn_gu, w_ffn_dn, w_qkv_a, sink_a, w_o_a, g_kv_b, w_kv_b, w_q_b, w_o_b):
    nb, seq, _ = x_prompt.shape
    nreq = x_sample.shape[0]
    tp = nb * seq
    b_caches = (cache_b_kv_w128, cache_b_kv_w512, cache_b_kv_w2048)

    def gain(l, k):
        return norm_g[l, k].reshape(1, D_MODEL)

    ffn_w = {(1, 1): _cast_weights(w_ffn_gu, w_ffn_dn, 1, 1)}

    def ffn(x, l, k):
        if (l, k) in ffn_w:
            return _ffn(x, gain(l, 4 * k), *ffn_w[l, k], gain(l, 4 * k + 1))
        y, *ffn_w[l, k] = _ffn_cast(x, gain(l, 4 * k), w_ffn_gu, w_ffn_dn, gain(l, 4 * k + 1), l, k)
        return y

    bias = _band_bias()
    tabs_p = _rope_tabs(np.arange(seq))
    tabs_s = _rope_tabs(np.full((nreq,), PAST_LEN))
    perms = {dil: _fold_perm(dil) for _, dil in B_GROUPS if dil > 1}

    wqkv_a = w_qkv_a.astype(BF16)
    wq_b = w_q_b.astype(BF16)
    wkv_b = w_kv_b.astype(BF16)
    col3 = lambda w, j: pl.BlockSpec((None, D_MODEL, w), lambda *_: (0, 0, j), pipeline_mode=pl.Buffered(1))
    col2 = lambda w, j: pl.BlockSpec((D_MODEL, w), lambda *_: (0, j), pipeline_mode=pl.Buffered(1))
    wo_a = w_o_a[0].astype(BF16)
    wo_b = w_o_b[0].astype(BF16)
    g_kv = g_kv_b.reshape(1, D_MODEL)
    sink = sink_a[0].astype(F32)
    sink_rows = sink.reshape(N_KV, 4).T.reshape(N_HEADS, 1)

    assert all(win == dil * LANES for win, dil in B_GROUPS + ((WIN_A, 1),))
    xs = x_sample.reshape(nreq, D_MODEL)
    xs = ffn(xs, 0, 0)
    q, new, newt = _proj_decode(xs, gain(0, 2), gain(0, 2), wqkv_a, col3(D_MODEL, 0),
                                wqkv_a, col3(2 * D_KV, 2), tabs_s)
    o, _, cache_a_new = _decode_attention(_block_diag_q(q), _window_minor(cache_a_kv[0]),
                                          new.reshape(nreq, 1, 2 * D_KV), newt, sink_rows)
    xs = _oproj([_own_head(o)], [], xs, wo_a, gain(0, 3))
    xs = ffn(xs, 0, 1)
    xs = ffn(xs, 1, 0)
    jobs = []
    for gi, (win, dil) in enumerate(B_GROUPS):
        q, new, newt = _proj_decode(xs, gain(1, 2), g_kv, wq_b, col3(D_MODEL, gi),
                                    wkv_b, col2(2 * D_KV, gi), tabs_s)
        jobs.append((_block_diag_q(q), _window_minor(b_caches[gi]), new.reshape(nreq, 1, 2 * D_KV), newt))

    dec_o, dec_lse, dec_caches = [], [], None

    def ffn_with_decode(x, l, k, mixer=None):
        nonlocal dec_caches
        req0 = len(dec_o) * (x.shape[0] // ROW_TILE)
        x, o, lse, dec_caches = _ffn_decode(x, gain(l, 4 * k), *ffn_w[l, k], gain(l, 4 * k + 1),
                                            jobs, req0, dec_caches, mixer)
        dec_o.append(o)
        dec_lse.append(lse)
        return x

    x = x_prompt.reshape(tp, D_MODEL)
    x = ffn_with_decode(x, 0, 0)
    q, kv_a, k2, v2 = _proj(x.reshape(nb, seq, D_MODEL), gain(0, 2), gain(0, 2),
                            wqkv_a, col3(D_MODEL, 0), wqkv_a, col3(2 * D_KV, 2), tabs_p, None,
                            dil=1, win=WIN_A)
    o, = _band_attention(q, k2, v2, bias, sink, has_lse=False)
    x = ffn_with_decode(x, 0, 1, ([o], [], [], None, wo_a, gain(0, 3)))
    x = ffn_with_decode(x, 1, 0).reshape(nb, seq, D_MODEL)
    os_, lses, kv_b = [], [], []
    for gi, (win, dil) in enumerate(B_GROUPS):
        perm = jnp.asarray(perms[dil], BF16) if dil > 1 else None
        q, kvc, k2, v2 = _proj(x, gain(1, 2), g_kv, wq_b, col3(D_MODEL, gi), wkv_b, col2(2 * D_KV, gi),
                               tabs_p, perm, dil=dil, win=win)
        o, lse = _band_attention(q, k2, v2, bias, None, has_lse=True)
        os_.append(o)
        lses.append(lse)
        kv_b.append(kvc.reshape(nb, win, 2, N_KV, HEAD_DIM))
    unperms = [jnp.asarray(perms[d].T, BF16) for _, d in B_GROUPS if d > 1]
    x = _oproj_folded(os_, lses, unperms, _head_expand(), x, wo_b, gain(1, 3))
    y_p = ffn_with_decode(x.reshape(tp, D_MODEL), 1, 1).reshape(nb, seq, D_MODEL)
    a_p = kv_a.reshape(1, nb, WIN_A, 2, N_KV, HEAD_DIM)

    assert len(dec_o) * (tp // ROW_TILE) == nreq
    os_ = [_own_head(jnp.concatenate([o[gi] for o in dec_o], axis=0)) for gi in range(len(B_GROUPS))]
    lses = [_expand_lse(jnp.concatenate([l[gi] for l in dec_lse], axis=0)) for gi in range(len(B_GROUPS))]
    b_s = [_window_major(c) for c in dec_caches]
    xs = _oproj(os_, lses, xs, wo_b, gain(1, 3))
    y_s = ffn(xs, 1, 1).reshape(nreq, 1, D_MODEL)
    a_s = _window_major(cache_a_new)[None]

    return (y_p, y_s, a_p, kv_b[0], kv_b[1], kv_b[2], a_s, b_s[0], b_s[1], b_s[2])
```

```python
import functools

import jax
import jax.numpy as jnp
import numpy as np
from jax.experimental import pallas as pl
from jax.experimental.pallas import tpu as pltpu

F32 = jnp.float32
BF16 = jnp.bfloat16

D_MODEL = 1024
HEAD_DIM = 64
ROT_DIM = HEAD_DIM // 4
ROPE_THETA = 500000.0
N_HEADS = 16
N_KV = 4
D_KV = N_KV * HEAD_DIM
D_FF = 2816
EPS = 1e-6
PAST_LEN = 8192
WIN_A = 128
B_GROUPS = ((128, 1), (512, 4), (2048, 16))

LANES = 128
FF_CHUNK = 512
ROW_TILE = 512
QBLK = 128
NEG = -1e30
LOG2E = 1.4426950408889634
LN2 = 0.6931471805599453
Q_SCALE = HEAD_DIM ** -0.5 * LOG2E
MAX_QBLKS = 16
VMEM_LIMIT = 56 * 1024 * 1024
VMEM_LIMIT_FUSED = 60 * 1024 * 1024
DECODE_BLOCK_BYTES = 4 * 1024 * 1024
SHIFT_ROWS = 128
DECODE_UNROLL = 4


def _params(sem, vmem_limit=VMEM_LIMIT):
    return pltpu.CompilerParams(dimension_semantics=sem, vmem_limit_bytes=vmem_limit)


def _resident(shape):
    nd = len(shape)
    return pl.BlockSpec(shape, lambda *_: (0,) * nd, pipeline_mode=pl.Buffered(1))


def _rms_scale(x):
    return x * jax.lax.rsqrt(jnp.mean(x * x, axis=-1, keepdims=True) + EPS)


def _split_bf16(x, parts):
    out = []
    for _ in range(parts - 1):
        hi = x.astype(BF16)
        out.append(hi)
        x = x - hi.astype(F32)
    out.append(x.astype(BF16))
    return out


def _onehot_dot(sel, x, parts):
    return functools.reduce(
        jnp.add, [jnp.dot(sel, p, preferred_element_type=F32) for p in _split_bf16(x, parts)])


def _ffn_kernel(x_ref, gpre_ref, wg_ref, wu_ref, wd_ref, gpost_ref, o_ref, side_work=(), mix=None):
    x = x_ref[...]
    if mix is not None:
        x = x + mix()
    xn = (_rms_scale(x) * gpre_ref[...]).astype(BF16)
    acc = None
    bounds = list(range(0, D_FF, FF_CHUNK)) + [D_FF]
    n_chunks = len(bounds) - 1
    for c in range(n_chunks):
        sl = slice(bounds[c], bounds[c + 1])
        gate = jnp.dot(xn, wg_ref[:, sl], preferred_element_type=F32)
        up = jnp.dot(xn, wu_ref[:, sl], preferred_element_type=F32)
        act = (gate * jax.nn.sigmoid(gate) * up).astype(BF16)
        part = jnp.dot(act, wd_ref[sl, :], preferred_element_type=F32)
        acc = part if acc is None else acc + part
        for work in side_work[c::n_chunks]:
            work()
    o_ref[...] = x + 0.5 * (_rms_scale(acc) * gpost_ref[...])


def _ffn_specs(t, wg, wu, wd):
    tm = min(ROW_TILE, t)
    row = pl.BlockSpec((tm, D_MODEL), lambda i: (i, 0))
    in_specs = [row, _resident((1, D_MODEL)), _resident(wg.shape), _resident(wu.shape),
                _resident(wd.shape), _resident((1, D_MODEL))]
    return t // tm, in_specs, row


def _ffn(x, gpre, wg, wu, wd, gpost):
    t = x.shape[0]
    steps, in_specs, row = _ffn_specs(t, wg, wu, wd)
    return pl.pallas_call(
        _ffn_kernel,
        grid=(steps,),
        in_specs=in_specs,
        out_specs=row,
        out_shape=jax.ShapeDtypeStruct((t, D_MODEL), F32),
        compiler_params=_params(("parallel",)),
        name="ffn",
    )(x, gpre, wg, wu, wd, gpost)


CAST_CHUNK = 256


def _ffn_cast_kernel(x_ref, gpre_ref, wg_ref, wu_ref, wd_ref, gpost_ref,
                     y_ref, wg_out, wu_out, wd_out, xn_ref, acc_ref):
    c = pl.program_id(0)

    @pl.when(c == 0)
    def _():
        xn_ref[...] = (_rms_scale(x_ref[...]) * gpre_ref[...]).astype(BF16)
        acc_ref[...] = jnp.zeros_like(acc_ref)

    wg, wu, wd = wg_ref[...].astype(BF16), wu_ref[...].astype(BF16), wd_ref[...].astype(BF16)
    wg_out[...], wu_out[...], wd_out[...] = wg, wu, wd
    xn = xn_ref[...]
    gate = jnp.dot(xn, wg, preferred_element_type=F32)
    up = jnp.dot(xn, wu, preferred_element_type=F32)
    act = (gate * jax.nn.sigmoid(gate) * up).astype(BF16)
    acc_ref[...] += jnp.dot(act, wd, preferred_element_type=F32)

    @pl.when(c == pl.num_programs(0) - 1)
    def _():
        y_ref[...] = x_ref[...] + 0.5 * (_rms_scale(acc_ref[...]) * gpost_ref[...])


def _cast_kernel(wg_ref, wu_ref, wd_ref, wg_out, wu_out, wd_out):
    wg_out[...] = wg_ref[...].astype(BF16)
    wu_out[...] = wu_ref[...].astype(BF16)
    wd_out[...] = wd_ref[...].astype(BF16)


def _cast_specs(l, k):
    n = D_FF // CAST_CHUNK
    in_specs = [pl.BlockSpec((None, None, D_MODEL, CAST_CHUNK), lambda c: (l, k, 0, c)),
                pl.BlockSpec((None, None, D_MODEL, CAST_CHUNK), lambda c: (l, k, 0, n + c)),
                pl.BlockSpec((None, None, CAST_CHUNK, D_MODEL), lambda c: (l, k, c, 0))]
    out_specs = [pl.BlockSpec((D_MODEL, CAST_CHUNK), lambda c: (0, c)),
                 pl.BlockSpec((D_MODEL, CAST_CHUNK), lambda c: (0, c)),
                 pl.BlockSpec((CAST_CHUNK, D_MODEL), lambda c: (c, 0))]
    out_shape = [jax.ShapeDtypeStruct((D_MODEL, D_FF), BF16),
                 jax.ShapeDtypeStruct((D_MODEL, D_FF), BF16),
                 jax.ShapeDtypeStruct((D_FF, D_MODEL), BF16)]
    return n, in_specs, out_specs, out_shape


def _cast_weights(w_gu, w_dn, l, k):
    n, in_specs, out_specs, out_shape = _cast_specs(l, k)
    return pl.pallas_call(
        _cast_kernel, grid=(n,), in_specs=in_specs, out_specs=out_specs, out_shape=out_shape,
        compiler_params=_params(("parallel",)), name="cast_weights",
    )(w_gu, w_gu, w_dn)


def _ffn_cast(x, gpre, w_gu, w_dn, gpost, l, k):
    t = x.shape[0]
    n, w_in, w_out, w_shape = _cast_specs(l, k)
    full = pl.BlockSpec((t, D_MODEL), lambda c: (0, 0))
    vec = pl.BlockSpec((1, D_MODEL), lambda c: (0, 0))
    return pl.pallas_call(
        _ffn_cast_kernel,
        grid=(n,),
        in_specs=[full, vec, *w_in, vec],
        out_specs=[full, *w_out],
        out_shape=[jax.ShapeDtypeStruct((t, D_MODEL), F32), *w_shape],
        scratch_shapes=[pltpu.VMEM((t, D_MODEL), BF16), pltpu.VMEM((t, D_MODEL), F32)],
        compiler_params=_params(("arbitrary",)),
        name="ffn_cast",
    )(x, gpre, w_gu, w_gu, w_dn, gpost)


def _ffn_decode_kernel(*refs, n_jobs, req0, has_prev, mix_dils):
    per_job = 5 if has_prev else 4
    ffn_in = refs[:6]
    n_mix = 0
    mix = None
    if mix_dils is not None:
        ng = len(mix_dils)
        n_mix = 2 * ng + sum(d > 1 for d in mix_dils) + 3 if ng > 1 else 3
        mix_refs = refs[6:6 + n_mix]
        wo_ref, g_ref = mix_refs[-2:]

        def mix():
            if ng == 1:
                o = mix_refs[0][...]
            else:
                o = _merge_groups(mix_refs[:ng], mix_refs[ng:2 * ng], mix_refs[2 * ng:-3], mix_refs[-3],
                                  ffn_in[0].shape[0], mix_dils)
            return _rms_scale(jnp.dot(o, wo_ref[...], preferred_element_type=F32)) * g_ref[...]

    first = 6 + n_mix
    job_in = [refs[first + j * per_job:first + (j + 1) * per_job] for j in range(n_jobs)]
    outs = refs[first + n_jobs * per_job:]
    req = req0 + pl.program_id(0)
    side_work = []
    for j in range(n_jobs):
        qbd_ref, cache_ref, new_ref, newt_ref = job_in[j][:4]
        o_ref, lse_ref, cout_ref = outs[1 + 3 * j:4 + 3 * j]
        side_work.append(functools.partial(
            _decode_attend, qbd_ref, cache_ref, 0, new_ref, None, o_ref, lse_ref))
        for r0 in range(0, 2 * D_KV, SHIFT_ROWS):
            side_work.append(functools.partial(
                _shift_rows, cache_ref, 0, newt_ref, req, cout_ref, r0, r0 + SHIFT_ROWS))
    _ffn_kernel(*ffn_in, outs[0], side_work=side_work, mix=mix)


def _ffn_decode(x, gpre, wg, wu, wd, gpost, jobs, req0, prev_caches, mixer=None):
    t = x.shape[0]
    steps, in_specs, row = _ffn_specs(t, wg, wu, wd)
    has_prev = prev_caches is not None
    args = [x, gpre, wg, wu, wd, gpost]
    mix_dils = None
    if mixer is not None:
        os_, lses, unperms, expand, wo, g = mixer
        mix_dils = tuple(o.shape[1] for o in os_)
        tiles = os_[0].shape[2] * mix_dils[0] // ROW_TILE
        folded = lambda dil, w: pl.BlockSpec((None, dil, ROW_TILE // dil, w),
                                             lambda i: (i // tiles, 0, i % tiles, 0))
        if len(os_) == 1:
            in_specs += [pl.BlockSpec((None, None, ROW_TILE, D_MODEL), lambda i: (i // tiles, 0, i % tiles, 0))]
            args += [os_[0]]
        else:
            in_specs += ([folded(d, D_MODEL) for d in mix_dils] + [folded(d, LANES) for d in mix_dils]
                         + [_resident(u.shape) for u in unperms] + [_resident(expand.shape)])
            args += [*os_, *lses, *unperms, expand]
        in_specs += [_resident(wo.shape), _resident((1, D_MODEL))]
        args += [wo, g]
    out_specs, out_shape, aliases = [row], [jax.ShapeDtypeStruct((t, D_MODEL), F32)], {}
    for j, (qbd, cache, new, newt) in enumerate(jobs):
        _, feat, win = cache.shape
        req = lambda *tail: pl.BlockSpec((1,) + tail, lambda i: (req0 + i,) + (0,) * len(tail))
        loc = lambda *tail: pl.BlockSpec((1,) + tail, lambda i: (i,) + (0,) * len(tail))
        in_specs += [req(N_HEADS, D_KV), req(feat, win), req(1, 2 * D_KV), _resident(newt.shape)]
        args += [qbd, cache, new, newt]
        if has_prev:
            aliases[len(args)] = 3 + 3 * j
            in_specs.append(pl.BlockSpec(memory_space=pl.ANY))
            args.append(prev_caches[j])
        out_specs += [loc(N_HEADS, D_KV), loc(N_HEADS, LANES), req(feat, win)]
        out_shape += [jax.ShapeDtypeStruct((steps, N_HEADS, D_KV), F32),
                      jax.ShapeDtypeStruct((steps, N_HEADS, LANES), F32),
                      jax.ShapeDtypeStruct(cache.shape, F32)]
    res = pl.pallas_call(
        functools.partial(_ffn_decode_kernel, n_jobs=len(jobs), req0=req0, has_prev=has_prev,
                          mix_dils=mix_dils),
        grid=(steps,),
        in_specs=in_specs, out_specs=out_specs, out_shape=out_shape,
        input_output_aliases=aliases,
        compiler_params=_params(("arbitrary",), VMEM_LIMIT_FUSED),
        name="ffn_decode",
    )(*args)
    return res[0], list(res[1::3]), list(res[2::3]), list(res[3::3])


def _project(x_ref, gq_ref, gk_ref, wq_ref, wkv_ref, rc_ref, rs1_ref, rs2_ref):
    xh = _rms_scale(x_ref[...])
    hq = (xh * gq_ref[...]).astype(BF16)
    hk = (xh * gk_ref[...]).astype(BF16)
    rc, rs1, rs2 = rc_ref[...], rs1_ref[...], rs2_ref[...]

    def rope(t):
        return t * rc + pltpu.roll(t, LANES - ROT_DIM // 2, 1) * rs1 + pltpu.roll(t, ROT_DIM // 2, 1) * rs2

    q = jnp.dot(hq, wq_ref[...], preferred_element_type=F32)
    qs = [(rope(q[:, c * LANES:(c + 1) * LANES]) * Q_SCALE).astype(BF16)
          for c in range(D_MODEL // LANES)]
    kv = jnp.dot(hk, wkv_ref[...], preferred_element_type=F32)
    kvs = []
    for c in range(2 * D_KV // LANES):
        t = kv[:, c * LANES:(c + 1) * LANES]
        kvs.append(rope(t) if c < D_KV // LANES else t)
    return qs, kvs


def _dup_heads(t):
    lo = jax.lax.broadcasted_iota(jnp.int32, t.shape, 1) < HEAD_DIM
    sw = pltpu.roll(t, HEAD_DIM, 1)
    return [jnp.where(lo, t, sw).astype(BF16), jnp.where(lo, sw, t).astype(BF16)]


def _proj_kernel(*refs, dil, cache_rows):
    ins = refs[:8]
    perm_ref = refs[8] if dil > 1 else None
    q_ref, kv_ref, k2_ref, v2_ref = refs[-4:]
    tm = ins[0].shape[0]
    qs, kvs = _project(*ins)
    for c, t in enumerate(kvs):
        kv_ref[:, c * LANES:(c + 1) * LANES] = t[tm - cache_rows:, :]
    nk = D_KV // LANES
    if dil == 1:
        q_ref[0] = jnp.concatenate(qs, axis=1)
        k2_ref[0] = jnp.concatenate([d for t in kvs[:nk] for d in _dup_heads(t)], axis=1)
        v2_ref[0] = jnp.concatenate([d for t in kvs[nk:] for d in _dup_heads(t)], axis=1)
        return
    cols = jnp.concatenate(qs + [t.astype(BF16) for t in kvs], axis=1)
    half = tm // 2
    n = half // dil
    for hh in range(2):
        f = jnp.dot(perm_ref[...], cols[hh * half:(hh + 1) * half, :], preferred_element_type=F32)
        qf = f[:, :D_MODEL].astype(BF16)
        kf = jnp.concatenate([d for c in range(nk)
                              for d in _dup_heads(f[:, D_MODEL + c * LANES:D_MODEL + (c + 1) * LANES])], axis=1)
        vf = jnp.concatenate([d for c in range(nk, 2 * nk)
                              for d in _dup_heads(f[:, D_MODEL + c * LANES:D_MODEL + (c + 1) * LANES])], axis=1)
        for r in range(dil):
            dst = slice(hh * n, (hh + 1) * n)
            q_ref[r, dst, :] = qf[r * n:(r + 1) * n, :]
            k2_ref[r, dst, :] = kf[r * n:(r + 1) * n, :]
            v2_ref[r, dst, :] = vf[r * n:(r + 1) * n, :]


def _proj(x, gq, gk, wq, wq_spec, wkv, wkv_spec, tabs, perm, *, dil, win):
    nb, seq, _ = x.shape
    tm = ROW_TILE if dil > 1 else 2 * ROW_TILE
    nt = seq // tm
    cache_rows = min(win, tm)
    first_cache_tile = nt - win // cache_rows
    n = tm // dil
    tab = pl.BlockSpec((tm, LANES), lambda b, i: (i, 0))
    folded = lambda w: pl.BlockSpec((None, dil, n, w), lambda b, i: (b, 0, i, 0))
    in_specs = [pl.BlockSpec((None, tm, D_MODEL), lambda b, i: (b, i, 0)),
                _resident((1, D_MODEL)), _resident((1, D_MODEL)),
                wq_spec, wkv_spec, tab, tab, tab]
    args = [x, gq, gk, wq, wkv, *tabs]
    if dil > 1:
        in_specs.append(_resident(perm.shape))
        args.append(perm)
    return pl.pallas_call(
        functools.partial(_proj_kernel, dil=dil, cache_rows=cache_rows),
        grid=(nb, nt),
        in_specs=in_specs,
        out_specs=[folded(D_MODEL),
                   pl.BlockSpec((None, cache_rows, 2 * D_KV),
                                lambda b, i: (b, jnp.maximum(i - first_cache_tile, 0), 0)),
                   folded(2 * D_KV), folded(2 * D_KV)],
        out_shape=[jax.ShapeDtypeStruct((nb, dil, seq // dil, D_MODEL), BF16),
                   jax.ShapeDtypeStruct((nb, win, 2 * D_KV), F32),
                   jax.ShapeDtypeStruct((nb, dil, seq // dil, 2 * D_KV), BF16),
                   jax.ShapeDtypeStruct((nb, dil, seq // dil, 2 * D_KV), BF16)],
        compiler_params=_params(("parallel", "arbitrary")),
        name="proj",
    )(*args)


def _proj_decode_kernel(*refs):
    q_ref, new_ref, newt_ref = refs[-3:]
    qs, kvs = _project(*refs[:8])
    for c, t in enumerate(qs):
        q_ref[:, c * LANES:(c + 1) * LANES] = t
    for c, t in enumerate(kvs):
        new_ref[:, c * LANES:(c + 1) * LANES] = t
        newt_ref[c * LANES:(c + 1) * LANES, :] = t.T


def _proj_decode(x, gq, gk, wq, wq_spec, wkv, wkv_spec, tabs):
    nreq = x.shape[0]
    full = lambda w: pl.BlockSpec((nreq, w), lambda i: (0, 0))
    return pl.pallas_call(
        _proj_decode_kernel,
        grid=(1,),
        in_specs=[full(D_MODEL), _resident((1, D_MODEL)), _resident((1, D_MODEL)),
                  wq_spec, wkv_spec, full(LANES), full(LANES), full(LANES)],
        out_specs=[full(D_MODEL), full(2 * D_KV), pl.BlockSpec((2 * D_KV, nreq), lambda i: (0, 0))],
        out_shape=[jax.ShapeDtypeStruct((nreq, D_MODEL), BF16),
                   jax.ShapeDtypeStruct((nreq, 2 * D_KV), F32),
                   jax.ShapeDtypeStruct((2 * D_KV, nreq), F32)],
        compiler_params=_params(("arbitrary",)),
        name="proj_decode",
    )(x, gq, gk, wq, wkv, *tabs)


def _lse_lane(head):
    return (head % 2) * HEAD_DIM + head - head % 2


def _band_kernel(*refs, has_sink, has_lse, nq, nr):
    refs = list(refs)
    sink_ref = refs.pop(0) if has_sink else None
    q_ref, kp_ref, kc_ref, vp_ref, vc_ref, bias_ref, o_ref = refs[:7]
    lse_ref = refs[7] if has_lse else None

    lo = jax.lax.broadcasted_iota(jnp.int32, (2 * QBLK, LANES), 1) < HEAD_DIM
    lane_q = jax.lax.broadcasted_iota(jnp.int32, (QBLK, LANES), 1)
    top = jax.lax.broadcasted_iota(jnp.int32, (2 * QBLK, 1), 0) < QBLK
    zero = jnp.zeros((), BF16)
    krow_lo = jax.lax.broadcasted_iota(jnp.int32, (4 * QBLK, LANES), 0) < 2 * QBLK
    klane_lo = jax.lax.broadcasted_iota(jnp.int32, (4 * QBLK, LANES), 1) < HEAD_DIM
    ones_sel = jnp.where(krow_lo == klane_lo, 1.0, 0.0).astype(BF16)
    first_step = pl.program_id(2) == 0

    for rr in range(nr):
        q_r, kp_r, kc_r, vp_r, vc_r, o_r = (ref.at[rr] for ref in (q_ref, kp_ref, kc_ref, vp_ref, vc_ref, o_ref))
        lse_r = lse_ref.at[rr] if has_lse else None
        for t in range(nq):
            rows = slice(t * QBLK, (t + 1) * QBLK)
            bias = bias_ref[1] if t else bias_ref[jnp.where(first_step, 0, 1)]
            lse_tile = jnp.zeros((QBLK, LANES), F32)
            for h in range(N_KV):
                ksl = slice(h * LANES, (h + 1) * LANES)
                kprev = kc_r[(t - 1) * QBLK:t * QBLK, ksl] if t else kp_r[:, ksl]
                vprev = vc_r[(t - 1) * QBLK:t * QBLK, ksl] if t else vp_r[:, ksl]
                k2 = jnp.concatenate([kprev, kc_r[rows, ksl]], axis=0)
                v2 = jnp.concatenate([vprev, vc_r[rows, ksl]], axis=0)
                kbd = jnp.concatenate([jnp.where(lo, k2, zero), jnp.where(lo, zero, k2)], axis=0)
                vbd = jnp.concatenate([jnp.where(lo, v2, zero), jnp.where(lo, zero, v2)], axis=0)
                vext = jnp.concatenate([vbd, ones_sel], axis=1)
                qq = jnp.concatenate([q_r[rows, (2 * h) * LANES:(2 * h + 1) * LANES],
                                      q_r[rows, (2 * h + 1) * LANES:(2 * h + 2) * LANES]], axis=0)
                s = jax.lax.dot_general(qq, kbd, (((1,), (1,)), ((), ())),
                                        preferred_element_type=F32)
                ps, ms, sks = [], [], []
                for c in range(2):
                    sc = s[:, c * 2 * QBLK:(c + 1) * 2 * QBLK] + bias
                    m = jnp.max(sc, axis=-1, keepdims=True)
                    if has_sink:
                        sk = jnp.where(top, sink_ref[4 * h + c], sink_ref[4 * h + 2 + c]) * LOG2E
                        m = jnp.maximum(m, sk)
                        sks.append(sk)
                    ps.append(jnp.exp2(sc - m).astype(BF16))
                    ms.append(m)
                ov = jnp.dot(jnp.concatenate(ps, axis=1), vext, preferred_element_type=F32)
                l = ov[:, LANES:]
                if has_sink:
                    l = l + jnp.where(lo, jnp.exp2(sks[0] - ms[0]), jnp.exp2(sks[1] - ms[1]))
                o = ov[:, :LANES] / l
                o_r[rows, (2 * h) * LANES:(2 * h + 1) * LANES] = o[:QBLK].astype(o_ref.dtype)
                o_r[rows, (2 * h + 1) * LANES:(2 * h + 2) * LANES] = o[QBLK:].astype(o_ref.dtype)
                if has_lse:
                    lse = jnp.where(lo, ms[0], ms[1]) * LN2 + jnp.log(l)
                    for j in range(2):
                        pair = (lane_q == _lse_lane(4 * h + 2 * j)) | (lane_q == _lse_lane(4 * h + 2 * j + 1))
                        lse_tile = jnp.where(pair, lse[j * QBLK:(j + 1) * QBLK], lse_tile)
            if has_lse:
                lse_r[rows, :] = lse_tile


def _band_attention(q, k2, v2, bias, sink, *, has_lse):
    nb, dil, length, _ = q.shape
    nq = min(MAX_QBLKS, length // QBLK)
    nr = min(dil, MAX_QBLKS // nq)
    grid = (nb, dil // nr, length // (nq * QBLK))
    has_sink = sink is not None
    cur = lambda w: pl.BlockSpec((None, nr, nq * QBLK, w), lambda b, r, i: (b, r, i, 0))
    prev = lambda w: pl.BlockSpec((None, nr, QBLK, w),
                                  lambda b, r, i: (b, r, jnp.maximum(i * nq - 1, 0), 0))
    in_specs = [cur(D_MODEL), prev(2 * D_KV), cur(2 * D_KV), prev(2 * D_KV), cur(2 * D_KV),
                _resident(bias.shape)]
    args = [q, k2, k2, v2, v2, bias]
    if has_sink:
        in_specs = [pl.BlockSpec(memory_space=pltpu.SMEM)] + in_specs
        args = [sink] + args
    out_specs = [cur(D_MODEL)]
    out_shape = [jax.ShapeDtypeStruct((nb, dil, length, D_MODEL), BF16)]
    if has_lse:
        out_specs.append(cur(LANES))
        out_shape.append(jax.ShapeDtypeStruct((nb, dil, length, LANES), F32))
    return pl.pallas_call(
        functools.partial(_band_kernel, has_sink=has_sink, has_lse=has_lse, nq=nq, nr=nr),
        grid=grid, in_specs=in_specs, out_specs=out_specs, out_shape=out_shape,
        compiler_params=_params(("parallel", "parallel", "parallel")),
        name="band_attn",
    )(*args)


def _tap_columns(cache_ref, b, r0):
    dil = cache_ref.shape[2] // LANES
    if dil == 1:
        return cache_ref[b, r0:r0 + D_KV, :]
    lane = jax.lax.broadcasted_iota(jnp.int32, (D_KV, LANES), 1)
    is_tap = (lane & (dil - 1)) == 0
    acc = None
    for t in range(dil):
        part = jnp.where(is_tap, cache_ref[b, r0:r0 + D_KV, t * LANES:(t + 1) * LANES], 0.0)
        if t:
            part = pltpu.roll(part, t, 1)
        acc = part if acc is None else acc + part
    return acc


def _decode_attend(qbd_ref, cache_ref, b, new_ref, sink_ref, o_ref, lse_ref):
    qb = qbd_ref[b]
    new = new_ref[b]
    kn = new[:, 0:D_KV]
    vn = new[:, D_KV:2 * D_KV]
    s = jnp.dot(qb.astype(BF16), _tap_columns(cache_ref, b, 0).astype(BF16),
                preferred_element_type=F32)
    sn = jnp.sum(qb * kn, axis=-1, keepdims=True)
    m = jnp.maximum(jnp.max(s, axis=-1, keepdims=True), sn)
    if sink_ref is not None:
        sk = sink_ref[...] * LOG2E
        m = jnp.maximum(m, sk)
    p = jnp.exp2(s - m)
    pn = jnp.exp2(sn - m)
    l = jnp.sum(p, axis=-1, keepdims=True) + pn
    if sink_ref is not None:
        l = l + jnp.exp2(sk - m)
    o = jax.lax.dot_general(p.astype(BF16), _tap_columns(cache_ref, b, D_KV).astype(BF16),
                            (((1,), (1,)), ((), ())), preferred_element_type=F32) + pn * vn
    o_ref[b] = o / l
    lse_ref[b] = jnp.broadcast_to(m * LN2 + jnp.log(l), (N_HEADS, LANES))


def _shift_rows(cache_ref, b, newt_ref, req, cout_ref, r0, r1):
    win = cache_ref.shape[2]
    lane = jax.lax.broadcasted_iota(jnp.int32, (r1 - r0, LANES), 1)
    shifted = pltpu.roll(cache_ref[b, r0:r1, :], win - 1, 1)
    newcol = pltpu.roll(newt_ref[r0:r1, :], LANES - 1 - req, 1)
    cout_ref[b, r0:r1, :] = shifted
    cout_ref[b, r0:r1, win - LANES:win] = jnp.where(lane == LANES - 1, newcol, shifted[:, win - LANES:win])


def _decode_kernel(*refs, has_sink, rb):
    refs = list(refs)
    sink_ref = refs.pop(0) if has_sink else None
    qbd_ref, cache_ref, new_ref, newt_ref, o_ref, lse_ref, cout_ref = refs
    base = pl.program_id(0) * rb

    def body(b, carry):
        _decode_attend(qbd_ref, cache_ref, b, new_ref, sink_ref, o_ref, lse_ref)
        _shift_rows(cache_ref, b, newt_ref, base + b, cout_ref, 0, 2 * D_KV)
        return carry

    jax.lax.fori_loop(0, rb, body, 0, unroll=min(rb, DECODE_UNROLL))


def _decode_attention(qbd, cache, new, newt, sink_rows):
    nreq, feat, win = cache.shape
    rb = max(1, DECODE_BLOCK_BYTES // (feat * win * 4))
    has_sink = sink_rows is not None
    req = lambda *tail: pl.BlockSpec((rb,) + tail, lambda i: (i,) + (0,) * len(tail))
    in_specs = [req(N_HEADS, D_KV), req(feat, win), req(1, 2 * D_KV),
                pl.BlockSpec((feat, nreq), lambda i: (0, 0))]
    args = [qbd, cache, new, newt]
    if has_sink:
        in_specs = [pl.BlockSpec((N_HEADS, 1), lambda i: (0, 0))] + in_specs
        args = [sink_rows] + args
    return pl.pallas_call(
        functools.partial(_decode_kernel, has_sink=has_sink, rb=rb),
        grid=(nreq // rb,), in_specs=in_specs,
        out_specs=[req(N_HEADS, D_KV), req(N_HEADS, LANES), req(feat, win)],
        out_shape=[jax.ShapeDtypeStruct((nreq, N_HEADS, D_KV), F32),
                   jax.ShapeDtypeStruct((nreq, N_HEADS, LANES), F32),
                   jax.ShapeDtypeStruct(cache.shape, F32)],
        compiler_params=_params(("parallel",)),
        name="decode_attn",
    )(*args)


def _oproj_kernel(*refs, n_groups):
    o_refs = refs[:n_groups]
    lse_refs = refs[n_groups:2 * n_groups] if n_groups > 1 else ()
    x_ref, wo_ref, g_ref, out_ref = refs[-4:]
    if n_groups == 1:
        o = o_refs[0][...].astype(BF16)
    else:
        lses = [r[...] for r in lse_refs]
        m = functools.reduce(jnp.maximum, lses)
        es = [jnp.exp(l - m) for l in lses]
        den = functools.reduce(jnp.add, es)
        o = functools.reduce(jnp.add, [(e / den) * r[...].astype(F32)
                                       for e, r in zip(es, o_refs)]).astype(BF16)
    mix = jnp.dot(o, wo_ref[...], preferred_element_type=F32)
    out_ref[...] = x_ref[...] + _rms_scale(mix) * g_ref[...]


def _oproj(os_, lses, x, wo, g):
    t = x.shape[0]
    tm = min(ROW_TILE, t)
    row = pl.BlockSpec((tm, D_MODEL), lambda i: (i, 0))
    n_groups = len(os_)
    return pl.pallas_call(
        functools.partial(_oproj_kernel, n_groups=n_groups),
        grid=(t // tm,),
        in_specs=[row] * (n_groups + len(lses) + 1) + [_resident(wo.shape), _resident((1, D_MODEL))],
        out_specs=row,
        out_shape=jax.ShapeDtypeStruct((t, D_MODEL), F32),
        compiler_params=_params(("parallel",)),
        name="oproj",
    )(*os_, *lses, x, wo, g)


def _merge_groups(o_refs, lse_refs, unperm_refs, expand_ref, tm, dils):
    os_, lses, k = [], [], 0
    for gi, dil in enumerate(dils):
        if dil == 1:
            os_.append(o_refs[gi][0].astype(F32))
            lses.append(lse_refs[gi][0])
            continue
        sel = unperm_refs[k][...]
        k += 1
        n = tm // 2 // dil
        o_halves, lse_halves = [], []
        for hh in range(2):
            rows = slice(hh * n, (hh + 1) * n)
            o = jnp.concatenate([o_refs[gi][r, rows, :] for r in range(dil)], axis=0)
            lse = jnp.concatenate([lse_refs[gi][r, rows, :] for r in range(dil)], axis=0)
            o_halves.append(jnp.dot(sel, o, preferred_element_type=F32))
            lse_halves.append(_onehot_dot(sel, lse, 3))
        os_.append(jnp.concatenate(o_halves, axis=0))
        lses.append(jnp.concatenate(lse_halves, axis=0))
    m = functools.reduce(jnp.maximum, lses)
    es = [jnp.exp(l - m) for l in lses]
    den = functools.reduce(jnp.add, es)
    ws = [jnp.dot((e / den).astype(BF16), expand_ref[...], preferred_element_type=F32) for e in es]
    return functools.reduce(jnp.add, [w * o for w, o in zip(ws, os_)]).astype(BF16)


def _oproj_folded_kernel(*refs, dils):
    ng = len(dils)
    o_refs, lse_refs = refs[:ng], refs[ng:2 * ng]
    unperm_refs = refs[2 * ng:2 * ng + sum(d > 1 for d in dils)]
    expand_ref, x_ref, wo_ref, g_ref, out_ref = refs[-5:]
    o = _merge_groups(o_refs, lse_refs, unperm_refs, expand_ref, x_ref.shape[0], dils)
    mix = jnp.dot(o, wo_ref[...], preferred_element_type=F32)
    out_ref[...] = x_ref[...] + _rms_scale(mix) * g_ref[...]


def _oproj_folded(os_, lses, unperms, expand, x, wo, g):
    nb, seq, _ = x.shape
    tm = ROW_TILE
    dils = tuple(o.shape[1] for o in os_)
    folded = lambda dil, w: pl.BlockSpec((None, dil, tm // dil, w), lambda b, i: (b, 0, i, 0))
    row = pl.BlockSpec((None, tm, D_MODEL), lambda b, i: (b, i, 0))
    in_specs = ([folded(d, D_MODEL) for d in dils] + [folded(d, LANES) for d in dils]
                + [_resident(u.shape) for u in unperms]
                + [_resident(expand.shape), row, _resident(wo.shape), _resident((1, D_MODEL))])
    return pl.pallas_call(
        functools.partial(_oproj_folded_kernel, dils=dils),
        grid=(nb, seq // tm),
        in_specs=in_specs,
        out_specs=row,
        out_shape=jax.ShapeDtypeStruct((nb, seq, D_MODEL), F32),
        compiler_params=_params(("parallel", "parallel")),
        name="oproj_folded",
    )(*os_, *lses, *unperms, expand, x, wo, g)


def _rope_tabs(pos):
    half = ROT_DIM // 2
    pos = np.asarray(pos, np.float32)
    inv = np.float32(ROPE_THETA) ** (-np.arange(0, ROT_DIM, 2, dtype=np.float32) / np.float32(ROT_DIM))
    ang = pos[:, None] * inv[None, :]
    cos, sin = np.cos(ang), np.sin(ang)
    n = pos.shape[0]
    pad = np.zeros((n, HEAD_DIM - ROT_DIM), np.float32)
    zer = np.zeros((n, half), np.float32)
    c = np.concatenate([cos, cos, pad + 1.0], axis=1)
    s1 = np.concatenate([-sin, zer, pad], axis=1)
    s2 = np.concatenate([zer, sin, pad], axis=1)
    return tuple(jnp.asarray(np.tile(t, (1, LANES // HEAD_DIM)), F32) for t in (c, s1, s2))


def _band_bias():
    i = np.arange(QBLK)[:, None]
    j = np.arange(QBLK)[None, :]
    prev_ok = j >= i
    cur_ok = j <= i
    later = np.concatenate([prev_ok, cur_ok], axis=1)
    first = np.concatenate([np.zeros_like(prev_ok), cur_ok], axis=1)
    b = np.where(np.stack([first, later]), 0.0, NEG).astype(np.float32)
    return jnp.asarray(np.tile(b, (1, 2, 1)))


def _fold_perm(dil):
    half = ROW_TILE // 2
    n = half // dil
    dst = np.arange(half)
    src = (dst % n) * dil + dst // n
    return (src[:, None] == np.arange(half)[None, :]).astype(np.float32)


def _head_expand():
    sel = np.arange(LANES)[:, None] == _lse_lane(np.arange(D_MODEL)[None, :] // HEAD_DIM)
    return jnp.asarray(sel.astype(np.float32), BF16)


def _block_diag_q(q):
    r = q.shape[0]
    q4 = q.astype(F32).reshape(r, N_KV, 4, HEAD_DIM).transpose(0, 2, 1, 3)[:, :, None]
    eye = jnp.eye(N_KV, dtype=F32)[None, None, :, :, None]
    return (q4 * eye).reshape(r, N_HEADS, D_KV)


def _own_head(o):
    r = o.shape[0]
    o5 = o.reshape(r, 4, N_KV, N_KV, HEAD_DIM)
    d = jnp.einsum('rigkd,gk->rigd', o5, jnp.eye(N_KV, dtype=o.dtype))
    return d.transpose(0, 2, 1, 3).reshape(r, D_MODEL)


def _expand_lse(lse):
    r = lse.shape[0]
    l = lse[:, :, 0].reshape(r, 4, N_KV).transpose(0, 2, 1).reshape(r, N_HEADS)
    return jnp.repeat(l, HEAD_DIM, axis=1)


def _window_minor(cache):
    r, w = cache.shape[:2]
    return cache.transpose(0, 2, 3, 4, 1).reshape(r, 2 * D_KV, w)


def _window_major(cache_t):
    r, _, w = cache_t.shape
    return cache_t.reshape(r, 2, N_KV, HEAD_DIM, w).transpose(0, 4, 1, 2, 3)


def kernel(x_prompt, x_sample, cache_a_kv, cache_b_kv_w128, cache_b_kv_w512, cache_b_kv_w2048,
           norm_g, w_ffn_gu, w_ffn_dn, w_qkv_a, sink_a, w_o_a, g_kv_b, w_kv_b, w_q_b, w_o_b):
    nb, seq, _ = x_prompt.shape
    nreq = x_sample.shape[0]
    tp = nb * seq
    b_caches = (cache_b_kv_w128, cache_b_kv_w512, cache_b_kv_w2048)

    def gain(l, k):
        return norm_g[l, k].reshape(1, D_MODEL)

    ffn_w = {(1, 1): _cast_weights(w_ffn_gu, w_ffn_dn, 1, 1)}

    def ffn(x, l, k):
        if (l, k) in ffn_w:
            return _ffn(x, gain(l, 4 * k), *ffn_w[l, k], gain(l, 4 * k + 1))
        y, *ffn_w[l, k] = _ffn_cast(x, gain(l, 4 * k), w_ffn_gu, w_ffn_dn, gain(l, 4 * k + 1), l, k)
        return y

    bias = _band_bias()
    tabs_p = _rope_tabs(np.arange(seq))
    tabs_s = _rope_tabs(np.full((nreq,), PAST_LEN))
    perms = {dil: _fold_perm(dil) for _, dil in B_GROUPS if dil > 1}

    wqkv_a = w_qkv_a.astype(BF16)
    wq_b = w_q_b.astype(BF16)
    wkv_b = w_kv_b.astype(BF16)
    col3 = lambda w, j: pl.BlockSpec((None, D_MODEL, w), lambda *_: (0, 0, j), pipeline_mode=pl.Buffered(1))
    col2 = lambda w, j: pl.BlockSpec((D_MODEL, w), lambda *_: (0, j), pipeline_mode=pl.Buffered(1))
    wo_a = w_o_a[0].astype(BF16)
    wo_b = w_o_b[0].astype(BF16)
    g_kv = g_kv_b.reshape(1, D_MODEL)
    sink = sink_a[0].astype(F32)
    sink_rows = sink.reshape(N_KV, 4).T.reshape(N_HEADS, 1)

    assert all(win == dil * LANES for win, dil in B_GROUPS + ((WIN_A, 1),))
    xs = x_sample.reshape(nreq, D_MODEL)
    xs = ffn(xs, 0, 0)
    q, new, newt = _proj_decode(xs, gain(0, 2), gain(0, 2), wqkv_a, col3(D_MODEL, 0),
                                wqkv_a, col3(2 * D_KV, 2), tabs_s)
    o, _, cache_a_new = _decode_attention(_block_diag_q(q), _window_minor(cache_a_kv[0]),
                                          new.reshape(nreq, 1, 2 * D_KV), newt, sink_rows)
    xs = _oproj([_own_head(o)], [], xs, wo_a, gain(0, 3))
    xs = ffn(xs, 0, 1)
    xs = ffn(xs, 1, 0)
    jobs = []
    for gi, (win, dil) in enumerate(B_GROUPS):
        q, new, newt = _proj_decode(xs, gain(1, 2), g_kv, wq_b, col3(D_MODEL, gi),
                                    wkv_b, col2(2 * D_KV, gi), tabs_s)
        jobs.append((_block_diag_q(q), _window_minor(b_caches[gi]), new.reshape(nreq, 1, 2 * D_KV), newt))

    dec_o, dec_lse, dec_caches = [], [], None

    def ffn_with_decode(x, l, k, mixer=None):
        nonlocal dec_caches
        req0 = len(dec_o) * (x.shape[0] // ROW_TILE)
        x, o, lse, dec_caches = _ffn_decode(x, gain(l, 4 * k), *ffn_w[l, k], gain(l, 4 * k + 1),
                                            jobs, req0, dec_caches, mixer)
        dec_o.append(o)
        dec_lse.append(lse)
        return x

    x = x_prompt.reshape(tp, D_MODEL)
    x = ffn_with_decode(x, 0, 0)
    q, kv_a, k2, v2 = _proj(x.reshape(nb, seq, D_MODEL), gain(0, 2), gain(0, 2),
                            wqkv_a, col3(D_MODEL, 0), wqkv_a, col3(2 * D_KV, 2), tabs_p, None,
                            dil=1, win=WIN_A)
    o, = _band_attention(q, k2, v2, bias, sink, has_lse=False)
    x = ffn_with_decode(x, 0, 1, ([o], [], [], None, wo_a, gain(0, 3)))
    x = ffn_with_decode(x, 1, 0).reshape(nb, seq, D_MODEL)
    os_, lses, kv_b = [], [], []
    for gi, (win, dil) in enumerate(B_GROUPS):
        perm = jnp.asarray(perms[dil], BF16) if dil > 1 else None
        q, kvc, k2, v2 = _proj(x, gain(1, 2), g_kv, wq_b, col3(D_MODEL, gi), wkv_b, col2(2 * D_KV, gi),
                               tabs_p, perm, dil=dil, win=win)
        o, lse = _band_attention(q, k2, v2, bias, None, has_lse=True)
        os_.append(o)
        lses.append(lse)
        kv_b.append(kvc.reshape(nb, win, 2, N_KV, HEAD_DIM))
    unperms = [jnp.asarray(perms[d].T, BF16) for _, d in B_GROUPS if d > 1]
    x = _oproj_folded(os_, lses, unperms, _head_expand(), x, wo_b, gain(1, 3))
    y_p = ffn_with_decode(x.reshape(tp, D_MODEL), 1, 1).reshape(nb, seq, D_MODEL)
    a_p = kv_a.reshape(1, nb, WIN_A, 2, N_KV, HEAD_DIM)

    assert len(dec_o) * (tp // ROW_TILE) == nreq
    os_ = [_own_head(jnp.concatenate([o[gi] for o in dec_o], axis=0)) for gi in range(len(B_GROUPS))]
    lses = [_expand_lse(jnp.concatenate([l[gi] for l in dec_lse], axis=0)) for gi in range(len(B_GROUPS))]
    b_s = [_window_major(c) for c in dec_caches]
    xs = _oproj(os_, lses, xs, wo_b, gain(1, 3))
    y_s = ffn(xs, 1, 1).reshape(nreq, 1, D_MODEL)
    a_s = _window_major(cache_a_new)[None]

    return (y_p, y_s, a_p, kv_b[0], kv_b[1], kv_b[2], a_s, b_s[0], b_s[1], b_s[2])
```
